```python
import math
import jax, jax.numpy as jnp
from jax import lax
import numpy as np

D_MODEL = 2048
BATCH = 2
SEQ = 8192
DEPTH = 4
DEC_BATCH = 16
DEC_SEQ = 2048
PAST_LEN = 128

HEAD_DIM = 128
D_FF = 5632
EPS = 1e-6
A_HEADS = 8
A_KV_HEADS = 2
WINDOW = 128
WIN_BLOCK = 128
T5_BUCKETS = 32
T5_MAX_DIST = 128
B_HEADS = 4
GRID_W = 64
NA_ROWS = 8
NA_COLS = 16
C_HEADS = 4
C_KEY_DIM = 128
C_VAL_DIM = 128
C_CHUNK = 64
A_Q = A_HEADS * HEAD_DIM
A_KV = A_KV_HEADS * HEAD_DIM
B_W = B_HEADS * HEAD_DIM
C_K = C_HEADS * C_KEY_DIM
C_V = C_HEADS * C_VAL_DIM
SPLIT_SIZES = (A_Q, A_KV, A_KV, B_W, B_W, B_W, C_K, C_K, C_V, C_K, C_V, D_MODEL, D_MODEL, D_MODEL)
PROJ_WIDTH = sum(SPLIT_SIZES)

kernel_name = "hybrid_bidir_gated_encoder"


def rmsnorm(x, gain):
    xf = x.astype(jnp.float32)
    y = xf * lax.rsqrt(jnp.mean(xf * xf, axis=-1, keepdims=True) + EPS)
    return (y * gain.astype(jnp.float32)).astype(x.dtype)


def swiglu(x, w_gate, w_up, w_down):
    return (jax.nn.silu(x @ w_gate) * (x @ w_up)) @ w_down


def t5_bucket(rel):
    nb = T5_BUCKETS // 2
    max_exact = nb // 2
    base = jnp.where(rel > 0, nb, 0)
    n = jnp.abs(rel)
    large = max_exact + (jnp.log(jnp.maximum(n, 1).astype(jnp.float32) / max_exact)
                         / math.log(T5_MAX_DIST / max_exact) * (nb - max_exact)).astype(jnp.int32)
    large = jnp.minimum(large, nb - 1)
    return base + jnp.where(n < max_exact, n, large)


def window_attention(q, k, v, sink, t5_bias):
    B, T = q.shape[0], q.shape[1]
    nb = T // WIN_BLOCK
    G = A_HEADS // A_KV_HEADS
    pad = ((0, 0), (WIN_BLOCK, WIN_BLOCK), (0, 0), (0, 0))

    def band(a):
        ap = jnp.pad(a, pad).reshape(B, nb + 2, WIN_BLOCK, A_KV_HEADS, HEAD_DIM)
        return jnp.concatenate([ap[:, :-2], ap[:, 1:-1], ap[:, 2:]], axis=2)

    kb, vb = band(k), band(v)
    qb = q.reshape(B, nb, WIN_BLOCK, A_KV_HEADS, G, HEAD_DIM)
    s = jnp.einsum('bnqkgd,bnskd->bnkgqs', qb, kb).astype(jnp.float32) * (HEAD_DIM ** -0.5)
    qi = jnp.arange(WIN_BLOCK)[:, None]
    si = jnp.arange(3 * WIN_BLOCK)[None, :]
    rel = si - WIN_BLOCK - qi
    bias = t5_bias.astype(jnp.float32)[t5_bucket(rel)]
    bias = bias.transpose(2, 0, 1).reshape(A_KV_HEADS, G, WIN_BLOCK, 3 * WIN_BLOCK)
    key_pos = jnp.arange(nb)[:, None] * WIN_BLOCK + si - WIN_BLOCK
    valid = (jnp.abs(rel) <= WINDOW)[None] & ((key_pos >= 0) & (key_pos < T))[:, None, :]
    s = jnp.where(valid[None, :, None, None], s + bias, -jnp.inf)
    sink_b = sink.astype(jnp.float32).reshape(A_KV_HEADS, G)[None, None, :, :, None, None]
    m = jnp.maximum(jnp.max(s, axis=-1, keepdims=True), sink_b)
    e = jnp.exp(s - m)
    p = e / (jnp.sum(e, axis=-1, keepdims=True) + jnp.exp(sink_b - m))
    o = jnp.einsum('bnkgqs,bnskd->bnqkgd', p.astype(v.dtype), vb)
    return o.reshape(B, T, A_Q)


def neighbourhood_attention(q, k, v, rel_table):
    B, T = q.shape[0], q.shape[1]
    rows = T // GRID_W
    kr = min(NA_ROWS, rows)
    r = jnp.arange(rows)
    row_start = jnp.clip(r - kr // 2, 0, rows - kr)
    row_idx = row_start[:, None] + jnp.arange(kr)[None, :]
    c = jnp.arange(GRID_W)
    col_start = jnp.clip(c - NA_COLS // 2, 0, GRID_W - NA_COLS)
    col_ok = (c[None, :] >= col_start[:, None]) & (c[None, :] < col_start[:, None] + NA_COLS)
    qg = q.reshape(B, rows, GRID_W, B_HEADS, HEAD_DIM)
    kg = k.reshape(B, rows, GRID_W, B_HEADS, HEAD_DIM)[:, row_idx].reshape(B, rows, kr * GRID_W, B_HEADS, HEAD_DIM)
    vg = v.reshape(B, rows, GRID_W, B_HEADS, HEAD_DIM)[:, row_idx].reshape(B, rows, kr * GRID_W, B_HEADS, HEAD_DIM)
    s = jnp.einsum('brqhd,brkhd->bhrqk', qg, kg).astype(jnp.float32) * (HEAD_DIM ** -0.5)
    dr = row_idx - r[:, None] + NA_ROWS - 1
    dc = jnp.clip(c[None, :] - c[:, None], -(NA_COLS - 1), NA_COLS - 1) + NA_COLS - 1
    bias = rel_table.astype(jnp.float32)[:, dr[:, None, :, None], dc[None, :, None, :]]
    bias = bias.reshape(B_HEADS, rows, GRID_W, kr * GRID_W)
    mask = jnp.broadcast_to(col_ok[:, None, :], (GRID_W, kr, GRID_W)).reshape(GRID_W, kr * GRID_W)
    s = jnp.where(mask, s + bias, -jnp.inf)
    p = jax.nn.softmax(s, axis=-1)
    o = jnp.einsum('bhrqk,brkhd->brqhd', p.astype(v.dtype), vg)
    return o.reshape(B, T, B_W)


def chunk_gated_recurrence(q, k, v, log_f):
    B, T, H, DK = q.shape
    DV = v.shape[-1]
    n = T // C_CHUNK

    def to_chunks(a):
        return a.reshape(B, n, C_CHUNK, H, a.shape[-1]).transpose(1, 0, 3, 2, 4)

    lower = jnp.tril(jnp.ones((C_CHUNK, C_CHUNK), dtype=bool))

    def step(S, inp):
        qc, kc, vc, gc = inp
        b = jnp.cumsum(gc, axis=2)
        diff = b[:, :, :, None, :] - b[:, :, None, :, :]
        decay = jnp.exp(jnp.where(lower[:, :, None], diff, -jnp.inf))
        A = jnp.einsum('bhtd,bhsd,bhtsd->bhts', qc, kc, decay)
        o = jnp.einsum('bhts,bhse->bhte', A, vc) + jnp.einsum('bhtd,bhde->bhte', qc * jnp.exp(b), S)
        b_last = b[:, :, -1:, :]
        S = jnp.exp(b_last[:, :, 0, :])[..., None] * S + jnp.einsum('bhsd,bhse->bhde', kc * jnp.exp(b_last - b), vc)
        return S, o

    S0 = jnp.zeros((B, H, DK, DV), jnp.float32)
    _, o = lax.scan(step, S0, (to_chunks(q), to_chunks(k), to_chunks(v), to_chunks(log_f)))
    return o.transpose(1, 0, 3, 2, 4).reshape(B, T, H, DV)


def hgrn2_bidirectional(f_fwd, f_bwd, i, q, g, lb_fwd, lb_bwd, norm_gain):
    B, T = q.shape[0], q.shape[1]

    def heads(a):
        return a.reshape(B, T, C_HEADS, -1).astype(jnp.float32)

    def forget(z, lb):
        lb = lb.reshape(C_HEADS, C_KEY_DIM)
        f = lb + (1.0 - lb) * jax.nn.sigmoid(heads(z))
        return jnp.log(f), 1.0 - f

    qh = heads(q) * (C_KEY_DIM ** -0.5)
    vh = heads(i)
    lf_f, k_f = forget(f_fwd, lb_fwd)
    lf_b, k_b = forget(f_bwd, lb_bwd)
    o_f = chunk_gated_recurrence(qh, k_f, vh, lf_f)
    flip = lambda a: jnp.flip(a, axis=1)
    o_b = flip(chunk_gated_recurrence(flip(qh), flip(k_b), flip(vh), flip(lf_b)))
    o = rmsnorm(o_f + o_b, norm_gain.reshape(C_HEADS, C_VAL_DIM)) * jax.nn.silu(heads(g))
    return o.reshape(B, T, C_V).astype(q.dtype)


def trunk(x, ffn1_norm, ffn1_w_gate, ffn1_w_up, ffn1_w_down, mix_norm, w_in, attn_sink, t5_bias,
          na_bias, hgrn_lb_logits, hgrn_norm, w_branch_a, w_branch_b, w_branch_c, w_out,
          ffn2_norm, ffn2_w_gate, ffn2_w_up, ffn2_w_down, final_norm):
    B, T = x.shape[0], x.shape[1]
    lb_p = jax.nn.softmax(hgrn_lb_logits.astype(jnp.float32), axis=1)
    lower_bounds = jnp.cumsum(lb_p, axis=1) - lb_p[:, :1]
    offsets = [int(o) for o in np.cumsum(SPLIT_SIZES)[:-1]]
    for l in range(DEPTH):
        h = x + 0.5 * swiglu(rmsnorm(x, ffn1_norm[l]), ffn1_w_gate[l], ffn1_w_up[l], ffn1_w_down[l])
        u = rmsnorm(h, mix_norm[l])
        (a_q, a_k, a_v, b_q, b_k, b_v, c_ff, c_fb, c_i, c_q, c_g,
         gate_a, gate_b, gate_c) = jnp.split(u @ w_in[l], offsets, axis=-1)
        y_a = window_attention(a_q.reshape(B, T, A_HEADS, HEAD_DIM),
                               a_k.reshape(B, T, A_KV_HEADS, HEAD_DIM),
                               a_v.reshape(B, T, A_KV_HEADS, HEAD_DIM), attn_sink[l], t5_bias)
        y_b = neighbourhood_attention(b_q.reshape(B, T, B_HEADS, HEAD_DIM),
                                      b_k.reshape(B, T, B_HEADS, HEAD_DIM),
                                      b_v.reshape(B, T, B_HEADS, HEAD_DIM), na_bias[l])
        y_c = hgrn2_bidirectional(c_ff, c_fb, c_i, c_q, c_g, lower_bounds[0, l], lower_bounds[1, l], hgrn_norm[l])
        merged = (jax.nn.sigmoid(gate_a) * (y_a @ w_branch_a[l])
                  + jax.nn.sigmoid(gate_b) * (y_b @ w_branch_b[l])
                  + jax.nn.sigmoid(gate_c) * (y_c @ w_branch_c[l]))
        h = h + merged @ w_out[l]
        x = h + 0.5 * swiglu(rmsnorm(h, ffn2_norm[l]), ffn2_w_gate[l], ffn2_w_up[l], ffn2_w_down[l])
    return rmsnorm(x, final_norm)


def setup_inputs(seed: int = 0) -> dict:
    key = jax.random.key(seed)
    ks = jax.random.split(key, 24)
    f32 = jnp.float32
    nrm = lambda k, shape, scale: jax.random.normal(k, shape, f32) * scale
    gain = lambda k, shape: 1.0 + 0.02 * jax.random.normal(k, shape, f32)
    return {
        "x_prompt": jax.random.normal(ks[0], (BATCH, SEQ, D_MODEL), f32),
        "x_sample": jax.random.normal(ks[1], (DEC_BATCH, DEC_SEQ, D_MODEL), f32),
        "ffn1_norm": gain(ks[2], (DEPTH, D_MODEL)),
        "ffn1_w_gate": nrm(ks[3], (DEPTH, D_MODEL, D_FF), D_MODEL ** -0.5),
        "ffn1_w_up": nrm(ks[4], (DEPTH, D_MODEL, D_FF), D_MODEL ** -0.5),
        "ffn1_w_down": nrm(ks[5], (DEPTH, D_FF, D_MODEL), D_FF ** -0.5),
        "mix_norm": gain(ks[6], (DEPTH, D_MODEL)),
        "w_in": nrm(ks[7], (DEPTH, D_MODEL, PROJ_WIDTH), D_MODEL ** -0.5),
        "attn_sink": nrm(ks[8], (DEPTH, A_HEADS), 0.5),
        "t5_bias": nrm(ks[9], (T5_BUCKETS, A_HEADS), 0.1),
        "na_bias": nrm(ks[10], (DEPTH, B_HEADS, 2 * NA_ROWS - 1, 2 * NA_COLS - 1), 0.1),
        "hgrn_lb_logits": nrm(ks[11], (2, DEPTH, C_K), 0.1),
        "hgrn_norm": gain(ks[12], (DEPTH, C_V)),
        "w_branch_a": nrm(ks[13], (DEPTH, A_Q, D_MODEL), A_Q ** -0.5),
        "w_branch_b": nrm(ks[14], (DEPTH, B_W, D_MODEL), B_W ** -0.5),
        "w_branch_c": nrm(ks[15], (DEPTH, C_V, D_MODEL), C_V ** -0.5),
        "w_out": nrm(ks[16], (DEPTH, D_MODEL, D_MODEL), D_MODEL ** -0.5),
        "ffn2_norm": gain(ks[17], (DEPTH, D_MODEL)),
        "ffn2_w_gate": nrm(ks[18], (DEPTH, D_MODEL, D_FF), D_MODEL ** -0.5),
        "ffn2_w_up": nrm(ks[19], (DEPTH, D_MODEL, D_FF), D_MODEL ** -0.5),
        "ffn2_w_down": nrm(ks[20], (DEPTH, D_FF, D_MODEL), D_FF ** -0.5),
        "final_norm": gain(ks[21], (D_MODEL,)),
    }


def reference(x_prompt, x_sample, ffn1_norm, ffn1_w_gate, ffn1_w_up, ffn1_w_down, mix_norm, w_in,
              attn_sink, t5_bias, na_bias, hgrn_lb_logits, hgrn_norm, w_branch_a, w_branch_b,
              w_branch_c, w_out, ffn2_norm, ffn2_w_gate, ffn2_w_up, ffn2_w_down, final_norm):
    y_prompt = trunk(x_prompt, ffn1_norm, ffn1_w_gate, ffn1_w_up, ffn1_w_down, mix_norm, w_in, attn_sink,
                     t5_bias, na_bias, hgrn_lb_logits, hgrn_norm, w_branch_a, w_branch_b, w_branch_c,
                     w_out, ffn2_norm, ffn2_w_gate, ffn2_w_up, ffn2_w_down, final_norm)
    y_sample = trunk(x_sample, ffn1_norm, ffn1_w_gate, ffn1_w_up, ffn1_w_down, mix_norm, w_in, attn_sink,
                     t5_bias, na_bias, hgrn_lb_logits, hgrn_norm, w_branch_a, w_branch_b, w_branch_c,
                     w_out, ffn2_norm, ffn2_w_gate, ffn2_w_up, ffn2_w_down, final_norm)
    return (y_prompt, y_sample)
```

```python
import functools
import math

import numpy as np
import jax
import jax.numpy as jnp
from jax import lax
from jax.experimental import pallas as pl
from jax.experimental.pallas import tpu as pltpu

F32 = jnp.float32
BF16 = jnp.bfloat16

HEAD_DIM = 128
EPS = 1e-6
A_HEADS = 8
A_KV_HEADS = 2
WINDOW = 128
WIN_BLOCK = 128
T5_BUCKETS = 32
T5_MAX_DIST = 128
B_HEADS = 4
GRID_W = 64
NA_ROWS = 8
NA_COLS = 16
C_HEADS = 4
COL_A_Q, COL_A_K, COL_A_V = 0, 8, 10
COL_B_Q, COL_B_K, COL_B_V = 12, 16, 20
COL_C_FF, COL_C_FB, COL_C_I, COL_C_Q, COL_C_G = 24, 28, 32, 36, 40
MIX_COLS = 44

NEG_BIG = -1e30
V7X_VMEM_LIMIT = 56 * 1024 * 1024
HG_TILE = 128
HG_BLK = 16
NA_GROUP = 4 * GRID_W


def _cparams(sem):
    return pltpu.CompilerParams(dimension_semantics=sem, vmem_limit_bytes=V7X_VMEM_LIMIT)


def _seq_pos(gb, blk, segs):
    (n_seq0, t0), (_, t1) = segs
    n0, n1 = t0 // blk, t1 // blk
    tot0 = n_seq0 * n0
    in0 = gb < tot0
    local = jnp.where(in0, gb % n0, (gb - tot0) % n1)
    return local, jnp.where(in0, n0, n1)


def _ffn_kernel(x_ref, g1_ref, wg_ref, wu_ref, wd_ref, g2_ref, o_ref, o2_ref, n_ref, acc_ref):
    j = pl.program_id(1)

    @pl.when(j == 0)
    def _():
        x = x_ref[...]
        ms = jnp.mean(x * x, axis=-1, keepdims=True)
        n_ref[...] = (x * lax.rsqrt(ms + EPS) * g1_ref[...]).astype(BF16)
        acc_ref[...] = jnp.zeros_like(acc_ref)

    n = n_ref[...]
    g = jnp.dot(n, wg_ref[...], preferred_element_type=F32)
    u = jnp.dot(n, wu_ref[...], preferred_element_type=F32)
    a = (g * jax.nn.sigmoid(g) * u).astype(BF16)
    acc_ref[...] += jnp.dot(a, wd_ref[...], preferred_element_type=F32)

    @pl.when(j == pl.num_programs(1) - 1)
    def _():
        y = x_ref[...] + 0.5 * acc_ref[...]
        o_ref[...] = y
        ms = jnp.mean(y * y, axis=-1, keepdims=True)
        o2_ref[...] = (y * lax.rsqrt(ms + EPS) * g2_ref[...]).astype(o2_ref.dtype)


def _ffn(x, gain1, wg, wu, wd, gain2, out2_dtype, tm, tf):
    n_tok, d = x.shape
    d_ff = wg.shape[1]
    return pl.pallas_call(
        _ffn_kernel,
        out_shape=(jax.ShapeDtypeStruct((n_tok, d), F32), jax.ShapeDtypeStruct((n_tok, d), out2_dtype)),
        grid=(n_tok // tm, d_ff // tf),
        in_specs=[
            pl.BlockSpec((tm, d), lambda i, j: (i, 0)),
            pl.BlockSpec((1, d), lambda i, j: (0, 0)),
            pl.BlockSpec((d, tf), lambda i, j: (0, j)),
            pl.BlockSpec((d, tf), lambda i, j: (0, j)),
            pl.BlockSpec((tf, d), lambda i, j: (j, 0)),
            pl.BlockSpec((1, d), lambda i, j: (0, 0)),
        ],
        out_specs=(pl.BlockSpec((tm, d), lambda i, j: (i, 0)), pl.BlockSpec((tm, d), lambda i, j: (i, 0))),
        scratch_shapes=[pltpu.VMEM((tm, d), BF16), pltpu.VMEM((tm, d), F32)],
        compiler_params=_cparams(("parallel", "arbitrary")),
        name="ffn",
    )(x, gain1.reshape(1, d), wg, wu, wd, gain2.reshape(1, d))


def _matmul_kernel(a_ref, w_ref, o_ref):
    o_ref[...] = jnp.dot(a_ref[...], w_ref[...], preferred_element_type=F32).astype(o_ref.dtype)


def _matmul(a, w, out_dtype, tm, tn):
    n_tok, k = a.shape
    n_out = w.shape[1]
    return pl.pallas_call(
        _matmul_kernel,
        out_shape=jax.ShapeDtypeStruct((n_tok, n_out), out_dtype),
        grid=(n_tok // tm, n_out // tn),
        in_specs=[pl.BlockSpec((tm, k), lambda i, j: (i, 0)), pl.BlockSpec((k, tn), lambda i, j: (0, j))],
        out_specs=pl.BlockSpec((tm, tn), lambda i, j: (i, j)),
        compiler_params=_cparams(("parallel", "arbitrary")),
        name="proj_in",
    )(a, w)


def _wattn_kernel(sink_ref, q_ref, kp_ref, kc_ref, kn_ref, vp_ref, vc_ref, vn_ref, bias_ref, o_ref):
    group = A_HEADS // A_KV_HEADS
    scale = HEAD_DIM ** -0.5
    for kv in range(A_KV_HEADS):
        cs = slice(kv * HEAD_DIM, (kv + 1) * HEAD_DIM)
        kcat = jnp.concatenate([kp_ref[:, cs], kc_ref[:, cs], kn_ref[:, cs]], axis=0)
        vcat = jnp.concatenate([vp_ref[:, cs], vc_ref[:, cs], vn_ref[:, cs]], axis=0)
        heads = [kv * group + i for i in range(group)]
        qs = jnp.concatenate([q_ref[:, h * HEAD_DIM:(h + 1) * HEAD_DIM] for h in heads], axis=0)
        s = lax.dot_general(qs, kcat, (((1,), (1,)), ((), ())), preferred_element_type=F32)
        s = s * scale + bias_ref[kv * group * WIN_BLOCK:(kv + 1) * group * WIN_BLOCK, :]
        for i, h in enumerate(heads):
            sh = s[i * WIN_BLOCK:(i + 1) * WIN_BLOCK, :]
            sink = sink_ref[h]
            m = jnp.maximum(jnp.max(sh, axis=-1, keepdims=True), sink)
            e = jnp.exp(sh - m)
            den = jnp.sum(e, axis=-1, keepdims=True) + jnp.exp(sink - m)
            oh = jnp.dot(e.astype(BF16), vcat, preferred_element_type=F32)
            o_ref[:, h * HEAD_DIM:(h + 1) * HEAD_DIM] = (oh / den).astype(o_ref.dtype)


def _window_attention(proj, sink, bias_tab, segs):
    n_tok = proj.shape[0]
    nb = n_tok // WIN_BLOCK

    def first_last(gb):
        local, n_loc = _seq_pos(gb, WIN_BLOCK, segs)
        return local == 0, local == n_loc - 1

    def bias_map(gb):
        first, last = first_last(gb)
        return (jnp.where(first, 0, jnp.where(last, 2, 1)), 0, 0)

    def kv_spec(col0, delta):
        return pl.BlockSpec((WIN_BLOCK, A_KV_HEADS * HEAD_DIM),
                            lambda gb: (jnp.clip(gb + delta, 0, nb - 1), col0 // A_KV_HEADS))

    return pl.pallas_call(
        _wattn_kernel,
        out_shape=jax.ShapeDtypeStruct((n_tok, A_HEADS * HEAD_DIM), BF16),
        grid=(nb,),
        in_specs=[
            pl.BlockSpec(memory_space=pltpu.SMEM),
            pl.BlockSpec((WIN_BLOCK, A_HEADS * HEAD_DIM), lambda gb: (gb, COL_A_Q // A_HEADS)),
            kv_spec(COL_A_K, -1), kv_spec(COL_A_K, 0), kv_spec(COL_A_K, 1),
            kv_spec(COL_A_V, -1), kv_spec(COL_A_V, 0), kv_spec(COL_A_V, 1),
            pl.BlockSpec((None, A_HEADS * WIN_BLOCK, 3 * WIN_BLOCK), bias_map),
        ],
        out_specs=pl.BlockSpec((WIN_BLOCK, A_HEADS * HEAD_DIM), lambda gb: (gb, 0)),
        compiler_params=_cparams(("parallel",)),
        name="window_attn",
    )(sink, proj, proj, proj, proj, proj, proj, proj, bias_tab)


def _t5_bucket(rel):
    nb = T5_BUCKETS // 2
    max_exact = nb // 2
    base = jnp.where(rel > 0, nb, 0)
    n = jnp.abs(rel)
    large = max_exact + (jnp.log(jnp.maximum(n, 1).astype(F32) / max_exact)
                         / math.log(T5_MAX_DIST / max_exact) * (nb - max_exact)).astype(jnp.int32)
    large = jnp.minimum(large, nb - 1)
    return base + jnp.where(n < max_exact, n, large)


def _window_bias_table(t5_bias):
    qi = jnp.arange(WIN_BLOCK)[:, None]
    si = jnp.arange(3 * WIN_BLOCK)[None, :]
    rel = si - WIN_BLOCK - qi
    bias = t5_bias.astype(F32)[_t5_bucket(rel)].transpose(2, 0, 1)
    band = jnp.abs(rel) <= WINDOW
    variants = []
    for lo_ok, hi_ok in ((False, True), (True, True), (True, False)):
        ok = band & ((si >= WIN_BLOCK) | lo_ok) & ((si < 2 * WIN_BLOCK) | hi_ok)
        variants.append(jnp.where(ok[None], bias, NEG_BIG).reshape(A_HEADS * WIN_BLOCK, 3 * WIN_BLOCK))
    return jnp.stack(variants)


def _na_window(g, segs):
    lg, ng = _seq_pos(g, NA_GROUP, segs)
    return lg, ng, jnp.clip(lg - 1, 0, ng - 3)


def _na_kernel(q_ref, k0_ref, k1_ref, k2_ref, v0_ref, v1_ref, v2_ref, bias_ref, o_ref, kbuf, vbuf, *, segs):
    g = pl.program_id(0)
    lg, ng, lo = _na_window(g, segs)
    rows = ng * (NA_GROUP // GRID_W)
    scale = HEAD_DIM ** -0.5
    for i, (kr, vr) in enumerate(((k0_ref, v0_ref), (k1_ref, v1_ref), (k2_ref, v2_ref))):
        kbuf[i * NA_GROUP:(i + 1) * NA_GROUP, :] = kr[...]
        vbuf[i * NA_GROUP:(i + 1) * NA_GROUP, :] = vr[...]
    n_keys = NA_ROWS * GRID_W
    for a in range(NA_GROUP // GRID_W):
        r = lg * (NA_GROUP // GRID_W) + a
        rs = jnp.clip(r - NA_ROWS // 2, 0, rows - NA_ROWS)
        off = pl.multiple_of((rs - lo * (NA_GROUP // GRID_W)) * GRID_W, GRID_W)
        dr0 = rs - r + NA_ROWS - 1
        for h in range(B_HEADS):
            cs = slice(h * HEAD_DIM, (h + 1) * HEAD_DIM)
            qh = q_ref[a * GRID_W:(a + 1) * GRID_W, cs]
            kw = kbuf[pl.ds(off, n_keys), cs]
            vw = vbuf[pl.ds(off, n_keys), cs]
            s = lax.dot_general(qh, kw, (((1,), (1,)), ((), ())), preferred_element_type=F32)
            s = s * scale + bias_ref[dr0, h]
            m = jnp.max(s, axis=-1, keepdims=True)
            e = jnp.exp(s - m)
            den = jnp.sum(e, axis=-1, keepdims=True)
            oh = jnp.dot(e.astype(BF16), vw, preferred_element_type=F32)
            o_ref[a * GRID_W:(a + 1) * GRID_W, cs] = (oh / den).astype(o_ref.dtype)


def _neighbourhood_attention(proj, bias_tab, segs):
    n_tok = proj.shape[0]
    width = B_HEADS * HEAD_DIM

    def kv_spec(col0, j):
        def index_map(g):
            lg, _, lo = _na_window(g, segs)
            return (g - lg + lo + j, col0 // B_HEADS)
        return pl.BlockSpec((NA_GROUP, width), index_map)

    return pl.pallas_call(
        functools.partial(_na_kernel, segs=segs),
        out_shape=jax.ShapeDtypeStruct((n_tok, width), BF16),
        grid=(n_tok // NA_GROUP,),
        in_specs=[
            pl.BlockSpec((NA_GROUP, width), lambda g: (g, COL_B_Q // B_HEADS)),
            kv_spec(COL_B_K, 0), kv_spec(COL_B_K, 1), kv_spec(COL_B_K, 2),
            kv_spec(COL_B_V, 0), kv_spec(COL_B_V, 1), kv_spec(COL_B_V, 2),
            pl.BlockSpec((NA_ROWS, B_HEADS, GRID_W, NA_ROWS * GRID_W), lambda g: (0, 0, 0, 0)),
        ],
        out_specs=pl.BlockSpec((NA_GROUP, width), lambda g: (g, 0)),
        scratch_shapes=[pltpu.VMEM((3 * NA_GROUP, width), BF16), pltpu.VMEM((3 * NA_GROUP, width), BF16)],
        compiler_params=_cparams(("parallel",)),
        name="na_attn",
    )(proj, proj, proj, proj, proj, proj, proj, bias_tab)


def _na_bias_table(rel_table):
    c = jnp.arange(GRID_W)
    col_start = jnp.clip(c - NA_COLS // 2, 0, GRID_W - NA_COLS)
    col_ok = (c[None, :] >= col_start[:, None]) & (c[None, :] < col_start[:, None] + NA_COLS)
    dc = jnp.clip(c[None, :] - c[:, None], -(NA_COLS - 1), NA_COLS - 1) + NA_COLS - 1
    dr = jnp.arange(NA_ROWS)[:, None] + jnp.arange(NA_ROWS)[None, :]
    tab = rel_table.astype(F32)[:, dr[:, :, None, None], dc[None, None, :, :]]
    tab = jnp.where(col_ok[None, None, None], tab, NEG_BIG)
    tab = tab.transpose(1, 0, 3, 2, 4)
    return tab.reshape(NA_ROWS, B_HEADS, GRID_W, NA_ROWS * GRID_W)


def _split3(x):
    hi = x.astype(BF16)
    r1 = x - hi.astype(F32)
    mid = r1.astype(BF16)
    lo = (r1 - mid.astype(F32)).astype(BF16)
    return hi, mid, lo


def _hgrn_kernel(z_ref, v_ref, q_ref, lb_ref, scan_ref, wsum_ref, o_ref, st_ref, k3_ref, b3_ref,
                 *, segs, reverse):
    i = pl.program_id(0)
    n_tiles = pl.num_programs(0)
    tile = (n_tiles - 1 - i) if reverse else i
    local, n_loc = _seq_pos(tile, HG_TILE, segs)
    is_start = (local == n_loc - 1) if reverse else (local == 0)

    @pl.when(is_start)
    def _():
        st_ref[...] = jnp.zeros_like(st_ref)

    nblk = HG_TILE // HG_BLK
    row = lax.broadcasted_iota(jnp.int32, (HG_TILE, HG_TILE), 0)
    col = lax.broadcasted_iota(jnp.int32, (HG_TILE, HG_TILE), 1)
    same_blk = (row // HG_BLK) == (col // HG_BLK)
    causal = (col <= row) if not reverse else (col >= row)
    pair_ok = same_blk & causal
    col_blk = col // HG_BLK
    edge_row = 0 if reverse else HG_BLK - 1
    scan = scan_ref[...]
    wsum = wsum_ref[...]

    for h in range(C_HEADS):
        cs = slice(h * HEAD_DIM, (h + 1) * HEAD_DIM)
        lb = lb_ref[:, cs]
        f = lb + (1.0 - lb) * jax.nn.sigmoid(z_ref[:, cs].astype(F32))
        logf = jnp.log(f)
        k = 1.0 - f
        q = q_ref[:, cs].astype(F32) * (HEAD_DIM ** -0.5)
        v = v_ref[:, cs]

        pieces = jnp.concatenate(_split3(logf), axis=1)
        bc = jnp.dot(scan, pieces, preferred_element_type=F32)
        bc = bc[:, :HEAD_DIM] + bc[:, HEAD_DIM:2 * HEAD_DIM] + bc[:, 2 * HEAD_DIM:]
        b = bc[:HG_TILE]
        c = bc[HG_TILE:]

        k3_ref[...] = k.reshape(nblk, HG_BLK, HEAD_DIM)
        b3_ref[...] = b.reshape(nblk, HG_BLK, HEAD_DIM)
        q3 = q.reshape(nblk, HG_BLK, HEAD_DIM)
        b3 = b.reshape(nblk, HG_BLK, HEAD_DIM)
        parts = []
        for j in range(HG_BLK):
            kj = k3_ref[:, j:j + 1, :]
            bj = b3_ref[:, j:j + 1, :]
            p = q3 * kj * jnp.exp(jnp.minimum(b3 - bj, 0.0))
            parts.append(p.reshape(HG_TILE, HEAD_DIM).astype(BF16))
        a_rep = jnp.dot(jnp.concatenate(parts, axis=1), wsum, preferred_element_type=F32)
        a_mat = jnp.where(pair_ok, a_rep, 0.0).astype(BF16)
        o_acc = jnp.dot(a_mat, v, preferred_element_type=F32)

        qp = (q * jnp.exp(b)).astype(BF16)
        kp = (k * jnp.exp(c)).astype(BF16)
        vt = v.astype(F32).T
        decay = jnp.exp(b3_ref[:, edge_row, :])
        inter = [None] * nblk
        for blk in (range(nblk - 1, -1, -1) if reverse else range(nblk)):
            st = st_ref[h]
            inter[blk] = lax.dot_general(qp[blk * HG_BLK:(blk + 1) * HG_BLK, :], st.astype(BF16),
                                         (((1,), (1,)), ((), ())), preferred_element_type=F32)
            upd = jnp.dot(jnp.where(col_blk == blk, vt, 0.0).astype(BF16), kp, preferred_element_type=F32)
            st_ref[h] = st * decay[blk:blk + 1, :] + upd
        o_ref[:, cs] = o_acc + jnp.concatenate(inter, axis=0)


def _hgrn_constants(reverse):
    t = np.arange(HG_TILE)
    same = (t[:, None] // HG_BLK) == (t[None, :] // HG_BLK)
    if reverse:
        incl, rest = t[None, :] >= t[:, None], t[None, :] < t[:, None]
    else:
        incl, rest = t[None, :] <= t[:, None], t[None, :] > t[:, None]
    scan = np.concatenate([same & incl, same & rest], axis=0).astype(np.float32)
    piece = np.repeat(np.arange(HG_BLK), HEAD_DIM)
    wsum = (piece[:, None] == (t[None, :] % HG_BLK)).astype(np.float32)
    return jnp.asarray(scan, BF16), jnp.asarray(wsum, BF16)


def _hgrn_direction(proj, lower_bound, segs, reverse):
    n_tok = proj.shape[0]
    n_tiles = n_tok // HG_TILE
    width = C_HEADS * HEAD_DIM
    scan, wsum = _hgrn_constants(reverse)

    def tok_spec(col0):
        return pl.BlockSpec((HG_TILE, width),
                            lambda i: ((n_tiles - 1 - i) if reverse else i, col0 // C_HEADS))

    nblk = HG_TILE // HG_BLK
    return pl.pallas_call(
        functools.partial(_hgrn_kernel, segs=segs, reverse=reverse),
        out_shape=jax.ShapeDtypeStruct((n_tok, width), F32),
        grid=(n_tiles,),
        in_specs=[
            tok_spec(COL_C_FB if reverse else COL_C_FF), tok_spec(COL_C_I), tok_spec(COL_C_Q),
            pl.BlockSpec((1, width), lambda i: (0, 0)),
            pl.BlockSpec(scan.shape, lambda i: (0, 0)),
            pl.BlockSpec(wsum.shape, lambda i: (0, 0)),
        ],
        out_specs=pl.BlockSpec((HG_TILE, width), lambda i: ((n_tiles - 1 - i) if reverse else i, 0)),
        scratch_shapes=[pltpu.VMEM((C_HEADS, HEAD_DIM, HEAD_DIM), F32),
                        pltpu.VMEM((nblk, HG_BLK, HEAD_DIM), F32),
                        pltpu.VMEM((nblk, HG_BLK, HEAD_DIM), F32)],
        compiler_params=_cparams(("arbitrary",)),
        name="hgrn_bwd" if reverse else "hgrn_fwd",
    )(proj, proj, proj, lower_bound.reshape(1, width), scan, wsum)


def _merge_kernel(ga_ref, gb_ref, gc_ref, cg_ref, ya_ref, yb_ref, of_ref, ob_ref, hn_ref,
                  wa_ref, wb_ref, wc_ref, wo_ref, h_ref, o_ref):
    o = of_ref[...] + ob_ref[...]
    heads = []
    for h in range(C_HEADS):
        oh = o[:, h * HEAD_DIM:(h + 1) * HEAD_DIM]
        ms = jnp.mean(oh * oh, axis=-1, keepdims=True)
        heads.append(oh * lax.rsqrt(ms + EPS))
    cg = cg_ref[...].astype(F32)
    yc = (jnp.concatenate(heads, axis=1) * hn_ref[...] * (cg * jax.nn.sigmoid(cg))).astype(BF16)
    m = jax.nn.sigmoid(ga_ref[...].astype(F32)) * jnp.dot(ya_ref[...], wa_ref[...], preferred_element_type=F32)
    m += jax.nn.sigmoid(gb_ref[...].astype(F32)) * jnp.dot(yb_ref[...], wb_ref[...], preferred_element_type=F32)
    m += jax.nn.sigmoid(gc_ref[...].astype(F32)) * jnp.dot(yc, wc_ref[...], preferred_element_type=F32)
    o_ref[...] = h_ref[...] + jnp.dot(m.astype(BF16), wo_ref[...], preferred_element_type=F32)


def _merge(proj, gates, ya, yb, o_f, o_b, hgrn_gain, wa, wb, wc, wo, h, tm):
    n_tok, d = h.shape
    cw = C_HEADS * HEAD_DIM

    def resident(shape):
        return pl.BlockSpec(shape, lambda i: (0, 0), pipeline_mode=pl.Buffered(1))

    def gate_spec(which):
        return pl.BlockSpec((tm, d), lambda i: (i, which))

    return pl.pallas_call(
        _merge_kernel,
        out_shape=jax.ShapeDtypeStruct((n_tok, d), F32),
        grid=(n_tok // tm,),
        in_specs=[
            gate_spec(0), gate_spec(1), gate_spec(2),
            pl.BlockSpec((tm, cw), lambda i: (i, COL_C_G // C_HEADS)),
            pl.BlockSpec((tm, ya.shape[1]), lambda i: (i, 0)),
            pl.BlockSpec((tm, cw), lambda i: (i, 0)),
            pl.BlockSpec((tm, cw), lambda i: (i, 0)),
            pl.BlockSpec((tm, cw), lambda i: (i, 0)),
            pl.BlockSpec((1, cw), lambda i: (0, 0)),
            resident(wa.shape), resident(wb.shape), resident(wc.shape), resident(wo.shape),
            pl.BlockSpec((tm, d), lambda i: (i, 0)),
        ],
        out_specs=pl.BlockSpec((tm, d), lambda i: (i, 0)),
        compiler_params=_cparams(("parallel",)),
        name="merge_out",
    )(gates, gates, gates, proj, ya, yb, o_f, o_b, hgrn_gain.reshape(1, cw), wa, wb, wc, wo, h)


def _tile(n, want):
    t = want
    while n % t:
        t //= 2
    return t


def _trunk(x, segs, p):
    depth = p["w_in"].shape[0]
    n_tok = x.shape[0]
    tm_ffn, tm_proj, tm_merge = _tile(n_tok, 512), _tile(n_tok, 1024), _tile(n_tok, 256)
    d_ff = p["ffn1_w_gate"].shape[-1]
    tf = 512 if d_ff % 512 == 0 else 256
    mix_w = MIX_COLS * HEAD_DIM

    lb_p = jax.nn.softmax(p["hgrn_lb_logits"].astype(F32), axis=1)
    lower_bounds = jnp.cumsum(lb_p, axis=1) - lb_p[:, :1]
    win_tab = _window_bias_table(p["t5_bias"])
    bf = lambda name: p[name].astype(BF16)
    w = {name: bf(name) for name in ("ffn1_w_gate", "ffn1_w_up", "ffn1_w_down", "w_in", "w_branch_a",
                                      "w_branch_b", "w_branch_c", "w_out", "ffn2_w_gate", "ffn2_w_up",
                                      "ffn2_w_down")}
    out = None
    for l in range(depth):
        h, u = _ffn(x, p["ffn1_norm"][l], w["ffn1_w_gate"][l], w["ffn1_w_up"][l], w["ffn1_w_down"][l],
                    p["mix_norm"][l], BF16, tm_ffn, tf)
        proj = _matmul(u, w["w_in"][l, :, :mix_w], BF16, tm_proj, 512)
        gates = _matmul(u, w["w_in"][l, :, mix_w:], BF16, tm_proj, 512)
        ya = _window_attention(proj, p["attn_sink"][l].astype(F32), win_tab, segs)
        yb = _neighbourhood_attention(proj, _na_bias_table(p["na_bias"][l]), segs)
        o_f = _hgrn_direction(proj, lower_bounds[0, l], segs, reverse=False)
        o_b = _hgrn_direction(proj, lower_bounds[1, l], segs, reverse=True)
        h = _merge(proj, gates, ya, yb, o_f, o_b, p["hgrn_norm"][l], w["w_branch_a"][l], w["w_branch_b"][l],
                   w["w_branch_c"][l], w["w_out"][l], h, tm_merge)
        next_gain = p["final_norm"] if l == depth - 1 else p["ffn1_norm"][l]
        x, out = _ffn(h, p["ffn2_norm"][l], w["ffn2_w_gate"][l], w["ffn2_w_up"][l], w["ffn2_w_down"][l],
                      next_gain, F32, tm_ffn, tf)
    return out


def kernel(x_prompt, x_sample, ffn1_norm, ffn1_w_gate, ffn1_w_up, ffn1_w_down, mix_norm, w_in, attn_sink,
           t5_bias, na_bias, hgrn_lb_logits, hgrn_norm, w_branch_a, w_branch_b, w_branch_c, w_out,
           ffn2_norm, ffn2_w_gate, ffn2_w_up, ffn2_w_down, final_norm):
    params = dict(ffn1_norm=ffn1_norm, ffn1_w_gate=ffn1_w_gate, ffn1_w_up=ffn1_w_up, ffn1_w_down=ffn1_w_down,
                  mix_norm=mix_norm, w_in=w_in, attn_sink=attn_sink, t5_bias=t5_bias, na_bias=na_bias,
                  hgrn_lb_logits=hgrn_lb_logits, hgrn_norm=hgrn_norm, w_branch_a=w_branch_a,
                  w_branch_b=w_branch_b, w_branch_c=w_branch_c, w_out=w_out, ffn2_norm=ffn2_norm,
                  ffn2_w_gate=ffn2_w_gate, ffn2_w_up=ffn2_w_up, ffn2_w_down=ffn2_w_down, final_norm=final_norm)
    d = x_prompt.shape[-1]
    segs = (x_prompt.shape[:2], x_sample.shape[:2])
    for _, t in segs:
        assert t % NA_GROUP == 0 and t // NA_GROUP >= 3 and t // GRID_W >= NA_ROWS
    x = jnp.concatenate([x_prompt.reshape(-1, d), x_sample.reshape(-1, d)], axis=0)
    y = _trunk(x, segs, params)
    n_prompt = x_prompt.shape[0] * x_prompt.shape[1]
    return y[:n_prompt].reshape(x_prompt.shape), y[n_prompt:].reshape(x_sample.shape)
```

```python
import functools
import math

import numpy as np
import jax
import jax.numpy as jnp
from jax import lax
from jax.experimental import pallas as pl
from jax.experimental.pallas import tpu as pltpu

F32 = jnp.float32
BF16 = jnp.bfloat16

HEAD_DIM = 128
EPS = 1e-6
A_HEADS = 8
A_KV_HEADS = 2
WINDOW = 128
WIN_BLOCK = 128
T5_BUCKETS = 32
T5_MAX_DIST = 128
B_HEADS = 4
GRID_W = 64
NA_ROWS = 8
NA_COLS = 16
C_HEADS = 4
COL_A_Q, COL_A_K, COL_A_V = 0, 8, 10
COL_B_Q, COL_B_K, COL_B_V = 12, 16, 20
COL_C_FF, COL_C_FB, COL_C_I, COL_C_Q, COL_C_G = 24, 28, 32, 36, 40
MIX_COLS = 44

NEG_BIG = -1e30
V7X_VMEM_LIMIT = 56 * 1024 * 1024
HG_TILE = 128
HG_BLK = 16
NA_GROUP = 4 * GRID_W


def _cparams(sem):
    return pltpu.CompilerParams(dimension_semantics=sem, vmem_limit_bytes=V7X_VMEM_LIMIT)


def _seq_pos(gb, blk, segs):
    (n_seq0, t0), (_, t1) = segs
    n0, n1 = t0 // blk, t1 // blk
    tot0 = n_seq0 * n0
    in0 = gb < tot0
    local = jnp.where(in0, gb % n0, (gb - tot0) % n1)
    return local, jnp.where(in0, n0, n1)


def _ffn_kernel(x_ref, g1_ref, wg_ref, wu_ref, wd_ref, g2_ref, o_ref, o2_ref, n_ref, acc_ref):
    j = pl.program_id(1)

    @pl.when(j == 0)
    def _():
        x = x_ref[...]
        ms = jnp.mean(x * x, axis=-1, keepdims=True)
        n_ref[...] = (x * lax.rsqrt(ms + EPS) * g1_ref[...]).astype(BF16)
        acc_ref[...] = jnp.zeros_like(acc_ref)

    n = n_ref[...]
    g = jnp.dot(n, wg_ref[...], preferred_element_type=F32)
    u = jnp.dot(n, wu_ref[...], preferred_element_type=F32)
    a = (g * jax.nn.sigmoid(g) * u).astype(BF16)
    acc_ref[...] += jnp.dot(a, wd_ref[...], preferred_element_type=F32)

    @pl.when(j == pl.num_programs(1) - 1)
    def _():
        y = x_ref[...] + 0.5 * acc_ref[...]
        o_ref[...] = y
        ms = jnp.mean(y * y, axis=-1, keepdims=True)
        o2_ref[...] = (y * lax.rsqrt(ms + EPS) * g2_ref[...]).astype(o2_ref.dtype)


def _ffn(x, gain1, wg, wu, wd, gain2, out2_dtype, tm, tf):
    n_tok, d = x.shape
    d_ff = wg.shape[1]
    return pl.pallas_call(
        _ffn_kernel,
        out_shape=(jax.ShapeDtypeStruct((n_tok, d), F32), jax.ShapeDtypeStruct((n_tok, d), out2_dtype)),
        grid=(n_tok // tm, d_ff // tf),
        in_specs=[
            pl.BlockSpec((tm, d), lambda i, j: (i, 0)),
            pl.BlockSpec((1, d), lambda i, j: (0, 0)),
            pl.BlockSpec((d, tf), lambda i, j: (0, j)),
            pl.BlockSpec((d, tf), lambda i, j: (0, j)),
            pl.BlockSpec((tf, d), lambda i, j: (j, 0)),
            pl.BlockSpec((1, d), lambda i, j: (0, 0)),
        ],
        out_specs=(pl.BlockSpec((tm, d), lambda i, j: (i, 0)), pl.BlockSpec((tm, d), lambda i, j: (i, 0))),
        scratch_shapes=[pltpu.VMEM((tm, d), BF16), pltpu.VMEM((tm, d), F32)],
        compiler_params=_cparams(("parallel", "arbitrary")),
        name="ffn",
    )(x, gain1.reshape(1, d), wg, wu, wd, gain2.reshape(1, d))


def _matmul_kernel(a_ref, w_ref, o_ref):
    o_ref[...] = jnp.dot(a_ref[...], w_ref[...], preferred_element_type=F32).astype(o_ref.dtype)


def _matmul(a, w, out_dtype, tm, tn):
    n_tok, k = a.shape
    n_out = w.shape[1]
    return pl.pallas_call(
        _matmul_kernel,
        out_shape=jax.ShapeDtypeStruct((n_tok, n_out), out_dtype),
        grid=(n_tok // tm, n_out // tn),
        in_specs=[pl.BlockSpec((tm, k), lambda i, j: (i, 0)), pl.BlockSpec((k, tn), lambda i, j: (0, j))],
        out_specs=pl.BlockSpec((tm, tn), lambda i, j: (i, j)),
        compiler_params=_cparams(("parallel", "arbitrary")),
        name="proj_in",
    )(a, w)


def _wattn_kernel(sink_ref, q_ref, kp_ref, kc_ref, kn_ref, vp_ref, vc_ref, vn_ref, bias_ref, o_ref):
    group = A_HEADS // A_KV_HEADS
    scale = HEAD_DIM ** -0.5
    for kv in range(A_KV_HEADS):
        cs = slice(kv * HEAD_DIM, (kv + 1) * HEAD_DIM)
        kcat = jnp.concatenate([kp_ref[:, cs], kc_ref[:, cs], kn_ref[:, cs]], axis=0)
        vcat = jnp.concatenate([vp_ref[:, cs], vc_ref[:, cs], vn_ref[:, cs]], axis=0)
        heads = [kv * group + i for i in range(group)]
        qs = jnp.concatenate([q_ref[:, h * HEAD_DIM:(h + 1) * HEAD_DIM] for h in heads], axis=0)
        s = lax.dot_general(qs, kcat, (((1,), (1,)), ((), ())), preferred_element_type=F32)
        s = s * scale + bias_ref[kv * group * WIN_BLOCK:(kv + 1) * group * WIN_BLOCK, :]
        for i, h in enumerate(heads):
            sh = s[i * WIN_BLOCK:(i + 1) * WIN_BLOCK, :]
            sink = sink_ref[h]
            m = jnp.maximum(jnp.max(sh, axis=-1, keepdims=True), sink)
            e = jnp.exp(sh - m)
            den = jnp.sum(e, axis=-1, keepdims=True) + jnp.exp(sink - m)
            oh = jnp.dot(e.astype(BF16), vcat, preferred_element_type=F32)
            o_ref[:, h * HEAD_DIM:(h + 1) * HEAD_DIM] = (oh / den).astype(o_ref.dtype)


def _window_attention(proj, sink, bias_tab, segs):
    n_tok = proj.shape[0]
    nb = n_tok // WIN_BLOCK

    def first_last(gb):
        local, n_loc = _seq_pos(gb, WIN_BLOCK, segs)
        return local == 0, local == n_loc - 1

    def bias_map(gb):
        first, last = first_last(gb)
        return (jnp.where(first, 0, jnp.where(last, 2, 1)), 0, 0)

    def kv_spec(col0, delta):
        return pl.BlockSpec((WIN_BLOCK, A_KV_HEADS * HEAD_DIM),
                            lambda gb: (jnp.clip(gb + delta, 0, nb - 1), col0 // A_KV_HEADS))

    return pl.pallas_call(
        _wattn_kernel,
        out_shape=jax.ShapeDtypeStruct((n_tok, A_HEADS * HEAD_DIM), BF16),
        grid=(nb,),
        in_specs=[
            pl.BlockSpec(memory_space=pltpu.SMEM),
            pl.BlockSpec((WIN_BLOCK, A_HEADS * HEAD_DIM), lambda gb: (gb, COL_A_Q // A_HEADS)),
            kv_spec(COL_A_K, -1), kv_spec(COL_A_K, 0), kv_spec(COL_A_K, 1),
            kv_spec(COL_A_V, -1), kv_spec(COL_A_V, 0), kv_spec(COL_A_V, 1),
            pl.BlockSpec((None, A_HEADS * WIN_BLOCK, 3 * WIN_BLOCK), bias_map),
        ],
        out_specs=pl.BlockSpec((WIN_BLOCK, A_HEADS * HEAD_DIM), lambda gb: (gb, 0)),
        compiler_params=_cparams(("parallel",)),
        name="window_attn",
    )(sink, proj, proj, proj, proj, proj, proj, proj, bias_tab)


def _t5_bucket(rel):
    nb = T5_BUCKETS // 2
    max_exact = nb // 2
    base = jnp.where(rel > 0, nb, 0)
    n = jnp.abs(rel)
    large = max_exact + (jnp.log(jnp.maximum(n, 1).astype(F32) / max_exact)
                         / math.log(T5_MAX_DIST / max_exact) * (nb - max_exact)).astype(jnp.int32)
    large = jnp.minimum(large, nb - 1)
    return base + jnp.where(n < max_exact, n, large)


def _window_bias_table(t5_bias):
    qi = jnp.arange(WIN_BLOCK)[:, None]
    si = jnp.arange(3 * WIN_BLOCK)[None, :]
    rel = si - WIN_BLOCK - qi
    onehot = (_t5_bucket(rel)[None] == jnp.arange(T5_BUCKETS)[:, None, None]).astype(F32)
    bias = jnp.einsum('bh,bqs->hqs', t5_bias.astype(F32), onehot, precision=lax.Precision.HIGHEST)
    band = jnp.abs(rel) <= WINDOW
    variants = []
    for lo_ok, hi_ok in ((False, True), (True, True), (True, False)):
        ok = band & ((si >= WIN_BLOCK) | lo_ok) & ((si < 2 * WIN_BLOCK) | hi_ok)
        variants.append(jnp.where(ok[None], bias, NEG_BIG).reshape(A_HEADS * WIN_BLOCK, 3 * WIN_BLOCK))
    return jnp.stack(variants)


def _na_window(g, segs):
    lg, ng = _seq_pos(g, NA_GROUP, segs)
    return lg, ng, jnp.clip(lg - 1, 0, ng - 3)


def _na_kernel(q_ref, k0_ref, k1_ref, k2_ref, v0_ref, v1_ref, v2_ref, bias_ref, o_ref):
    scale = HEAD_DIM ** -0.5
    for h in range(B_HEADS):
        cs = slice(h * HEAD_DIM, (h + 1) * HEAD_DIM)
        kcat = jnp.concatenate([k0_ref[:, cs], k1_ref[:, cs], k2_ref[:, cs]], axis=0)
        vcat = jnp.concatenate([v0_ref[:, cs], v1_ref[:, cs], v2_ref[:, cs]], axis=0)
        s = lax.dot_general(q_ref[:, cs], kcat, (((1,), (1,)), ((), ())), preferred_element_type=F32)
        s = s * scale + bias_ref[h]
        m = jnp.max(s, axis=-1, keepdims=True)
        e = jnp.exp(s - m)
        den = jnp.sum(e, axis=-1, keepdims=True)
        oh = jnp.dot(e.astype(BF16), vcat, preferred_element_type=F32)
        o_ref[:, cs] = (oh / den).astype(o_ref.dtype)


def _neighbourhood_attention(proj, bias_tab, segs):
    n_tok = proj.shape[0]
    width = B_HEADS * HEAD_DIM

    def kv_spec(col0, j):
        def index_map(g):
            lg, _, lo = _na_window(g, segs)
            return (g - lg + lo + j, col0 // B_HEADS)
        return pl.BlockSpec((NA_GROUP, width), index_map)

    def bias_map(g):
        lg, ng, _ = _na_window(g, segs)
        return (jnp.where(lg == 0, 0, jnp.where(lg == ng - 1, 2, 1)), 0, 0, 0)

    return pl.pallas_call(
        _na_kernel,
        out_shape=jax.ShapeDtypeStruct((n_tok, width), BF16),
        grid=(n_tok // NA_GROUP,),
        in_specs=[
            pl.BlockSpec((NA_GROUP, width), lambda g: (g, COL_B_Q // B_HEADS)),
            kv_spec(COL_B_K, 0), kv_spec(COL_B_K, 1), kv_spec(COL_B_K, 2),
            kv_spec(COL_B_V, 0), kv_spec(COL_B_V, 1), kv_spec(COL_B_V, 2),
            pl.BlockSpec((None, B_HEADS, NA_GROUP, 3 * NA_GROUP), bias_map),
        ],
        out_specs=pl.BlockSpec((NA_GROUP, width), lambda g: (g, 0)),
        compiler_params=_cparams(("parallel",)),
        name="na_attn",
    )(proj, proj, proj, proj, proj, proj, proj, bias_tab)


def _na_bias_table(rel_table):
    rows_q = NA_GROUP // GRID_W
    rows_k = 3 * rows_q
    c = np.arange(GRID_W)
    col_start = np.clip(c - NA_COLS // 2, 0, GRID_W - NA_COLS)
    col_ok = (c[None, :] >= col_start[:, None]) & (c[None, :] < col_start[:, None] + NA_COLS)
    dc = np.clip(c[None, :] - c[:, None], -(NA_COLS - 1), NA_COLS - 1) + NA_COLS - 1
    onehot = (dc[None] == np.arange(2 * NA_COLS - 1)[:, None, None]).astype(np.float32)
    by_dr = jnp.einsum('hrd,dqk->hrqk', rel_table.astype(F32), jnp.asarray(onehot),
                       precision=lax.Precision.HIGHEST)
    by_dr = jnp.where(jnp.asarray(col_ok)[None, None], by_dr, NEG_BIG)
    by_dr = jnp.concatenate([by_dr, jnp.full_like(by_dr[:, :1], NEG_BIG)], axis=1)
    cfgs = ([(0, NA_ROWS - 1 - a) for a in range(rows_q)],
            [(a, NA_ROWS // 2 - 1) for a in range(rows_q)],
            [(rows_q, NA_ROWS // 2 - 1 - a) for a in range(rows_q)])
    idx = np.full((3, rows_q, rows_k), 2 * NA_ROWS - 1, np.int32)
    for ci, cfg in enumerate(cfgs):
        for a, (off, dr0) in enumerate(cfg):
            for j in range(NA_ROWS):
                idx[ci, a, off + j] = dr0 + j
    tab = jnp.take(by_dr, jnp.asarray(idx.reshape(-1)), axis=1)
    tab = tab.reshape(B_HEADS, 3, rows_q, rows_k, GRID_W, GRID_W).transpose(1, 0, 2, 4, 3, 5)
    return tab.reshape(3, B_HEADS, NA_GROUP, 3 * NA_GROUP)


def _split3(x):
    hi = x.astype(BF16)
    r1 = x - hi.astype(F32)
    mid = r1.astype(BF16)
    lo = (r1 - mid.astype(F32)).astype(BF16)
    return hi, mid, lo


def _hgrn_kernel(z_ref, v_ref, q_ref, lb_ref, scan_ref, o_ref, st_ref, k3_ref, b3_ref, *, segs, reverse):
    i = pl.program_id(0)
    n_tiles = pl.num_programs(0)
    tile = (n_tiles - 1 - i) if reverse else i
    local, n_loc = _seq_pos(tile, HG_TILE, segs)
    is_start = (local == n_loc - 1) if reverse else (local == 0)

    @pl.when(is_start)
    def _():
        st_ref[...] = jnp.zeros_like(st_ref)

    nblk = HG_TILE // HG_BLK
    half = HG_BLK // 2
    n_sc = nblk // 2
    log2e = math.log2(math.e)
    nt = (((1,), (1,)), ((), ()))

    def sel_index(t0):
        shape = (nblk, half, HG_TILE)
        t = lax.broadcasted_iota(jnp.int32, shape, 1) + t0
        j = lax.broadcasted_iota(jnp.int32, shape, 2) - lax.broadcasted_iota(jnp.int32, shape, 0) * HG_BLK
        ok = (j >= 0) & (j < HG_BLK) & ((j >= t) if reverse else (j <= t))
        return jnp.where(ok, j, -1)

    sel = (sel_index(0), sel_index(half))
    rb = lax.broadcasted_iota(jnp.int32, (HG_TILE, HG_TILE), 0) // HG_BLK
    cb = lax.broadcasted_iota(jnp.int32, (HG_TILE, HG_TILE), 1) // HG_BLK
    cross = ((rb % 2 == 0) & (cb == rb + 1)) if reverse else ((rb % 2 == 1) & (cb == rb - 1))
    blk_odd = (lax.broadcasted_iota(jnp.int32, (nblk, 1, HEAD_DIM), 0) % 2) == 1
    edge_row = 0 if reverse else HG_BLK - 1
    scan = scan_ref[...]

    for h in range(C_HEADS):
        cs = slice(h * HEAD_DIM, (h + 1) * HEAD_DIM)
        lb = lb_ref[:, cs]
        f = lb + (1.0 - lb) * jax.nn.sigmoid(z_ref[:, cs].astype(F32))
        k3 = (1.0 - f).reshape(nblk, HG_BLK, HEAD_DIM)
        q3 = (q_ref[:, cs].astype(F32) * (HEAD_DIM ** -0.5)).reshape(nblk, HG_BLK, HEAD_DIM)
        v = v_ref[:, cs]

        pieces = jnp.concatenate(_split3(jnp.log(f)), axis=1)
        bm = jnp.dot(scan, pieces, preferred_element_type=F32)
        b = bm[:, :HEAD_DIM] + bm[:, HEAD_DIM:2 * HEAD_DIM] + bm[:, 2 * HEAD_DIM:]
        b3 = (b * log2e).reshape(nblk, HG_BLK, HEAD_DIM)
        k3_ref[...] = k3
        b3_ref[...] = b3
        btot = b3_ref[:, edge_row:edge_row + 1, :]

        halves = ((q3[:, :half], b3[:, :half]), (q3[:, half:], b3[:, half:]))
        a_half = [jnp.zeros((nblk, half, HG_TILE), F32), jnp.zeros((nblk, half, HG_TILE), F32)]
        for j in range(HG_BLK):
            kj = k3_ref[:, j:j + 1, :]
            bj = b3_ref[:, j:j + 1, :]
            for hi in range(2):
                if (j < half * hi) if reverse else (j > half * hi + half - 1):
                    continue
                qx, bx = halves[hi]
                p = qx * kj * jnp.exp2(jnp.minimum(bx - bj, 0.0))
                r = jnp.sum(p, axis=-1, keepdims=True)
                a_half[hi] = jnp.where(sel[hi] == j, r, a_half[hi])
        a_in = jnp.concatenate(a_half, axis=1).reshape(HG_TILE, HG_TILE)

        qp3 = q3 * jnp.exp2(b3)
        kp3 = k3 * jnp.exp2(btot - b3)
        qp = qp3.reshape(HG_TILE, HEAD_DIM).astype(BF16)
        kp = kp3.reshape(HG_TILE, HEAD_DIM).astype(BF16)
        g = lax.dot_general(qp, kp, nt, preferred_element_type=F32)
        a_all = jnp.where(cross, g, a_in).astype(BF16)
        o_acc = jnp.dot(a_all, v, preferred_element_type=F32)

        dec = jnp.exp2(btot)
        one = jnp.ones_like(dec[:1])
        dprev = jnp.concatenate([one, dec[:-1]], axis=0)
        dnext = jnp.concatenate([dec[1:], one], axis=0)
        if reverse:
            qscale, kscale = jnp.where(blk_odd, 1.0, dnext), jnp.where(blk_odd, dprev, 1.0)
        else:
            qscale, kscale = jnp.where(blk_odd, dprev, 1.0), jnp.where(blk_odd, 1.0, dnext)
        qpp = (qp3 * qscale).reshape(HG_TILE, HEAD_DIM).astype(BF16)
        kpp = (kp3 * kscale).reshape(HG_TILE, HEAD_DIM).astype(BF16)
        st = st_ref[h]
        inter = [None] * n_sc
        for m in (range(n_sc - 1, -1, -1) if reverse else range(n_sc)):
            rows = slice(2 * m * HG_BLK, 2 * (m + 1) * HG_BLK)
            inter[m] = lax.dot_general(qpp[rows], st.astype(BF16), nt, preferred_element_type=F32)
            upd = lax.dot_general(v[rows], kpp[rows], (((0,), (0,)), ((), ())), preferred_element_type=F32)
            st = st * (dec[2 * m] * dec[2 * m + 1]) + upd
        st_ref[h] = st
        o_ref[:, cs] = o_acc + jnp.concatenate(inter, axis=0)


def _hgrn_scan_matrix(reverse):
    t = np.arange(HG_TILE)
    same = (t[:, None] // HG_BLK) == (t[None, :] // HG_BLK)
    incl = (t[None, :] >= t[:, None]) if reverse else (t[None, :] <= t[:, None])
    return jnp.asarray((same & incl).astype(np.float32), BF16)


def _hgrn_direction(proj, lower_bound, segs, reverse):
    n_tok = proj.shape[0]
    n_tiles = n_tok // HG_TILE
    width = C_HEADS * HEAD_DIM
    scan = _hgrn_scan_matrix(reverse)

    def tok_spec(col0):
        return pl.BlockSpec((HG_TILE, width),
                            lambda i: ((n_tiles - 1 - i) if reverse else i, col0 // C_HEADS))

    nblk = HG_TILE // HG_BLK
    return pl.pallas_call(
        functools.partial(_hgrn_kernel, segs=segs, reverse=reverse),
        out_shape=jax.ShapeDtypeStruct((n_tok, width), F32),
        grid=(n_tiles,),
        in_specs=[
            tok_spec(COL_C_FB if reverse else COL_C_FF), tok_spec(COL_C_I), tok_spec(COL_C_Q),
            pl.BlockSpec((1, width), lambda i: (0, 0)),
            pl.BlockSpec(scan.shape, lambda i: (0, 0)),
        ],
        out_specs=pl.BlockSpec((HG_TILE, width), lambda i: ((n_tiles - 1 - i) if reverse else i, 0)),
        scratch_shapes=[pltpu.VMEM((C_HEADS, HEAD_DIM, HEAD_DIM), F32),
                        pltpu.VMEM((nblk, HG_BLK, HEAD_DIM), F32),
                        pltpu.VMEM((nblk, HG_BLK, HEAD_DIM), F32)],
        compiler_params=_cparams(("arbitrary",)),
        name="hgrn_bwd" if reverse else "hgrn_fwd",
    )(proj, proj, proj, lower_bound.reshape(1, width), scan)


def _merge_kernel(ga_ref, gb_ref, gc_ref, cg_ref, ya_ref, yb_ref, of_ref, ob_ref, hn_ref,
                  wa_ref, wb_ref, wc_ref, wo_ref, h_ref, o_ref):
    o = of_ref[...] + ob_ref[...]
    heads = []
    for h in range(C_HEADS):
        oh = o[:, h * HEAD_DIM:(h + 1) * HEAD_DIM]
        ms = jnp.mean(oh * oh, axis=-1, keepdims=True)
        heads.append(oh * lax.rsqrt(ms + EPS))
    cg = cg_ref[...].astype(F32)
    yc = (jnp.concatenate(heads, axis=1) * hn_ref[...] * (cg * jax.nn.sigmoid(cg))).astype(BF16)
    m = jax.nn.sigmoid(ga_ref[...].astype(F32)) * jnp.dot(ya_ref[...], wa_ref[...], preferred_element_type=F32)
    m += jax.nn.sigmoid(gb_ref[...].astype(F32)) * jnp.dot(yb_ref[...], wb_ref[...], preferred_element_type=F32)
    m += jax.nn.sigmoid(gc_ref[...].astype(F32)) * jnp.dot(yc, wc_ref[...], preferred_element_type=F32)
    o_ref[...] = h_ref[...] + jnp.dot(m.astype(BF16), wo_ref[...], preferred_element_type=F32)


def _merge(proj, gates, ya, yb, o_f, o_b, hgrn_gain, wa, wb, wc, wo, h, tm):
    n_tok, d = h.shape
    cw = C_HEADS * HEAD_DIM

    def resident(shape):
        return pl.BlockSpec(shape, lambda i: (0, 0), pipeline_mode=pl.Buffered(1))

    def gate_spec(which):
        return pl.BlockSpec((tm, d), lambda i: (i, which))

    return pl.pallas_call(
        _merge_kernel,
        out_shape=jax.ShapeDtypeStruct((n_tok, d), F32),
        grid=(n_tok // tm,),
        in_specs=[
            gate_spec(0), gate_spec(1), gate_spec(2),
            pl.BlockSpec((tm, cw), lambda i: (i, COL_C_G // C_HEADS)),
            pl.BlockSpec((tm, ya.shape[1]), lambda i: (i, 0)),
            pl.BlockSpec((tm, cw), lambda i: (i, 0)),
            pl.BlockSpec((tm, cw), lambda i: (i, 0)),
            pl.BlockSpec((tm, cw), lambda i: (i, 0)),
            pl.BlockSpec((1, cw), lambda i: (0, 0)),
            resident(wa.shape), resident(wb.shape), resident(wc.shape), resident(wo.shape),
            pl.BlockSpec((tm, d), lambda i: (i, 0)),
        ],
        out_specs=pl.BlockSpec((tm, d), lambda i: (i, 0)),
        compiler_params=_cparams(("parallel",)),
        name="merge_out",
    )(gates, gates, gates, proj, ya, yb, o_f, o_b, hgrn_gain.reshape(1, cw), wa, wb, wc, wo, h)


def _tile(n, want):
    t = want
    while n % t:
        t //= 2
    return t


def _trunk(x, segs, p):
    depth = p["w_in"].shape[0]
    n_tok = x.shape[0]
    tm_ffn, tm_proj, tm_merge = _tile(n_tok, 512), _tile(n_tok, 1024), _tile(n_tok, 256)
    d_ff = p["ffn1_w_gate"].shape[-1]
    tf = 512 if d_ff % 512 == 0 else 256
    mix_w = MIX_COLS * HEAD_DIM

    lb_p = jax.nn.softmax(p["hgrn_lb_logits"].astype(F32), axis=1)
    lower_bounds = jnp.cumsum(lb_p, axis=1) - lb_p[:, :1]
    win_tab = _window_bias_table(p["t5_bias"])
    bf = lambda name: p[name].astype(BF16)
    w = {name: bf(name) for name in ("ffn1_w_gate", "ffn1_w_up", "ffn1_w_down", "w_in", "w_branch_a",
                                      "w_branch_b", "w_branch_c", "w_out", "ffn2_w_gate", "ffn2_w_up",
                                      "ffn2_w_down")}
    out = None
    for l in range(depth):
        h, u = _ffn(x, p["ffn1_norm"][l], w["ffn1_w_gate"][l], w["ffn1_w_up"][l], w["ffn1_w_down"][l],
                    p["mix_norm"][l], BF16, tm_ffn, tf)
        proj = _matmul(u, w["w_in"][l, :, :mix_w], BF16, tm_proj, 512)
        gates = _matmul(u, w["w_in"][l, :, mix_w:], BF16, tm_proj, 512)
        ya = _window_attention(proj, p["attn_sink"][l].astype(F32), win_tab, segs)
        yb = _neighbourhood_attention(proj, _na_bias_table(p["na_bias"][l]), segs)
        o_f = _hgrn_direction(proj, lower_bounds[0, l], segs, reverse=False)
        o_b = _hgrn_direction(proj, lower_bounds[1, l], segs, reverse=True)
        h = _merge(proj, gates, ya, yb, o_f, o_b, p["hgrn_norm"][l], w["w_branch_a"][l], w["w_branch_b"][l],
                   w["w_branch_c"][l], w["w_out"][l], h, tm_merge)
        next_gain = p["final_norm"] if l == depth - 1 else p["ffn1_norm"][l]
        x, out = _ffn(h, p["ffn2_norm"][l], w["ffn2_w_gate"][l], w["ffn2_w_up"][l], w["ffn2_w_down"][l],
                      next_gain, F32, tm_ffn, tf)
    return out


def kernel(x_prompt, x_sample, ffn1_norm, ffn1_w_gate, ffn1_w_up, ffn1_w_down, mix_norm, w_in, attn_sink,
           t5_bias, na_bias, hgrn_lb_logits, hgrn_norm, w_branch_a, w_branch_b, w_branch_c, w_out,
           ffn2_norm, ffn2_w_gate, ffn2_w_up, ffn2_w_down, final_norm):
    params = dict(ffn1_norm=ffn1_norm, ffn1_w_gate=ffn1_w_gate, ffn1_w_up=ffn1_w_up, ffn1_w_down=ffn1_w_down,
                  mix_norm=mix_norm, w_in=w_in, attn_sink=attn_sink, t5_bias=t5_bias, na_bias=na_bias,
                  hgrn_lb_logits=hgrn_lb_logits, hgrn_norm=hgrn_norm, w_branch_a=w_branch_a,
                  w_branch_b=w_branch_b, w_branch_c=w_branch_c, w_out=w_out, ffn2_norm=ffn2_norm,
                  ffn2_w_gate=ffn2_w_gate, ffn2_w_up=ffn2_w_up, ffn2_w_down=ffn2_w_down, final_norm=final_norm)
    d = x_prompt.shape[-1]
    segs = (x_prompt.shape[:2], x_sample.shape[:2])
    for _, t in segs:
        assert t % NA_GROUP == 0 and t // NA_GROUP >= 3 and t // GRID_W >= NA_ROWS
    x = jnp.concatenate([x_prompt.reshape(-1, d), x_sample.reshape(-1, d)], axis=0)
    y = _trunk(x, segs, params)
    n_prompt = x_prompt.shape[0] * x_prompt.shape[1]
    return y[:n_prompt].reshape(x_prompt.shape), y[n_prompt:].reshape(x_sample.shape)
```

```python
import functools
import math

import numpy as np
import jax
import jax.numpy as jnp
from jax import lax
from jax.experimental import pallas as pl
from jax.experimental.pallas import tpu as pltpu

F32 = jnp.float32
BF16 = jnp.bfloat16

HEAD_DIM = 128
EPS = 1e-6
A_HEADS = 8
A_KV_HEADS = 2
WINDOW = 128
WIN_BLOCK = 128
T5_BUCKETS = 32
T5_MAX_DIST = 128
B_HEADS = 4
GRID_W = 64
NA_ROWS = 8
NA_COLS = 16
C_HEADS = 4
GATE_COLS = 48
MIX_COLS = 44
COL_A_Q, COL_A_K, COL_A_V = (GATE_COLS + c for c in (0, 8, 10))
COL_B_Q, COL_B_K, COL_B_V = (GATE_COLS + c for c in (12, 16, 20))
COL_C_FF, COL_C_FB, COL_C_I, COL_C_Q, COL_C_G = (GATE_COLS + c for c in (24, 28, 32, 36, 40))

NEG_BIG = -1e30
V7X_VMEM_LIMIT = 60 * 1024 * 1024
HG_TILE = 128
HG_BLK = 16
NA_GROUP = 4 * GRID_W


def _cparams(sem):
    return pltpu.CompilerParams(dimension_semantics=sem, vmem_limit_bytes=V7X_VMEM_LIMIT)


def _seq_pos(gb, blk, segs):
    (n_seq0, t0), (_, t1) = segs
    n0, n1 = t0 // blk, t1 // blk
    tot0 = n_seq0 * n0
    in0 = gb < tot0
    local = jnp.where(in0, gb % n0, (gb - tot0) % n1)
    return local, jnp.where(in0, n0, n1)


def _ffn_kernel(*refs, norm_in, emit_o):
    refs = list(refs)
    x_ref = refs.pop(0)
    n_ref = refs.pop(0) if norm_in else None
    g1_ref = None if norm_in else refs.pop(0)
    wg_ref, wu_ref, wd_ref, g2_ref = refs[:4]
    del refs[:4]
    o_ref = refs.pop(0) if emit_o else None
    o2_ref = refs.pop(0)
    if not norm_in:
        n_ref = refs.pop(0)
    acc_ref = o_ref if emit_o else o2_ref
    j = pl.program_id(1)

    @pl.when(j == 0)
    def _():
        if not norm_in:
            x = x_ref[...]
            ms = jnp.mean(x * x, axis=-1, keepdims=True)
            n_ref[...] = (x * lax.rsqrt(ms + EPS) * g1_ref[...]).astype(BF16)
        acc_ref[...] = jnp.zeros_like(acc_ref)

    n = n_ref[...]
    g = jnp.dot(n, wg_ref[...], preferred_element_type=F32)
    u = jnp.dot(n, wu_ref[...], preferred_element_type=F32)
    a = (g * jax.nn.sigmoid(g) * u).astype(BF16)
    acc_ref[...] += jnp.dot(a, wd_ref[...], preferred_element_type=F32)

    @pl.when(j == pl.num_programs(1) - 1)
    def _():
        y = x_ref[...] + 0.5 * acc_ref[...]
        if emit_o:
            o_ref[...] = y
        ms = jnp.mean(y * y, axis=-1, keepdims=True)
        o2_ref[...] = (y * lax.rsqrt(ms + EPS) * g2_ref[...]).astype(o2_ref.dtype)


def _ffn(x, n_or_gain1, wg, wu, wd, gain2, tm, tf, *, norm_in, emit_o):
    n_tok, d = x.shape
    d_ff = wg.shape[1]
    tok_spec = pl.BlockSpec((tm, d), lambda i, j: (i, 0))
    vec_spec = pl.BlockSpec((1, d), lambda i, j: (0, 0))
    out_shape = [jax.ShapeDtypeStruct((n_tok, d), BF16 if emit_o else F32)]
    if emit_o:
        out_shape.insert(0, jax.ShapeDtypeStruct((n_tok, d), F32))
    return pl.pallas_call(
        functools.partial(_ffn_kernel, norm_in=norm_in, emit_o=emit_o),
        out_shape=tuple(out_shape),
        grid=(n_tok // tm, d_ff // tf),
        in_specs=[
            tok_spec,
            tok_spec if norm_in else vec_spec,
            pl.BlockSpec((d, tf), lambda i, j: (0, j)),
            pl.BlockSpec((d, tf), lambda i, j: (0, j)),
            pl.BlockSpec((tf, d), lambda i, j: (j, 0)),
            vec_spec,
        ],
        out_specs=tuple(tok_spec for _ in out_shape),
        scratch_shapes=[] if norm_in else [pltpu.VMEM((tm, d), BF16)],
        compiler_params=_cparams(("parallel", "arbitrary")),
        name="ffn",
    )(x, n_or_gain1 if norm_in else n_or_gain1.reshape(1, d), wg, wu, wd, gain2.reshape(1, d))


def _matmul_kernel(a_ref, w_ref, o_ref):
    o_ref[...] = jnp.dot(a_ref[...], w_ref[...], preferred_element_type=F32).astype(o_ref.dtype)


def _matmul(a, w, out_dtype, tm, tn):
    n_tok, k = a.shape
    n_out = w.shape[1]
    return pl.pallas_call(
        _matmul_kernel,
        out_shape=jax.ShapeDtypeStruct((n_tok, n_out), out_dtype),
        grid=(n_tok // tm, n_out // tn),
        in_specs=[pl.BlockSpec((tm, k), lambda i, j: (i, 0)), pl.BlockSpec((k, tn), lambda i, j: (0, j))],
        out_specs=pl.BlockSpec((tm, tn), lambda i, j: (i, j)),
        compiler_params=_cparams(("parallel", "arbitrary")),
        name="proj_in",
    )(a, w)


def _wattn_kernel(sink_ref, q_ref, kp_ref, kc_ref, kn_ref, vp_ref, vc_ref, vn_ref, bias_ref, o_ref):
    group = A_HEADS // A_KV_HEADS
    scale = HEAD_DIM ** -0.5
    for kv in range(A_KV_HEADS):
        cs = slice(kv * HEAD_DIM, (kv + 1) * HEAD_DIM)
        kcat = jnp.concatenate([kp_ref[:, cs], kc_ref[:, cs], kn_ref[:, cs]], axis=0)
        vcat = jnp.concatenate([vp_ref[:, cs], vc_ref[:, cs], vn_ref[:, cs]], axis=0)
        heads = [kv * group + i for i in range(group)]
        qs = jnp.concatenate([q_ref[:, h * HEAD_DIM:(h + 1) * HEAD_DIM] for h in heads], axis=0)
        s = lax.dot_general(qs, kcat, (((1,), (1,)), ((), ())), preferred_element_type=F32)
        s = s * scale + bias_ref[kv * group * WIN_BLOCK:(kv + 1) * group * WIN_BLOCK, :]
        for i, h in enumerate(heads):
            sh = s[i * WIN_BLOCK:(i + 1) * WIN_BLOCK, :]
            sink = sink_ref[h]
            m = jnp.maximum(jnp.max(sh, axis=-1, keepdims=True), sink)
            e = jnp.exp(sh - m)
            den = jnp.sum(e, axis=-1, keepdims=True) + jnp.exp(sink - m)
            oh = jnp.dot(e.astype(BF16), vcat, preferred_element_type=F32)
            o_ref[:, h * HEAD_DIM:(h + 1) * HEAD_DIM] = (oh / den).astype(o_ref.dtype)


def _window_attention(proj, sink, bias_tab, segs):
    n_tok = proj.shape[0]
    nb = n_tok // WIN_BLOCK

    def first_last(gb):
        local, n_loc = _seq_pos(gb, WIN_BLOCK, segs)
        return local == 0, local == n_loc - 1

    def bias_map(gb):
        first, last = first_last(gb)
        return (jnp.where(first, 0, jnp.where(last, 2, 1)), 0, 0)

    def kv_spec(col0, delta):
        return pl.BlockSpec((WIN_BLOCK, A_KV_HEADS * HEAD_DIM),
                            lambda gb: (jnp.clip(gb + delta, 0, nb - 1), col0 // A_KV_HEADS))

    return pl.pallas_call(
        _wattn_kernel,
        out_shape=jax.ShapeDtypeStruct((n_tok, A_HEADS * HEAD_DIM), BF16),
        grid=(nb,),
        in_specs=[
            pl.BlockSpec(memory_space=pltpu.SMEM),
            pl.BlockSpec((WIN_BLOCK, A_HEADS * HEAD_DIM), lambda gb: (gb, COL_A_Q // A_HEADS)),
            kv_spec(COL_A_K, -1), kv_spec(COL_A_K, 0), kv_spec(COL_A_K, 1),
            kv_spec(COL_A_V, -1), kv_spec(COL_A_V, 0), kv_spec(COL_A_V, 1),
            pl.BlockSpec((None, A_HEADS * WIN_BLOCK, 3 * WIN_BLOCK), bias_map),
        ],
        out_specs=pl.BlockSpec((WIN_BLOCK, A_HEADS * HEAD_DIM), lambda gb: (gb, 0)),
        compiler_params=_cparams(("parallel",)),
        name="window_attn",
    )(sink, proj, proj, proj, proj, proj, proj, proj, bias_tab)


def _t5_bucket(rel):
    nb = T5_BUCKETS // 2
    max_exact = nb // 2
    base = jnp.where(rel > 0, nb, 0)
    n = jnp.abs(rel)
    large = max_exact + (jnp.log(jnp.maximum(n, 1).astype(F32) / max_exact)
                         / math.log(T5_MAX_DIST / max_exact) * (nb - max_exact)).astype(jnp.int32)
    large = jnp.minimum(large, nb - 1)
    return base + jnp.where(n < max_exact, n, large)


def _window_bias_table(t5_bias):
    qi = jnp.arange(WIN_BLOCK)[:, None]
    si = jnp.arange(3 * WIN_BLOCK)[None, :]
    rel = si - WIN_BLOCK - qi
    onehot = (_t5_bucket(rel)[None] == jnp.arange(T5_BUCKETS)[:, None, None]).astype(F32)
    bias = jnp.einsum('bh,bqs->hqs', t5_bias.astype(F32), onehot, precision=lax.Precision.HIGHEST)
    band = jnp.abs(rel) <= WINDOW
    variants = []
    for lo_ok, hi_ok in ((False, True), (True, True), (True, False)):
        ok = band & ((si >= WIN_BLOCK) | lo_ok) & ((si < 2 * WIN_BLOCK) | hi_ok)
        variants.append(jnp.where(ok[None], bias, NEG_BIG).reshape(A_HEADS * WIN_BLOCK, 3 * WIN_BLOCK))
    return jnp.stack(variants)


def _na_window(g, segs):
    lg, ng = _seq_pos(g, NA_GROUP, segs)
    return lg, ng, jnp.clip(lg - 1, 0, ng - 3)


def _na_kernel(q_ref, k0_ref, k1_ref, k2_ref, v0_ref, v1_ref, v2_ref, bias_ref, o_ref):
    scale = HEAD_DIM ** -0.5
    for h in range(B_HEADS):
        cs = slice(h * HEAD_DIM, (h + 1) * HEAD_DIM)
        kcat = jnp.concatenate([k0_ref[:, cs], k1_ref[:, cs], k2_ref[:, cs]], axis=0)
        vcat = jnp.concatenate([v0_ref[:, cs], v1_ref[:, cs], v2_ref[:, cs]], axis=0)
        s = lax.dot_general(q_ref[:, cs], kcat, (((1,), (1,)), ((), ())), preferred_element_type=F32)
        s = s * scale + bias_ref[h]
        m = jnp.max(s, axis=-1, keepdims=True)
        e = jnp.exp(s - m)
        den = jnp.sum(e, axis=-1, keepdims=True)
        oh = jnp.dot(e.astype(BF16), vcat, preferred_element_type=F32)
        o_ref[:, cs] = (oh / den).astype(o_ref.dtype)


def _neighbourhood_attention(proj, bias_tab, segs):
    n_tok = proj.shape[0]
    width = B_HEADS * HEAD_DIM

    def kv_spec(col0, j):
        def index_map(g):
            lg, _, lo = _na_window(g, segs)
            return (g - lg + lo + j, col0 // B_HEADS)
        return pl.BlockSpec((NA_GROUP, width), index_map)

    def bias_map(g):
        lg, ng, _ = _na_window(g, segs)
        return (jnp.where(lg == 0, 0, jnp.where(lg == ng - 1, 2, 1)), 0, 0, 0)

    return pl.pallas_call(
        _na_kernel,
        out_shape=jax.ShapeDtypeStruct((n_tok, width), BF16),
        grid=(n_tok // NA_GROUP,),
        in_specs=[
            pl.BlockSpec((NA_GROUP, width), lambda g: (g, COL_B_Q // B_HEADS)),
            kv_spec(COL_B_K, 0), kv_spec(COL_B_K, 1), kv_spec(COL_B_K, 2),
            kv_spec(COL_B_V, 0), kv_spec(COL_B_V, 1), kv_spec(COL_B_V, 2),
            pl.BlockSpec((None, B_HEADS, NA_GROUP, 3 * NA_GROUP), bias_map),
        ],
        out_specs=pl.BlockSpec((NA_GROUP, width), lambda g: (g, 0)),
        compiler_params=_cparams(("parallel",)),
        name="na_attn",
    )(proj, proj, proj, proj, proj, proj, proj, bias_tab)


def _na_bias_table(rel_table):
    rows_q = NA_GROUP // GRID_W
    rows_k = 3 * rows_q
    c = np.arange(GRID_W)
    col_start = np.clip(c - NA_COLS // 2, 0, GRID_W - NA_COLS)
    col_ok = (c[None, :] >= col_start[:, None]) & (c[None, :] < col_start[:, None] + NA_COLS)
    dc = np.clip(c[None, :] - c[:, None], -(NA_COLS - 1), NA_COLS - 1) + NA_COLS - 1
    onehot = (dc[None] == np.arange(2 * NA_COLS - 1)[:, None, None]).astype(np.float32)
    by_dr = jnp.einsum('hrd,dqk->hrqk', rel_table.astype(F32), jnp.asarray(onehot),
                       precision=lax.Precision.HIGHEST)
    by_dr = jnp.where(jnp.asarray(col_ok)[None, None], by_dr, NEG_BIG)
    by_dr = jnp.concatenate([by_dr, jnp.full_like(by_dr[:, :1], NEG_BIG)], axis=1)
    cfgs = ([(0, NA_ROWS - 1 - a) for a in range(rows_q)],
            [(a, NA_ROWS // 2 - 1) for a in range(rows_q)],
            [(rows_q, NA_ROWS // 2 - 1 - a) for a in range(rows_q)])
    idx = np.full((3, rows_q, rows_k), 2 * NA_ROWS - 1, np.int32)
    for ci, cfg in enumerate(cfgs):
        for a, (off, dr0) in enumerate(cfg):
            for j in range(NA_ROWS):
                idx[ci, a, off + j] = dr0 + j
    tab = jnp.take(by_dr, jnp.asarray(idx.reshape(-1)), axis=1)
    tab = tab.reshape(B_HEADS, 3, rows_q, rows_k, GRID_W, GRID_W).transpose(1, 0, 2, 4, 3, 5)
    return tab.reshape(3, B_HEADS, NA_GROUP, 3 * NA_GROUP)


def _split3(x):
    hi = x.astype(BF16)
    r1 = x - hi.astype(F32)
    mid = r1.astype(BF16)
    lo = (r1 - mid.astype(F32)).astype(BF16)
    return hi, mid, lo


def _hgrn_kernel(z_ref, v_ref, q_ref, lb_ref, scan_ref, o_ref, st_ref, c3_ref, b3_ref, k3_ref, q3_ref, a_ref,
                 *, segs, reverse):
    i = pl.program_id(0)
    n_steps = pl.num_programs(0)
    n_sub = z_ref.shape[0] // HG_TILE
    step = (n_steps - 1 - i) if reverse else i
    local, n_loc = _seq_pos(step, n_sub * HG_TILE, segs)
    is_start = (local == n_loc - 1) if reverse else (local == 0)

    @pl.when(is_start)
    def _():
        st_ref[...] = jnp.zeros_like(st_ref)

    nblk = HG_TILE // HG_BLK
    half = HG_BLK // 2
    n_sc = nblk // 2
    log2e = math.log2(math.e)
    nt = (((1,), (1,)), ((), ()))

    def sel_index(t0):
        shape = (nblk, half, HG_TILE)
        t = lax.broadcasted_iota(jnp.int32, shape, 1) + t0
        j = lax.broadcasted_iota(jnp.int32, shape, 2) - lax.broadcasted_iota(jnp.int32, shape, 0) * HG_BLK
        ok = (j >= 0) & (j < HG_BLK) & ((j >= t) if reverse else (j <= t))
        return jnp.where(ok, j, -1)

    sel = (sel_index(0), sel_index(half))
    rb = lax.broadcasted_iota(jnp.int32, (HG_TILE, HG_TILE), 0) // HG_BLK
    cb = lax.broadcasted_iota(jnp.int32, (HG_TILE, HG_TILE), 1) // HG_BLK
    cross = ((rb % 2 == 0) & (cb == rb + 1)) if reverse else ((rb % 2 == 1) & (cb == rb - 1))
    blk_odd = (lax.broadcasted_iota(jnp.int32, (nblk, 1, HEAD_DIM), 0) % 2) == 1
    edge_row = 0 if reverse else HG_BLK - 1
    scan = scan_ref[...]

    def one_tile(t, carry):
        tile = (n_sub - 1 - t) if reverse else t
        tok = pl.ds(pl.multiple_of(tile * HG_TILE, HG_TILE), HG_TILE)
        heads = [slice(h * HEAD_DIM, (h + 1) * HEAD_DIM) for h in range(C_HEADS)]

        for h, cs in enumerate(heads):
            lb = lb_ref[:, cs]
            f = lb + (1.0 - lb) * jax.nn.sigmoid(z_ref[tok, cs].astype(F32))
            k3 = (1.0 - f).reshape(nblk, HG_BLK, HEAD_DIM)
            pieces = jnp.concatenate(_split3(jnp.log(f)), axis=1)
            bm = jnp.dot(scan, pieces, preferred_element_type=F32)
            b = bm[:, :HEAD_DIM] + bm[:, HEAD_DIM:2 * HEAD_DIM] + bm[:, 2 * HEAD_DIM:]
            b3 = (b * log2e).reshape(nblk, HG_BLK, HEAD_DIM)
            k3_ref[h] = k3
            b3_ref[h] = b3
            c3_ref[h] = b3 - jnp.log2(jnp.maximum(k3, 0.0))
            q3_ref[h] = (q_ref[tok, cs].astype(F32) * (HEAD_DIM ** -0.5)).reshape(nblk, HG_BLK, HEAD_DIM)

        for h in range(C_HEADS):
            halves = ((q3_ref[h, :, :half, :], b3_ref[h, :, :half, :]), (q3_ref[h, :, half:, :], b3_ref[h, :, half:, :]))
            a_half = [jnp.zeros((nblk, half, HG_TILE), F32), jnp.zeros((nblk, half, HG_TILE), F32)]
            for j in range(HG_BLK):
                cj = c3_ref[h, :, j:j + 1, :]
                for hi in range(2):
                    if (j < half * hi) if reverse else (j > half * hi + half - 1):
                        continue
                    qx, bx = halves[hi]
                    p = qx * jnp.exp2(bx - cj)
                    r = jnp.sum(p, axis=-1, keepdims=True)
                    a_half[hi] = jnp.where(sel[hi] == j, r, a_half[hi])
            a_ref[h] = jnp.concatenate(a_half, axis=1).reshape(HG_TILE, HG_TILE)

        for h, cs in enumerate(heads):
            q3, k3, b3 = q3_ref[h], k3_ref[h], b3_ref[h]
            btot = b3_ref[h, :, edge_row:edge_row + 1, :]
            v = v_ref[tok, cs]
            qp3 = q3 * jnp.exp2(b3)
            kp3 = k3 * jnp.exp2(btot - b3)
            qp = qp3.reshape(HG_TILE, HEAD_DIM).astype(BF16)
            kp = kp3.reshape(HG_TILE, HEAD_DIM).astype(BF16)
            g = lax.dot_general(qp, kp, nt, preferred_element_type=F32)
            a_all = jnp.where(cross, g, a_ref[h]).astype(BF16)
            o_acc = jnp.dot(a_all, v, preferred_element_type=F32)

            dec = jnp.exp2(btot)
            one = jnp.ones_like(dec[:1])
            dprev = jnp.concatenate([one, dec[:-1]], axis=0)
            dnext = jnp.concatenate([dec[1:], one], axis=0)
            if reverse:
                qscale, kscale = jnp.where(blk_odd, 1.0, dnext), jnp.where(blk_odd, dprev, 1.0)
            else:
                qscale, kscale = jnp.where(blk_odd, dprev, 1.0), jnp.where(blk_odd, 1.0, dnext)
            qpp = (qp3 * qscale).reshape(HG_TILE, HEAD_DIM).astype(BF16)
            kpp = (kp3 * kscale).reshape(HG_TILE, HEAD_DIM).astype(BF16)
            st = st_ref[h]
            inter = [None] * n_sc
            for m in (range(n_sc - 1, -1, -1) if reverse else range(n_sc)):
                rows = slice(2 * m * HG_BLK, 2 * (m + 1) * HG_BLK)
                inter[m] = lax.dot_general(qpp[rows], st.astype(BF16), nt, preferred_element_type=F32)
                upd = lax.dot_general(v[rows], kpp[rows], (((0,), (0,)), ((), ())), preferred_element_type=F32)
                st = st * (dec[2 * m] * dec[2 * m + 1]) + upd
            st_ref[h] = st
            o_ref[tok, cs] = o_acc + jnp.concatenate(inter, axis=0)
        return carry

    lax.fori_loop(0, n_sub, one_tile, 0)


def _hgrn_scan_matrix(reverse):
    t = np.arange(HG_TILE)
    same = (t[:, None] // HG_BLK) == (t[None, :] // HG_BLK)
    incl = (t[None, :] >= t[:, None]) if reverse else (t[None, :] <= t[:, None])
    return jnp.asarray((same & incl).astype(np.float32), BF16)


def _hgrn_direction(proj, lower_bound, segs, reverse):
    n_tok = proj.shape[0]
    step_tok = next(s for s in (4 * HG_TILE, 2 * HG_TILE, HG_TILE) if all(t % s == 0 for _, t in segs))
    n_steps = n_tok // step_tok
    width = C_HEADS * HEAD_DIM
    scan = _hgrn_scan_matrix(reverse)

    def tok_spec(col0):
        return pl.BlockSpec((step_tok, width),
                            lambda i: ((n_steps - 1 - i) if reverse else i, col0 // C_HEADS))

    nblk = HG_TILE // HG_BLK
    return pl.pallas_call(
        functools.partial(_hgrn_kernel, segs=segs, reverse=reverse),
        out_shape=jax.ShapeDtypeStruct((n_tok, width), F32),
        grid=(n_steps,),
        in_specs=[
            tok_spec(COL_C_FB if reverse else COL_C_FF), tok_spec(COL_C_I), tok_spec(COL_C_Q),
            pl.BlockSpec((1, width), lambda i: (0, 0)),
            pl.BlockSpec(scan.shape, lambda i: (0, 0)),
        ],
        out_specs=pl.BlockSpec((step_tok, width), lambda i: ((n_steps - 1 - i) if reverse else i, 0)),
        scratch_shapes=[pltpu.VMEM((C_HEADS, HEAD_DIM, HEAD_DIM), F32)]
        + [pltpu.VMEM((C_HEADS, nblk, HG_BLK, HEAD_DIM), F32)] * 4
        + [pltpu.VMEM((C_HEADS, HG_TILE, HG_TILE), F32)],
        compiler_params=_cparams(("arbitrary",)),
        name="hgrn_bwd" if reverse else "hgrn_fwd",
    )(proj, proj, proj, lower_bound.reshape(1, width), scan)


def _merge_kernel(ga_ref, gb_ref, gc_ref, cg_ref, ya_ref, yb_ref, of_ref, ob_ref, hn_ref,
                  wa_ref, wb_ref, wc_ref, wo_ref, h_ref, g2_ref, o_ref, o2_ref):
    o = of_ref[...] + ob_ref[...]
    heads = []
    for h in range(C_HEADS):
        oh = o[:, h * HEAD_DIM:(h + 1) * HEAD_DIM]
        ms = jnp.mean(oh * oh, axis=-1, keepdims=True)
        heads.append(oh * lax.rsqrt(ms + EPS))
    cg = cg_ref[...].astype(F32)
    yc = (jnp.concatenate(heads, axis=1) * hn_ref[...] * (cg * jax.nn.sigmoid(cg))).astype(BF16)
    m = jax.nn.sigmoid(ga_ref[...].astype(F32)) * jnp.dot(ya_ref[...], wa_ref[...], preferred_element_type=F32)
    m += jax.nn.sigmoid(gb_ref[...].astype(F32)) * jnp.dot(yb_ref[...], wb_ref[...], preferred_element_type=F32)
    m += jax.nn.sigmoid(gc_ref[...].astype(F32)) * jnp.dot(yc, wc_ref[...], preferred_element_type=F32)
    y = h_ref[...] + jnp.dot(m.astype(BF16), wo_ref[...], preferred_element_type=F32)
    o_ref[...] = y
    ms = jnp.mean(y * y, axis=-1, keepdims=True)
    o2_ref[...] = (y * lax.rsqrt(ms + EPS) * g2_ref[...]).astype(o2_ref.dtype)


def _merge(proj, ya, yb, o_f, o_b, hgrn_gain, wa, wb, wc, wo, h, gain2, tm):
    n_tok, d = h.shape
    cw = C_HEADS * HEAD_DIM

    def resident(shape):
        return pl.BlockSpec(shape, lambda i: (0, 0), pipeline_mode=pl.Buffered(1))

    def gate_spec(which):
        return pl.BlockSpec((tm, d), lambda i: (i, which))

    return pl.pallas_call(
        _merge_kernel,
        out_shape=(jax.ShapeDtypeStruct((n_tok, d), F32), jax.ShapeDtypeStruct((n_tok, d), BF16)),
        grid=(n_tok // tm,),
        in_specs=[
            gate_spec(0), gate_spec(1), gate_spec(2),
            pl.BlockSpec((tm, cw), lambda i: (i, COL_C_G // C_HEADS)),
            pl.BlockSpec((tm, ya.shape[1]), lambda i: (i, 0)),
            pl.BlockSpec((tm, cw), lambda i: (i, 0)),
            pl.BlockSpec((tm, cw), lambda i: (i, 0)),
            pl.BlockSpec((tm, cw), lambda i: (i, 0)),
            pl.BlockSpec((1, cw), lambda i: (0, 0)),
            resident(wa.shape), resident(wb.shape), resident(wc.shape), resident(wo.shape),
            pl.BlockSpec((tm, d), lambda i: (i, 0)),
            pl.BlockSpec((1, d), lambda i: (0, 0)),
        ],
        out_specs=(pl.BlockSpec((tm, d), lambda i: (i, 0)), pl.BlockSpec((tm, d), lambda i: (i, 0))),
        compiler_params=_cparams(("parallel",)),
        name="merge_out",
    )(proj, proj, proj, proj, ya, yb, o_f, o_b, hgrn_gain.reshape(1, cw), wa, wb, wc, wo, h,
      gain2.reshape(1, d))


def _tile(n, candidates):
    return next(t for t in candidates if n % t == 0)


def _trunk(x, segs, p):
    depth = p["w_in"].shape[0]
    n_tok = x.shape[0]
    tm_ffn = _tile(n_tok, (768, 512, 256, 128))
    tm_proj = _tile(n_tok, (2048, 1024, 512, 256, 128))
    tm_merge = _tile(n_tok, (256, 128))
    d_ff = p["ffn1_w_gate"].shape[-1]
    tf = 512 if d_ff % 512 == 0 else 256
    mix_w = MIX_COLS * HEAD_DIM

    lb_p = jax.nn.softmax(p["hgrn_lb_logits"].astype(F32), axis=1)
    lower_bounds = jnp.cumsum(lb_p, axis=1) - lb_p[:, :1]
    win_tab = _window_bias_table(p["t5_bias"])
    bf = lambda name: p[name].astype(BF16)
    w = {name: bf(name) for name in ("ffn1_w_gate", "ffn1_w_up", "ffn1_w_down", "w_in", "w_branch_a",
                                      "w_branch_b", "w_branch_c", "w_out", "ffn2_w_gate", "ffn2_w_up",
                                      "ffn2_w_down")}
    w_in = jnp.concatenate([w["w_in"][:, :, mix_w:], w["w_in"][:, :, :mix_w]], axis=-1)
    n = out = None
    for l in range(depth):
        last = l == depth - 1
        h, u = _ffn(x, p["ffn1_norm"][l] if l == 0 else n, w["ffn1_w_gate"][l], w["ffn1_w_up"][l],
                    w["ffn1_w_down"][l], p["mix_norm"][l], tm_ffn, tf, norm_in=l > 0, emit_o=True)
        proj = _matmul(u, w_in[l], BF16, tm_proj, 512)
        ya = _window_attention(proj, p["attn_sink"][l].astype(F32), win_tab, segs)
        yb = _neighbourhood_attention(proj, _na_bias_table(p["na_bias"][l]), segs)
        o_f = _hgrn_direction(proj, lower_bounds[0, l], segs, reverse=False)
        o_b = _hgrn_direction(proj, lower_bounds[1, l], segs, reverse=True)
        h, n = _merge(proj, ya, yb, o_f, o_b, p["hgrn_norm"][l], w["w_branch_a"][l], w["w_branch_b"][l],
                      w["w_branch_c"][l], w["w_out"][l], h, p["ffn2_norm"][l], tm_merge)
        ffn2 = functools.partial(_ffn, h, n, w["ffn2_w_gate"][l], w["ffn2_w_up"][l], w["ffn2_w_down"][l])
        if last:
            (out,) = ffn2(p["final_norm"], tm_ffn, tf, norm_in=True, emit_o=False)
        else:
            x, n = ffn2(p["ffn1_norm"][l + 1], tm_ffn, tf, norm_in=True, emit_o=True)
    return out


def kernel(x_prompt, x_sample, ffn1_norm, ffn1_w_gate, ffn1_w_up, ffn1_w_down, mix_norm, w_in, attn_sink,
           t5_bias, na_bias, hgrn_lb_logits, hgrn_norm, w_branch_a, w_branch_b, w_branch_c, w_out,
           ffn2_norm, ffn2_w_gate, ffn2_w_up, ffn2_w_down, final_norm):
    params = dict(ffn1_norm=ffn1_norm, ffn1_w_gate=ffn1_w_gate, ffn1_w_up=ffn1_w_up, ffn1_w_down=ffn1_w_down,
                  mix_norm=mix_norm, w_in=w_in, attn_sink=attn_sink, t5_bias=t5_bias, na_bias=na_bias,
                  hgrn_lb_logits=hgrn_lb_logits, hgrn_norm=hgrn_norm, w_branch_a=w_branch_a,
                  w_branch_b=w_branch_b, w_branch_c=w_branch_c, w_out=w_out, ffn2_norm=ffn2_norm,
                  ffn2_w_gate=ffn2_w_gate, ffn2_w_up=ffn2_w_up, ffn2_w_down=ffn2_w_down, final_norm=final_norm)
    d = x_prompt.shape[-1]
    segs = (x_prompt.shape[:2], x_sample.shape[:2])
    for _, t in segs:
        assert t % NA_GROUP == 0 and t // NA_GROUP >= 3 and t // GRID_W >= NA_ROWS
    x = jnp.concatenate([x_prompt.reshape(-1, d), x_sample.reshape(-1, d)], axis=0)
    y = _trunk(x, segs, params)
    n_prompt = x_prompt.shape[0] * x_prompt.shape[1]
    return y[:n_prompt].reshape(x_prompt.shape), y[n_prompt:].reshape(x_sample.shape)
```

```python
import functools
import math

import numpy as np
import jax
import jax.numpy as jnp
from jax import lax
from jax.experimental import pallas as pl
from jax.experimental.pallas import tpu as pltpu

F32 = jnp.float32
BF16 = jnp.bfloat16

HEAD_DIM = 128
EPS = 1e-6
A_HEADS = 8
A_KV_HEADS = 2
WINDOW = 128
WIN_BLOCK = 128
T5_BUCKETS = 32
T5_MAX_DIST = 128
B_HEADS = 4
GRID_W = 64
NA_ROWS = 8
NA_COLS = 16
C_HEADS = 4
GATE_COLS = 48
MIX_COLS = 44
COL_A_Q, COL_A_K, COL_A_V = (GATE_COLS + c for c in (0, 8, 10))
COL_B_Q, COL_B_K, COL_B_V = (GATE_COLS + c for c in (12, 16, 20))
COL_C_FF, COL_C_FB, COL_C_I, COL_C_Q, COL_C_G = (GATE_COLS + c for c in (24, 28, 32, 36, 40))

NEG_BIG = -1e30
V7X_VMEM_LIMIT = 60 * 1024 * 1024
HG_TILE = 128
HG_BLK = 16
NA_GROUP = 4 * GRID_W
WIN_STEP = 2 * WIN_BLOCK


def _cparams(sem):
    return pltpu.CompilerParams(dimension_semantics=sem, vmem_limit_bytes=V7X_VMEM_LIMIT)


def _seq_pos(gb, blk, segs):
    (n_seq0, t0), (_, t1) = segs
    n0, n1 = t0 // blk, t1 // blk
    tot0 = n_seq0 * n0
    in0 = gb < tot0
    local = jnp.where(in0, gb % n0, (gb - tot0) % n1)
    return local, jnp.where(in0, n0, n1)


def _ffn_kernel(*refs, norm_in, emit_o):
    refs = list(refs)
    x_ref = refs.pop(0)
    n_ref = refs.pop(0) if norm_in else None
    g1_ref = None if norm_in else refs.pop(0)
    wg_ref, wu_ref, wd_ref, g2_ref = refs[:4]
    del refs[:4]
    o_ref = refs.pop(0) if emit_o else None
    o2_ref = refs.pop(0)
    if not norm_in:
        n_ref = refs.pop(0)
    acc_ref = o_ref if emit_o else o2_ref
    j = pl.program_id(1)

    @pl.when(j == 0)
    def _():
        if not norm_in:
            x = x_ref[...]
            ms = jnp.mean(x * x, axis=-1, keepdims=True)
            n_ref[...] = (x * lax.rsqrt(ms + EPS) * g1_ref[...]).astype(BF16)
        acc_ref[...] = jnp.zeros_like(acc_ref)

    n = n_ref[...]
    g = jnp.dot(n, wg_ref[...], preferred_element_type=F32)
    u = jnp.dot(n, wu_ref[...], preferred_element_type=F32)
    a = (g * jax.nn.sigmoid(g) * u).astype(BF16)
    acc_ref[...] += jnp.dot(a, wd_ref[...], preferred_element_type=F32)

    @pl.when(j == pl.num_programs(1) - 1)
    def _():
        y = x_ref[...] + acc_ref[...]
        if emit_o:
            o_ref[...] = y
        ms = jnp.mean(y * y, axis=-1, keepdims=True)
        o2_ref[...] = (y * lax.rsqrt(ms + EPS) * g2_ref[...]).astype(o2_ref.dtype)


def _ffn(x, n_or_gain1, wg, wu, wd, layer, gain2, tm, tf, *, norm_in, emit_o):
    n_tok, d = x.shape
    d_ff = wg.shape[-1]
    tok_spec = pl.BlockSpec((tm, d), lambda i, j: (i, 0))
    vec_spec = pl.BlockSpec((1, d), lambda i, j: (0, 0))
    out_shape = [jax.ShapeDtypeStruct((n_tok, d), BF16 if emit_o else F32)]
    if emit_o:
        out_shape.insert(0, jax.ShapeDtypeStruct((n_tok, d), F32))
    return pl.pallas_call(
        functools.partial(_ffn_kernel, norm_in=norm_in, emit_o=emit_o),
        out_shape=tuple(out_shape),
        grid=(n_tok // tm, d_ff // tf),
        in_specs=[
            tok_spec,
            tok_spec if norm_in else vec_spec,
            pl.BlockSpec((None, d, tf), lambda i, j: (layer, 0, j)),
            pl.BlockSpec((None, d, tf), lambda i, j: (layer, 0, j)),
            pl.BlockSpec((None, tf, d), lambda i, j: (layer, j, 0)),
            vec_spec,
        ],
        out_specs=tuple(tok_spec for _ in out_shape),
        scratch_shapes=[] if norm_in else [pltpu.VMEM((tm, d), BF16)],
        compiler_params=_cparams(("parallel", "arbitrary")),
        name="ffn",
    )(x, n_or_gain1 if norm_in else n_or_gain1.reshape(1, d), wg, wu, wd, gain2.reshape(1, d))


def _matmul_kernel(a_ref, w_ref, o_ref):
    o_ref[...] = jnp.dot(a_ref[...], w_ref[...], preferred_element_type=F32).astype(o_ref.dtype)


def _matmul(a, w, layer, out_dtype, tm, tn):
    n_tok, k = a.shape
    n_out = w.shape[-1]
    return pl.pallas_call(
        _matmul_kernel,
        out_shape=jax.ShapeDtypeStruct((n_tok, n_out), out_dtype),
        grid=(n_tok // tm, n_out // tn),
        in_specs=[pl.BlockSpec((tm, k), lambda i, j: (i, 0)),
                  pl.BlockSpec((None, k, tn), lambda i, j: (layer, 0, j))],
        out_specs=pl.BlockSpec((tm, tn), lambda i, j: (i, j)),
        compiler_params=_cparams(("parallel", "arbitrary")),
        name="proj_in",
    )(a, w)


def _wattn_kernel(sink_ref, q_ref, kp_ref, kc_ref, kn_ref, vp_ref, vc_ref, vn_ref, bias_ref, o_ref, s_ref, e_ref,
                  *, segs):
    local, n_loc = _seq_pos(pl.program_id(0), WIN_STEP, segs)
    variant = (jnp.where(local == 0, 0, 1), jnp.where(local == n_loc - 1, 2, 1))
    group = A_HEADS // A_KV_HEADS
    gw = group * WIN_BLOCK
    scale = HEAD_DIM ** -0.5
    units = [(blk, kv) for blk in range(WIN_STEP // WIN_BLOCK) for kv in range(A_KV_HEADS)]

    def keys_of(p_ref, c_ref, n_ref, blk, kv):
        cs = slice(kv * HEAD_DIM, (kv + 1) * HEAD_DIM)
        cat = jnp.concatenate([p_ref[:, cs], c_ref[:, cs], n_ref[:, cs]], axis=0)
        return cat[blk * WIN_BLOCK:(blk + 3) * WIN_BLOCK]

    for u, (blk, kv) in enumerate(units):
        rows = slice(blk * WIN_BLOCK, (blk + 1) * WIN_BLOCK)
        qs = jnp.concatenate([q_ref[rows, h * HEAD_DIM:(h + 1) * HEAD_DIM]
                              for h in range(kv * group, (kv + 1) * group)], axis=0)
        s = lax.dot_general(qs, keys_of(kp_ref, kc_ref, kn_ref, blk, kv), (((1,), (1,)), ((), ())),
                            preferred_element_type=F32)
        s_ref[u] = s * scale + bias_ref[variant[blk], kv * gw:(kv + 1) * gw, :]
    for u, (blk, kv) in enumerate(units):
        rows = slice(blk * WIN_BLOCK, (blk + 1) * WIN_BLOCK)
        dens = []
        for i in range(group):
            hr = slice(i * WIN_BLOCK, (i + 1) * WIN_BLOCK)
            sh = s_ref[u, hr, :]
            sink = sink_ref[kv * group + i]
            m = jnp.maximum(jnp.max(sh, axis=-1, keepdims=True), sink)
            e = jnp.exp(sh - m)
            dens.append(jnp.sum(e, axis=-1, keepdims=True) + jnp.exp(sink - m))
            e_ref[u, hr, :] = e.astype(BF16)
        o4 = jnp.dot(e_ref[u], keys_of(vp_ref, vc_ref, vn_ref, blk, kv), preferred_element_type=F32)
        for i in range(group):
            h = kv * group + i
            o_ref[rows, h * HEAD_DIM:(h + 1) * HEAD_DIM] = (
                o4[i * WIN_BLOCK:(i + 1) * WIN_BLOCK] / dens[i]).astype(o_ref.dtype)


def _window_attention(proj, sink, bias_tab, segs):
    n_tok = proj.shape[0]
    nb = n_tok // WIN_BLOCK
    per_step = WIN_STEP // WIN_BLOCK
    kv_w = A_KV_HEADS * HEAD_DIM
    n_units = per_step * A_KV_HEADS
    gw = (A_HEADS // A_KV_HEADS) * WIN_BLOCK

    def kv_specs(col0):
        col = col0 // A_KV_HEADS
        return [pl.BlockSpec((WIN_BLOCK, kv_w), lambda s: (jnp.maximum(per_step * s - 1, 0), col)),
                pl.BlockSpec((WIN_STEP, kv_w), lambda s: (s, col)),
                pl.BlockSpec((WIN_BLOCK, kv_w), lambda s: (jnp.minimum(per_step * (s + 1), nb - 1), col))]

    return pl.pallas_call(
        functools.partial(_wattn_kernel, segs=segs),
        out_shape=jax.ShapeDtypeStruct((n_tok, A_HEADS * HEAD_DIM), BF16),
        grid=(n_tok // WIN_STEP,),
        in_specs=[
            pl.BlockSpec(memory_space=pltpu.SMEM),
            pl.BlockSpec((WIN_STEP, A_HEADS * HEAD_DIM), lambda s: (s, COL_A_Q // A_HEADS)),
            *kv_specs(COL_A_K), *kv_specs(COL_A_V),
            pl.BlockSpec(bias_tab.shape, lambda s: (0, 0, 0)),
        ],
        out_specs=pl.BlockSpec((WIN_STEP, A_HEADS * HEAD_DIM), lambda s: (s, 0)),
        scratch_shapes=[pltpu.VMEM((n_units, gw, 3 * WIN_BLOCK), F32),
                        pltpu.VMEM((n_units, gw, 3 * WIN_BLOCK), BF16)],
        compiler_params=_cparams(("parallel",)),
        name="window_attn",
    )(sink, proj, proj, proj, proj, proj, proj, proj, bias_tab)


def _t5_bucket(rel):
    nb = T5_BUCKETS // 2
    max_exact = nb // 2
    base = jnp.where(rel > 0, nb, 0)
    n = jnp.abs(rel)
    large = max_exact + (jnp.log(jnp.maximum(n, 1).astype(F32) / max_exact)
                         / math.log(T5_MAX_DIST / max_exact) * (nb - max_exact)).astype(jnp.int32)
    large = jnp.minimum(large, nb - 1)
    return base + jnp.where(n < max_exact, n, large)


def _window_bias_table(t5_bias):
    qi = jnp.arange(WIN_BLOCK)[:, None]
    si = jnp.arange(3 * WIN_BLOCK)[None, :]
    rel = si - WIN_BLOCK - qi
    onehot = (_t5_bucket(rel)[None] == jnp.arange(T5_BUCKETS)[:, None, None]).astype(F32)
    bias = jnp.einsum('bh,bqs->hqs', t5_bias.astype(F32), onehot, precision=lax.Precision.HIGHEST)
    band = jnp.abs(rel) <= WINDOW
    variants = []
    for lo_ok, hi_ok in ((False, True), (True, True), (True, False)):
        ok = band & ((si >= WIN_BLOCK) | lo_ok) & ((si < 2 * WIN_BLOCK) | hi_ok)
        variants.append(jnp.where(ok[None], bias, NEG_BIG).reshape(A_HEADS * WIN_BLOCK, 3 * WIN_BLOCK))
    return jnp.stack(variants)


def _na_window(g, segs):
    lg, ng = _seq_pos(g, NA_GROUP, segs)
    return lg, ng, jnp.clip(lg - 1, 0, ng - 3)


def _na_kernel(q_ref, k0_ref, k1_ref, k2_ref, v0_ref, v1_ref, v2_ref, bias_ref, o_ref, s_ref):
    scale = HEAD_DIM ** -0.5
    heads = [slice(h * HEAD_DIM, (h + 1) * HEAD_DIM) for h in range(B_HEADS)]
    for h, cs in enumerate(heads):
        kcat = jnp.concatenate([k0_ref[:, cs], k1_ref[:, cs], k2_ref[:, cs]], axis=0)
        s = lax.dot_general(q_ref[:, cs], kcat, (((1,), (1,)), ((), ())), preferred_element_type=F32)
        s_ref[h] = s * scale + bias_ref[h]
    for h, cs in enumerate(heads):
        vcat = jnp.concatenate([v0_ref[:, cs], v1_ref[:, cs], v2_ref[:, cs]], axis=0)
        s = s_ref[h]
        m = jnp.max(s, axis=-1, keepdims=True)
        e = jnp.exp(s - m)
        den = jnp.sum(e, axis=-1, keepdims=True)
        oh = jnp.dot(e.astype(BF16), vcat, preferred_element_type=F32)
        o_ref[:, cs] = (oh / den).astype(o_ref.dtype)


def _neighbourhood_attention(proj, bias_tab, segs):
    n_tok = proj.shape[0]
    width = B_HEADS * HEAD_DIM

    def kv_spec(col0, j):
        def index_map(g):
            lg, _, lo = _na_window(g, segs)
            return (g - lg + lo + j, col0 // B_HEADS)
        return pl.BlockSpec((NA_GROUP, width), index_map)

    def bias_map(g):
        lg, ng, _ = _na_window(g, segs)
        return (jnp.where(lg == 0, 0, jnp.where(lg == ng - 1, 2, 1)), 0, 0, 0)

    return pl.pallas_call(
        _na_kernel,
        out_shape=jax.ShapeDtypeStruct((n_tok, width), BF16),
        grid=(n_tok // NA_GROUP,),
        in_specs=[
            pl.BlockSpec((NA_GROUP, width), lambda g: (g, COL_B_Q // B_HEADS)),
            kv_spec(COL_B_K, 0), kv_spec(COL_B_K, 1), kv_spec(COL_B_K, 2),
            kv_spec(COL_B_V, 0), kv_spec(COL_B_V, 1), kv_spec(COL_B_V, 2),
            pl.BlockSpec((None, B_HEADS, NA_GROUP, 3 * NA_GROUP), bias_map),
        ],
        out_specs=pl.BlockSpec((NA_GROUP, width), lambda g: (g, 0)),
        scratch_shapes=[pltpu.VMEM((B_HEADS, NA_GROUP, 3 * NA_GROUP), F32)],
        compiler_params=_cparams(("parallel",)),
        name="na_attn",
    )(proj, proj, proj, proj, proj, proj, proj, bias_tab)


def _na_bias_table(rel_table):
    rows_q = NA_GROUP // GRID_W
    rows_k = 3 * rows_q
    c = np.arange(GRID_W)
    col_start = np.clip(c - NA_COLS // 2, 0, GRID_W - NA_COLS)
    col_ok = (c[None, :] >= col_start[:, None]) & (c[None, :] < col_start[:, None] + NA_COLS)
    dc = np.clip(c[None, :] - c[:, None], -(NA_COLS - 1), NA_COLS - 1) + NA_COLS - 1
    onehot = (dc[None] == np.arange(2 * NA_COLS - 1)[:, None, None]).astype(np.float32)
    by_dr = jnp.einsum('hrd,dqk->hrqk', rel_table.astype(F32), jnp.asarray(onehot),
                       precision=lax.Precision.HIGHEST)
    by_dr = jnp.where(jnp.asarray(col_ok)[None, None], by_dr, NEG_BIG)
    by_dr = jnp.concatenate([by_dr, jnp.full_like(by_dr[:, :1], NEG_BIG)], axis=1)
    cfgs = ([(0, NA_ROWS - 1 - a) for a in range(rows_q)],
            [(a, NA_ROWS // 2 - 1) for a in range(rows_q)],
            [(rows_q, NA_ROWS // 2 - 1 - a) for a in range(rows_q)])
    idx = np.full((3, rows_q, rows_k), 2 * NA_ROWS - 1, np.int32)
    for ci, cfg in enumerate(cfgs):
        for a, (off, dr0) in enumerate(cfg):
            for j in range(NA_ROWS):
                idx[ci, a, off + j] = dr0 + j
    tab = jnp.take(by_dr, jnp.asarray(idx.reshape(-1)), axis=1)
    tab = tab.reshape(B_HEADS, 3, rows_q, rows_k, GRID_W, GRID_W).transpose(1, 0, 2, 4, 3, 5)
    return tab.reshape(3, B_HEADS, NA_GROUP, 3 * NA_GROUP)


def _split3(x):
    hi = x.astype(BF16)
    r1 = x - hi.astype(F32)
    mid = r1.astype(BF16)
    lo = (r1 - mid.astype(F32)).astype(BF16)
    return hi, mid, lo


def _hgrn_kernel(z_ref, v_ref, q_ref, lb_ref, scan_ref, o_ref, st_ref, c3_ref, b3_ref, k3_ref, q3_ref, a_ref,
                 *, segs, reverse):
    i = pl.program_id(0)
    n_steps = pl.num_programs(0)
    n_sub = z_ref.shape[0] // HG_TILE
    step = (n_steps - 1 - i) if reverse else i
    local, n_loc = _seq_pos(step, n_sub * HG_TILE, segs)
    is_start = (local == n_loc - 1) if reverse else (local == 0)

    @pl.when(is_start)
    def _():
        st_ref[...] = jnp.zeros_like(st_ref)

    nblk = HG_TILE // HG_BLK
    half = HG_BLK // 2
    n_sc = nblk // 2
    log2e = math.log2(math.e)
    nt = (((1,), (1,)), ((), ()))

    def sel_index(t0):
        shape = (nblk, half, HG_TILE)
        t = lax.broadcasted_iota(jnp.int32, shape, 1) + t0
        j = lax.broadcasted_iota(jnp.int32, shape, 2) - lax.broadcasted_iota(jnp.int32, shape, 0) * HG_BLK
        ok = (j >= 0) & (j < HG_BLK) & ((j >= t) if reverse else (j <= t))
        return jnp.where(ok, j, -1)

    sel = (sel_index(0), sel_index(half))
    rb = lax.broadcasted_iota(jnp.int32, (HG_TILE, HG_TILE), 0) // HG_BLK
    cb = lax.broadcasted_iota(jnp.int32, (HG_TILE, HG_TILE), 1) // HG_BLK
    cross = ((rb % 2 == 0) & (cb == rb + 1)) if reverse else ((rb % 2 == 1) & (cb == rb - 1))
    blk_odd = (lax.broadcasted_iota(jnp.int32, (nblk, 1, HEAD_DIM), 0) % 2) == 1
    edge_row = 0 if reverse else HG_BLK - 1
    scan = scan_ref[...]

    def one_tile(t, carry):
        tile = (n_sub - 1 - t) if reverse else t
        tok = pl.ds(pl.multiple_of(tile * HG_TILE, HG_TILE), HG_TILE)
        heads = [slice(h * HEAD_DIM, (h + 1) * HEAD_DIM) for h in range(C_HEADS)]

        for h, cs in enumerate(heads):
            lb = lb_ref[:, cs]
            f = lb + (1.0 - lb) * jax.nn.sigmoid(z_ref[tok, cs].astype(F32))
            k3 = (1.0 - f).reshape(nblk, HG_BLK, HEAD_DIM)
            pieces = jnp.concatenate(_split3(jnp.log(f)), axis=1)
            bm = jnp.dot(scan, pieces, preferred_element_type=F32)
            b = bm[:, :HEAD_DIM] + bm[:, HEAD_DIM:2 * HEAD_DIM] + bm[:, 2 * HEAD_DIM:]
            b3 = (b * log2e).reshape(nblk, HG_BLK, HEAD_DIM)
            k3_ref[h] = k3
            b3_ref[h] = b3
            c3_ref[h] = b3 - jnp.log2(jnp.maximum(k3, 0.0))
            q3_ref[h] = (q_ref[tok, cs].astype(F32) * (HEAD_DIM ** -0.5)).reshape(nblk, HG_BLK, HEAD_DIM)

        for h in range(C_HEADS):
            halves = ((q3_ref[h, :, :half, :], b3_ref[h, :, :half, :]), (q3_ref[h, :, half:, :], b3_ref[h, :, half:, :]))
            a_half = [jnp.zeros((nblk, half, HG_TILE), F32), jnp.zeros((nblk, half, HG_TILE), F32)]
            for j in range(HG_BLK):
                cj = c3_ref[h, :, j:j + 1, :]
                for hi in range(2):
                    if (j < half * hi) if reverse else (j > half * hi + half - 1):
                        continue
                    qx, bx = halves[hi]
                    p = qx * jnp.exp2(bx - cj)
                    r = jnp.sum(p, axis=-1, keepdims=True)
                    a_half[hi] = jnp.where(sel[hi] == j, r, a_half[hi])
            a_ref[h] = jnp.concatenate(a_half, axis=1).reshape(HG_TILE, HG_TILE)

        for h, cs in enumerate(heads):
            q3, k3, b3 = q3_ref[h], k3_ref[h], b3_ref[h]
            btot = b3_ref[h, :, edge_row:edge_row + 1, :]
            v = v_ref[tok, cs]
            qp3 = q3 * jnp.exp2(b3)
            kp3 = k3 * jnp.exp2(btot - b3)
            qp = qp3.reshape(HG_TILE, HEAD_DIM).astype(BF16)
            kp = kp3.reshape(HG_TILE, HEAD_DIM).astype(BF16)
            g = lax.dot_general(qp, kp, nt, preferred_element_type=F32)
            a_all = jnp.where(cross, g, a_ref[h]).astype(BF16)
            o_acc = jnp.dot(a_all, v, preferred_element_type=F32)

            dec = jnp.exp2(btot)
            one = jnp.ones_like(dec[:1])
            dprev = jnp.concatenate([one, dec[:-1]], axis=0)
            dnext = jnp.concatenate([dec[1:], one], axis=0)
            if reverse:
                qscale, kscale = jnp.where(blk_odd, 1.0, dnext), jnp.where(blk_odd, dprev, 1.0)
            else:
                qscale, kscale = jnp.where(blk_odd, dprev, 1.0), jnp.where(blk_odd, 1.0, dnext)
            qpp = (qp3 * qscale).reshape(HG_TILE, HEAD_DIM).astype(BF16)
            kpp = (kp3 * kscale).reshape(HG_TILE, HEAD_DIM).astype(BF16)
            st = st_ref[h]
            inter = [None] * n_sc
            for m in (range(n_sc - 1, -1, -1) if reverse else range(n_sc)):
                rows = slice(2 * m * HG_BLK, 2 * (m + 1) * HG_BLK)
                inter[m] = lax.dot_general(qpp[rows], st.astype(BF16), nt, preferred_element_type=F32)
                upd = lax.dot_general(v[rows], kpp[rows], (((0,), (0,)), ((), ())), preferred_element_type=F32)
                st = st * (dec[2 * m] * dec[2 * m + 1]) + upd
            st_ref[h] = st
            o_ref[tok, cs] = o_acc + jnp.concatenate(inter, axis=0)
        return carry

    lax.fori_loop(0, n_sub, one_tile, 0)


def _hgrn_scan_matrix(reverse):
    t = np.arange(HG_TILE)
    same = (t[:, None] // HG_BLK) == (t[None, :] // HG_BLK)
    incl = (t[None, :] >= t[:, None]) if reverse else (t[None, :] <= t[:, None])
    return jnp.asarray((same & incl).astype(np.float32), BF16)


def _hgrn_direction(proj, lower_bound, segs, reverse):
    n_tok = proj.shape[0]
    step_tok = next(s for s in (4 * HG_TILE, 2 * HG_TILE, HG_TILE) if all(t % s == 0 for _, t in segs))
    n_steps = n_tok // step_tok
    width = C_HEADS * HEAD_DIM
    scan = _hgrn_scan_matrix(reverse)

    def tok_spec(col0):
        return pl.BlockSpec((step_tok, width),
                            lambda i: ((n_steps - 1 - i) if reverse else i, col0 // C_HEADS))

    nblk = HG_TILE // HG_BLK
    return pl.pallas_call(
        functools.partial(_hgrn_kernel, segs=segs, reverse=reverse),
        out_shape=jax.ShapeDtypeStruct((n_tok, width), F32),
        grid=(n_steps,),
        in_specs=[
            tok_spec(COL_C_FB if reverse else COL_C_FF), tok_spec(COL_C_I), tok_spec(COL_C_Q),
            pl.BlockSpec((1, width), lambda i: (0, 0)),
            pl.BlockSpec(scan.shape, lambda i: (0, 0)),
        ],
        out_specs=pl.BlockSpec((step_tok, width), lambda i: ((n_steps - 1 - i) if reverse else i, 0)),
        scratch_shapes=[pltpu.VMEM((C_HEADS, HEAD_DIM, HEAD_DIM), F32)]
        + [pltpu.VMEM((C_HEADS, nblk, HG_BLK, HEAD_DIM), F32)] * 4
        + [pltpu.VMEM((C_HEADS, HG_TILE, HG_TILE), F32)],
        compiler_params=_cparams(("arbitrary",)),
        name="hgrn_bwd" if reverse else "hgrn_fwd",
    )(proj, proj, proj, lower_bound.reshape(1, width), scan)


def _merge_kernel(ga_ref, gb_ref, gc_ref, cg_ref, ya_ref, yb_ref, of_ref, ob_ref, hn_ref,
                  wa_ref, wb_ref, wc_ref, wo_ref, h_ref, g2_ref, o_ref, o2_ref):
    o = of_ref[...] + ob_ref[...]
    heads = []
    for h in range(C_HEADS):
        oh = o[:, h * HEAD_DIM:(h + 1) * HEAD_DIM]
        ms = jnp.mean(oh * oh, axis=-1, keepdims=True)
        heads.append(oh * lax.rsqrt(ms + EPS))
    cg = cg_ref[...].astype(F32)
    yc = (jnp.concatenate(heads, axis=1) * hn_ref[...] * (cg * jax.nn.sigmoid(cg))).astype(BF16)
    m = jax.nn.sigmoid(ga_ref[...].astype(F32)) * jnp.dot(ya_ref[...], wa_ref[...], preferred_element_type=F32)
    m += jax.nn.sigmoid(gb_ref[...].astype(F32)) * jnp.dot(yb_ref[...], wb_ref[...], preferred_element_type=F32)
    m += jax.nn.sigmoid(gc_ref[...].astype(F32)) * jnp.dot(yc, wc_ref[...], preferred_element_type=F32)
    y = h_ref[...] + jnp.dot(m.astype(BF16), wo_ref[...], preferred_element_type=F32)
    o_ref[...] = y
    ms = jnp.mean(y * y, axis=-1, keepdims=True)
    o2_ref[...] = (y * lax.rsqrt(ms + EPS) * g2_ref[...]).astype(o2_ref.dtype)


def _merge(proj, ya, yb, o_f, o_b, hgrn_gain, wa, wb, wc, wo, layer, h, gain2, tm):
    n_tok, d = h.shape
    cw = C_HEADS * HEAD_DIM

    def resident(w):
        return pl.BlockSpec((None,) + w.shape[1:], lambda i: (layer, 0, 0), pipeline_mode=pl.Buffered(1))

    def gate_spec(which):
        return pl.BlockSpec((tm, d), lambda i: (i, which))

    return pl.pallas_call(
        _merge_kernel,
        out_shape=(jax.ShapeDtypeStruct((n_tok, d), F32), jax.ShapeDtypeStruct((n_tok, d), BF16)),
        grid=(n_tok // tm,),
        in_specs=[
            gate_spec(0), gate_spec(1), gate_spec(2),
            pl.BlockSpec((tm, cw), lambda i: (i, COL_C_G // C_HEADS)),
            pl.BlockSpec((tm, ya.shape[1]), lambda i: (i, 0)),
            pl.BlockSpec((tm, cw), lambda i: (i, 0)),
            pl.BlockSpec((tm, cw), lambda i: (i, 0)),
            pl.BlockSpec((tm, cw), lambda i: (i, 0)),
            pl.BlockSpec((1, cw), lambda i: (0, 0)),
            resident(wa), resident(wb), resident(wc), resident(wo),
            pl.BlockSpec((tm, d), lambda i: (i, 0)),
            pl.BlockSpec((1, d), lambda i: (0, 0)),
        ],
        out_specs=(pl.BlockSpec((tm, d), lambda i: (i, 0)), pl.BlockSpec((tm, d), lambda i: (i, 0))),
        compiler_params=_cparams(("parallel",)),
        name="merge_out",
    )(proj, proj, proj, proj, ya, yb, o_f, o_b, hgrn_gain.reshape(1, cw), wa, wb, wc, wo, h,
      gain2.reshape(1, d))


def _tile(n, candidates):
    return next(t for t in candidates if n % t == 0)


def _trunk(x, segs, p):
    depth = p["w_in"].shape[0]
    n_tok = x.shape[0]
    tm_ffn = _tile(n_tok, (768, 512, 256, 128))
    tm_proj = _tile(n_tok, (2048, 1024, 512, 256, 128))
    tm_merge = _tile(n_tok, (256, 128))
    d_ff = p["ffn1_w_gate"].shape[-1]
    tf = 512 if d_ff % 512 == 0 else 256
    mix_w = MIX_COLS * HEAD_DIM

    lb_p = jax.nn.softmax(p["hgrn_lb_logits"].astype(F32), axis=1)
    lower_bounds = jnp.cumsum(lb_p, axis=1) - lb_p[:, :1]
    win_tab = _window_bias_table(p["t5_bias"])
    bf = lambda name: p[name].astype(BF16)
    w = {name: bf(name) for name in ("ffn1_w_gate", "ffn1_w_up", "ffn1_w_down", "w_in", "w_branch_a",
                                      "w_branch_b", "w_branch_c", "w_out", "ffn2_w_gate", "ffn2_w_up",
                                      "ffn2_w_down")}
    w_in = jnp.concatenate([w["w_in"][:, :, mix_w:], w["w_in"][:, :, :mix_w]], axis=-1)
    ffn1_w = (w["ffn1_w_gate"], w["ffn1_w_up"], w["ffn1_w_down"] * 0.5)
    ffn2_w = (w["ffn2_w_gate"], w["ffn2_w_up"], w["ffn2_w_down"] * 0.5)
    n = out = None
    for l in range(depth):
        last = l == depth - 1
        h, u = _ffn(x, p["ffn1_norm"][l] if l == 0 else n, *ffn1_w, l, p["mix_norm"][l], tm_ffn, tf,
                    norm_in=l > 0, emit_o=True)
        proj = _matmul(u, w_in, l, BF16, tm_proj, 512)
        ya = _window_attention(proj, p["attn_sink"][l].astype(F32), win_tab, segs)
        yb = _neighbourhood_attention(proj, _na_bias_table(p["na_bias"][l]), segs)
        o_f = _hgrn_direction(proj, lower_bounds[0, l], segs, reverse=False)
        o_b = _hgrn_direction(proj, lower_bounds[1, l], segs, reverse=True)
        h, n = _merge(proj, ya, yb, o_f, o_b, p["hgrn_norm"][l], w["w_branch_a"], w["w_branch_b"],
                      w["w_branch_c"], w["w_out"], l, h, p["ffn2_norm"][l], tm_merge)
        if last:
            (out,) = _ffn(h, n, *ffn2_w, l, p["final_norm"], tm_ffn, tf, norm_in=True, emit_o=False)
        else:
            x, n = _ffn(h, n, *ffn2_w, l, p["ffn1_norm"][l + 1], tm_ffn, tf, norm_in=True, emit_o=True)
    return out


def kernel(x_prompt, x_sample, ffn1_norm, ffn1_w_gate, ffn1_w_up, ffn1_w_down, mix_norm, w_in, attn_sink,
           t5_bias, na_bias, hgrn_lb_logits, hgrn_norm, w_branch_a, w_branch_b, w_branch_c, w_out,
           ffn2_norm, ffn2_w_gate, ffn2_w_up, ffn2_w_down, final_norm):
    params = dict(ffn1_norm=ffn1_norm, ffn1_w_gate=ffn1_w_gate, ffn1_w_up=ffn1_w_up, ffn1_w_down=ffn1_w_down,
                  mix_norm=mix_norm, w_in=w_in, attn_sink=attn_sink, t5_bias=t5_bias, na_bias=na_bias,
                  hgrn_lb_logits=hgrn_lb_logits, hgrn_norm=hgrn_norm, w_branch_a=w_branch_a,
                  w_branch_b=w_branch_b, w_branch_c=w_branch_c, w_out=w_out, ffn2_norm=ffn2_norm,
                  ffn2_w_gate=ffn2_w_gate, ffn2_w_up=ffn2_w_up, ffn2_w_down=ffn2_w_down, final_norm=final_norm)
    d = x_prompt.shape[-1]
    segs = (x_prompt.shape[:2], x_sample.shape[:2])
    for _, t in segs:
        assert t % NA_GROUP == 0 and t // NA_GROUP >= 3 and t // GRID_W >= NA_ROWS and t % WIN_STEP == 0
    x = jnp.concatenate([x_prompt.reshape(-1, d), x_sample.reshape(-1, d)], axis=0)
    y = _trunk(x, segs, params)
    n_prompt = x_prompt.shape[0] * x_prompt.shape[1]
    return y[:n_prompt].reshape(x_prompt.shape), y[n_prompt:].reshape(x_sample.shape)
```

```python
import functools
import math

import numpy as np
import jax
import jax.numpy as jnp
from jax import lax
from jax.experimental import pallas as pl
from jax.experimental.pallas import tpu as pltpu

F32 = jnp.float32
BF16 = jnp.bfloat16

HEAD_DIM = 128
EPS = 1e-6
A_HEADS = 8
A_KV_HEADS = 2
WINDOW = 128
WIN_BLOCK = 128
T5_BUCKETS = 32
T5_MAX_DIST = 128
B_HEADS = 4
GRID_W = 64
NA_ROWS = 8
NA_COLS = 16
C_HEADS = 4
GATE_COLS = 48
MIX_COLS = 44
COL_A_Q, COL_A_K, COL_A_V = (GATE_COLS + c for c in (0, 8, 10))
COL_B_Q, COL_B_K, COL_B_V = (GATE_COLS + c for c in (12, 16, 20))
COL_C_FF, COL_C_FB, COL_C_I, COL_C_Q, COL_C_G = (GATE_COLS + c for c in (24, 28, 32, 36, 40))

NEG_BIG = -1e30
LOG2E = math.log2(math.e)
V7X_VMEM_LIMIT = 60 * 1024 * 1024
HG_TILE = 128
HG_BLK = 16
NA_GROUP = 4 * GRID_W
WIN_STEP = 2 * WIN_BLOCK


def _cparams(sem):
    return pltpu.CompilerParams(dimension_semantics=sem, vmem_limit_bytes=V7X_VMEM_LIMIT)


def _seq_pos(gb, blk, segs):
    (n_seq0, t0), (_, t1) = segs
    n0, n1 = t0 // blk, t1 // blk
    tot0 = n_seq0 * n0
    in0 = gb < tot0
    local = jnp.where(in0, gb % n0, (gb - tot0) % n1)
    return local, jnp.where(in0, n0, n1)


def _normed(y, gain_ref, dtype):
    ms = jnp.mean(y * y, axis=-1, keepdims=True)
    return (y * lax.rsqrt(ms + EPS) * gain_ref[...]).astype(dtype)


def _ffn_accumulate(n_ref, wg_ref, wu_ref, wd_ref, acc_ref):
    n = n_ref[...]
    g = jnp.dot(n, wg_ref[...], preferred_element_type=F32)
    u = jnp.dot(n, wu_ref[...], preferred_element_type=F32)
    a = (g * jax.nn.sigmoid(g) * u).astype(BF16)
    acc_ref[...] += jnp.dot(a, wd_ref[...], preferred_element_type=F32)


def _ffn_first_kernel(xp_ref, xs_ref, g1_ref, wg_ref, wu_ref, wd_ref, g2_ref, o_ref, o2_ref, n_ref, *, prompt_tiles):
    i, j = pl.program_id(0), pl.program_id(1)

    @pl.when(j == 0)
    def _():
        x = jnp.where(i < prompt_tiles, xp_ref[...], xs_ref[...])
        n_ref[...] = _normed(x, g1_ref, BF16)
        o_ref[...] = x

    _ffn_accumulate(n_ref, wg_ref, wu_ref, wd_ref, o_ref)

    @pl.when(j == pl.num_programs(1) - 1)
    def _():
        o2_ref[...] = _normed(o_ref[...], g2_ref, o2_ref.dtype)


def _ffn_mid_kernel(x_ref, n_ref, wg_ref, wu_ref, wd_ref, g2_ref, o_ref, o2_ref):
    j = pl.program_id(1)

    @pl.when(j == 0)
    def _():
        o_ref[...] = x_ref[...]

    _ffn_accumulate(n_ref, wg_ref, wu_ref, wd_ref, o_ref)

    @pl.when(j == pl.num_programs(1) - 1)
    def _():
        o2_ref[...] = _normed(o_ref[...], g2_ref, o2_ref.dtype)


def _ffn_last_kernel(x_ref, n_ref, wg_ref, wu_ref, wd_ref, g2_ref, yp_ref, ys_ref, acc_ref, *, prompt_tiles):
    i, j = pl.program_id(0), pl.program_id(1)
    done = j == pl.num_programs(1) - 1

    @pl.when(j == 0)
    def _():
        acc_ref[...] = x_ref[...]

    _ffn_accumulate(n_ref, wg_ref, wu_ref, wd_ref, acc_ref)

    @pl.when(done & (i < prompt_tiles))
    def _():
        yp_ref[...] = _normed(acc_ref[...], g2_ref, yp_ref.dtype)

    @pl.when(done & (i >= prompt_tiles))
    def _():
        ys_ref[...] = _normed(acc_ref[...], g2_ref, ys_ref.dtype)


def _ffn(kind, acts, weights, layer, gain2, tm, tf, n_prompt=None, gain1=None):
    wg, wu, wd = weights
    d, d_ff = wg.shape[1:]
    n_tok = sum(a.shape[0] for a in acts) if kind == "first" else acts[0].shape[0]
    tok_spec = pl.BlockSpec((tm, d), lambda i, j: (i, 0))
    vec_spec = pl.BlockSpec((1, d), lambda i, j: (0, 0))
    w_specs = [pl.BlockSpec((None, d, tf), lambda i, j: (layer, 0, j)),
               pl.BlockSpec((None, d, tf), lambda i, j: (layer, 0, j)),
               pl.BlockSpec((None, tf, d), lambda i, j: (layer, j, 0))]
    if kind != "mid":
        assert n_prompt % tm == 0
        pt = n_prompt // tm
        prompt_spec = pl.BlockSpec((tm, d), lambda i, j: (jnp.minimum(i, pt - 1), 0))
        sample_spec = pl.BlockSpec((tm, d), lambda i, j: (jnp.maximum(i - pt, 0), 0))
    both = (jax.ShapeDtypeStruct((n_tok, d), F32), jax.ShapeDtypeStruct((n_tok, d), BF16))
    if kind == "first":
        body = functools.partial(_ffn_first_kernel, prompt_tiles=pt)
        operands = (*acts, gain1.reshape(1, d))
        in_specs = [prompt_spec, sample_spec, vec_spec]
        out_shape, out_specs, scratch = both, (tok_spec, tok_spec), [pltpu.VMEM((tm, d), BF16)]
    elif kind == "mid":
        body, operands, in_specs = _ffn_mid_kernel, acts, [tok_spec, tok_spec]
        out_shape, out_specs, scratch = both, (tok_spec, tok_spec), []
    else:
        body = functools.partial(_ffn_last_kernel, prompt_tiles=pt)
        operands, in_specs = acts, [tok_spec, tok_spec]
        out_shape = (jax.ShapeDtypeStruct((n_prompt, d), F32), jax.ShapeDtypeStruct((n_tok - n_prompt, d), F32))
        out_specs, scratch = (prompt_spec, sample_spec), [pltpu.VMEM((tm, d), F32)]
    return pl.pallas_call(
        body,
        out_shape=out_shape,
        grid=(n_tok // tm, d_ff // tf),
        in_specs=[*in_specs, *w_specs, vec_spec],
        out_specs=out_specs,
        scratch_shapes=scratch,
        compiler_params=_cparams(("arbitrary", "arbitrary")),
        name="ffn_" + kind,
    )(*operands, wg, wu, wd, gain2.reshape(1, d))


def _matmul_kernel(a_ref, w_ref, o_ref):
    o_ref[...] = jnp.dot(a_ref[...], w_ref[...], preferred_element_type=F32).astype(o_ref.dtype)


def _matmul(a, w, layer, out_dtype, tm, tn):
    n_tok, k = a.shape
    n_out = w.shape[-1]
    return pl.pallas_call(
        _matmul_kernel,
        out_shape=jax.ShapeDtypeStruct((n_tok, n_out), out_dtype),
        grid=(n_tok // tm, n_out // tn),
        in_specs=[pl.BlockSpec((tm, k), lambda i, j: (i, 0)),
                  pl.BlockSpec((None, k, tn), lambda i, j: (layer, 0, j))],
        out_specs=pl.BlockSpec((tm, tn), lambda i, j: (i, j)),
        compiler_params=_cparams(("parallel", "arbitrary")),
        name="proj_in",
    )(a, w)


def _wattn_kernel(sink_ref, q_ref, kp_ref, kc_ref, kn_ref, vp_ref, vc_ref, vn_ref, bias_ref, o_ref, s_ref, e_ref,
                  *, segs):
    local, n_loc = _seq_pos(pl.program_id(0), WIN_STEP, segs)
    variant = (jnp.where(local == 0, 0, 1), jnp.where(local == n_loc - 1, 2, 1))
    group = A_HEADS // A_KV_HEADS
    gw = group * WIN_BLOCK
    scale = HEAD_DIM ** -0.5 * LOG2E
    units = [(blk, kv) for blk in range(WIN_STEP // WIN_BLOCK) for kv in range(A_KV_HEADS)]

    def keys_of(p_ref, c_ref, n_ref, blk, kv):
        cs = slice(kv * HEAD_DIM, (kv + 1) * HEAD_DIM)
        cat = jnp.concatenate([p_ref[:, cs], c_ref[:, cs], n_ref[:, cs]], axis=0)
        return cat[blk * WIN_BLOCK:(blk + 3) * WIN_BLOCK]

    for u, (blk, kv) in enumerate(units):
        rows = slice(blk * WIN_BLOCK, (blk + 1) * WIN_BLOCK)
        qs = jnp.concatenate([q_ref[rows, h * HEAD_DIM:(h + 1) * HEAD_DIM]
                              for h in range(kv * group, (kv + 1) * group)], axis=0)
        s = lax.dot_general(qs, keys_of(kp_ref, kc_ref, kn_ref, blk, kv), (((1,), (1,)), ((), ())),
                            preferred_element_type=F32)
        s_ref[u] = s * scale + bias_ref[variant[blk], kv * gw:(kv + 1) * gw, :]
    for u, (blk, kv) in enumerate(units):
        rows = slice(blk * WIN_BLOCK, (blk + 1) * WIN_BLOCK)
        dens = []
        for i in range(group):
            hr = slice(i * WIN_BLOCK, (i + 1) * WIN_BLOCK)
            sh = s_ref[u, hr, :]
            sink = sink_ref[kv * group + i]
            m = jnp.maximum(jnp.max(sh, axis=-1, keepdims=True), sink)
            e = jnp.exp2(sh - m)
            dens.append(jnp.sum(e, axis=-1, keepdims=True) + jnp.exp2(sink - m))
            e_ref[u, hr, :] = e.astype(BF16)
        o4 = jnp.dot(e_ref[u], keys_of(vp_ref, vc_ref, vn_ref, blk, kv), preferred_element_type=F32)
        for i in range(group):
            h = kv * group + i
            o_ref[rows, h * HEAD_DIM:(h + 1) * HEAD_DIM] = (
                o4[i * WIN_BLOCK:(i + 1) * WIN_BLOCK] / dens[i]).astype(o_ref.dtype)


def _window_attention(proj, sink, bias_tab, segs):
    n_tok = proj.shape[0]
    nb = n_tok // WIN_BLOCK
    per_step = WIN_STEP // WIN_BLOCK
    kv_w = A_KV_HEADS * HEAD_DIM
    n_units = per_step * A_KV_HEADS
    gw = (A_HEADS // A_KV_HEADS) * WIN_BLOCK

    def kv_specs(col0):
        col = col0 // A_KV_HEADS
        return [pl.BlockSpec((WIN_BLOCK, kv_w), lambda s: (jnp.maximum(per_step * s - 1, 0), col)),
                pl.BlockSpec((WIN_STEP, kv_w), lambda s: (s, col)),
                pl.BlockSpec((WIN_BLOCK, kv_w), lambda s: (jnp.minimum(per_step * (s + 1), nb - 1), col))]

    return pl.pallas_call(
        functools.partial(_wattn_kernel, segs=segs),
        out_shape=jax.ShapeDtypeStruct((n_tok, A_HEADS * HEAD_DIM), BF16),
        grid=(n_tok // WIN_STEP,),
        in_specs=[
            pl.BlockSpec(memory_space=pltpu.SMEM),
            pl.BlockSpec((WIN_STEP, A_HEADS * HEAD_DIM), lambda s: (s, COL_A_Q // A_HEADS)),
            *kv_specs(COL_A_K), *kv_specs(COL_A_V),
            pl.BlockSpec(bias_tab.shape, lambda s: (0, 0, 0)),
        ],
        out_specs=pl.BlockSpec((WIN_STEP, A_HEADS * HEAD_DIM), lambda s: (s, 0)),
        scratch_shapes=[pltpu.VMEM((n_units, gw, 3 * WIN_BLOCK), F32),
                        pltpu.VMEM((n_units, gw, 3 * WIN_BLOCK), BF16)],
        compiler_params=_cparams(("parallel",)),
        name="window_attn",
    )(sink, proj, proj, proj, proj, proj, proj, proj, bias_tab)


def _t5_bucket(rel):
    nb = T5_BUCKETS // 2
    max_exact = nb // 2
    base = jnp.where(rel > 0, nb, 0)
    n = jnp.abs(rel)
    large = max_exact + (jnp.log(jnp.maximum(n, 1).astype(F32) / max_exact)
                         / math.log(T5_MAX_DIST / max_exact) * (nb - max_exact)).astype(jnp.int32)
    large = jnp.minimum(large, nb - 1)
    return base + jnp.where(n < max_exact, n, large)


def _window_bias_table(t5_bias):
    qi = jnp.arange(WIN_BLOCK)[:, None]
    si = jnp.arange(3 * WIN_BLOCK)[None, :]
    rel = si - WIN_BLOCK - qi
    onehot = (_t5_bucket(rel)[None] == jnp.arange(T5_BUCKETS)[:, None, None]).astype(F32)
    bias = jnp.einsum('bh,bqs->hqs', t5_bias.astype(F32), onehot, precision=lax.Precision.HIGHEST)
    band = jnp.abs(rel) <= WINDOW
    variants = []
    for lo_ok, hi_ok in ((False, True), (True, True), (True, False)):
        ok = band & ((si >= WIN_BLOCK) | lo_ok) & ((si < 2 * WIN_BLOCK) | hi_ok)
        variants.append(jnp.where(ok[None], bias * LOG2E, NEG_BIG).reshape(A_HEADS * WIN_BLOCK, 3 * WIN_BLOCK))
    return jnp.stack(variants)


def _na_window(g, segs):
    lg, ng = _seq_pos(g, NA_GROUP, segs)
    return lg, ng, jnp.clip(lg - 1, 0, ng - 3)


def _na_kernel(q_ref, k0_ref, k1_ref, k2_ref, v0_ref, v1_ref, v2_ref, bias_ref, o_ref, s_ref):
    scale = HEAD_DIM ** -0.5 * LOG2E
    heads = [slice(h * HEAD_DIM, (h + 1) * HEAD_DIM) for h in range(B_HEADS)]
    for h, cs in enumerate(heads):
        kcat = jnp.concatenate([k0_ref[:, cs], k1_ref[:, cs], k2_ref[:, cs]], axis=0)
        s = lax.dot_general(q_ref[:, cs], kcat, (((1,), (1,)), ((), ())), preferred_element_type=F32)
        s_ref[h] = s * scale + bias_ref[h]
    for h, cs in enumerate(heads):
        vcat = jnp.concatenate([v0_ref[:, cs], v1_ref[:, cs], v2_ref[:, cs]], axis=0)
        s = s_ref[h]
        m = jnp.max(s, axis=-1, keepdims=True)
        e = jnp.exp2(s - m)
        den = jnp.sum(e, axis=-1, keepdims=True)
        oh = jnp.dot(e.astype(BF16), vcat, preferred_element_type=F32)
        o_ref[:, cs] = (oh / den).astype(o_ref.dtype)


def _neighbourhood_attention(proj, bias_tab, segs):
    n_tok = proj.shape[0]
    width = B_HEADS * HEAD_DIM

    def kv_spec(col0, j):
        def index_map(g):
            lg, _, lo = _na_window(g, segs)
            return (g - lg + lo + j, col0 // B_HEADS)
        return pl.BlockSpec((NA_GROUP, width), index_map)

    def bias_map(g):
        lg, ng, _ = _na_window(g, segs)
        return (jnp.where(lg == 0, 0, jnp.where(lg == ng - 1, 2, 1)), 0, 0, 0)

    return pl.pallas_call(
        _na_kernel,
        out_shape=jax.ShapeDtypeStruct((n_tok, width), BF16),
        grid=(n_tok // NA_GROUP,),
        in_specs=[
            pl.BlockSpec((NA_GROUP, width), lambda g: (g, COL_B_Q // B_HEADS)),
            kv_spec(COL_B_K, 0), kv_spec(COL_B_K, 1), kv_spec(COL_B_K, 2),
            kv_spec(COL_B_V, 0), kv_spec(COL_B_V, 1), kv_spec(COL_B_V, 2),
            pl.BlockSpec((None, B_HEADS, NA_GROUP, 3 * NA_GROUP), bias_map),
        ],
        out_specs=pl.BlockSpec((NA_GROUP, width), lambda g: (g, 0)),
        scratch_shapes=[pltpu.VMEM((B_HEADS, NA_GROUP, 3 * NA_GROUP), F32)],
        compiler_params=_cparams(("parallel",)),
        name="na_attn",
    )(proj, proj, proj, proj, proj, proj, proj, bias_tab)


def _na_bias_table(rel_table):
    rows_q = NA_GROUP // GRID_W
    rows_k = 3 * rows_q
    c = np.arange(GRID_W)
    col_start = np.clip(c - NA_COLS // 2, 0, GRID_W - NA_COLS)
    col_ok = (c[None, :] >= col_start[:, None]) & (c[None, :] < col_start[:, None] + NA_COLS)
    dc = np.clip(c[None, :] - c[:, None], -(NA_COLS - 1), NA_COLS - 1) + NA_COLS - 1
    onehot = (dc[None] == np.arange(2 * NA_COLS - 1)[:, None, None]).astype(np.float32)
    by_dr = jnp.einsum('hrd,dqk->hrqk', rel_table.astype(F32), jnp.asarray(onehot),
                       precision=lax.Precision.HIGHEST)
    by_dr = jnp.where(jnp.asarray(col_ok)[None, None], by_dr * LOG2E, NEG_BIG)
    by_dr = jnp.concatenate([by_dr, jnp.full_like(by_dr[:, :1], NEG_BIG)], axis=1)
    cfgs = ([(0, NA_ROWS - 1 - a) for a in range(rows_q)],
            [(a, NA_ROWS // 2 - 1) for a in range(rows_q)],
            [(rows_q, NA_ROWS // 2 - 1 - a) for a in range(rows_q)])
    idx = np.full((3, rows_q, rows_k), 2 * NA_ROWS - 1, np.int32)
    for ci, cfg in enumerate(cfgs):
        for a, (off, dr0) in enumerate(cfg):
            for j in range(NA_ROWS):
                idx[ci, a, off + j] = dr0 + j
    tab = jnp.take(by_dr, jnp.asarray(idx.reshape(-1)), axis=1)
    tab = tab.reshape(B_HEADS, 3, rows_q, rows_k, GRID_W, GRID_W).transpose(1, 0, 2, 4, 3, 5)
    return tab.reshape(3, B_HEADS, NA_GROUP, 3 * NA_GROUP)


def _split3(x):
    hi = x.astype(BF16)
    r1 = x - hi.astype(F32)
    mid = r1.astype(BF16)
    lo = (r1 - mid.astype(F32)).astype(BF16)
    return hi, mid, lo


def _hgrn_kernel(z_ref, v_ref, q_ref, lb_ref, scan_ref, o_ref, st_ref, c3_ref, b3_ref, k3_ref, q3_ref, a_ref,
                 *, segs, reverse):
    i = pl.program_id(0)
    n_steps = pl.num_programs(0)
    n_sub = z_ref.shape[0] // HG_TILE
    step = (n_steps - 1 - i) if reverse else i
    local, n_loc = _seq_pos(step, n_sub * HG_TILE, segs)
    is_start = (local == n_loc - 1) if reverse else (local == 0)

    @pl.when(is_start)
    def _():
        st_ref[...] = jnp.zeros_like(st_ref)

    nblk = HG_TILE // HG_BLK
    half = HG_BLK // 2
    n_sc = nblk // 2
    log2e = math.log2(math.e)
    nt = (((1,), (1,)), ((), ()))

    def sel_index(t0):
        shape = (nblk, half, HG_TILE)
        t = lax.broadcasted_iota(jnp.int32, shape, 1) + t0
        j = lax.broadcasted_iota(jnp.int32, shape, 2) - lax.broadcasted_iota(jnp.int32, shape, 0) * HG_BLK
        ok = (j >= 0) & (j < HG_BLK) & ((j >= t) if reverse else (j <= t))
        return jnp.where(ok, j, -1)

    sel = (sel_index(0), sel_index(half))
    rb = lax.broadcasted_iota(jnp.int32, (HG_TILE, HG_TILE), 0) // HG_BLK
    cb = lax.broadcasted_iota(jnp.int32, (HG_TILE, HG_TILE), 1) // HG_BLK
    cross = ((rb % 2 == 0) & (cb == rb + 1)) if reverse else ((rb % 2 == 1) & (cb == rb - 1))
    blk_odd = (lax.broadcasted_iota(jnp.int32, (nblk, 1, HEAD_DIM), 0) % 2) == 1
    edge_row = 0 if reverse else HG_BLK - 1
    scan = scan_ref[...]

    def one_tile(t, carry):
        tile = (n_sub - 1 - t) if reverse else t
        tok = pl.ds(pl.multiple_of(tile * HG_TILE, HG_TILE), HG_TILE)
        heads = [slice(h * HEAD_DIM, (h + 1) * HEAD_DIM) for h in range(C_HEADS)]

        for h, cs in enumerate(heads):
            lb = lb_ref[:, cs]
            f = lb + (1.0 - lb) * jax.nn.sigmoid(z_ref[tok, cs].astype(F32))
            k3 = (1.0 - f).reshape(nblk, HG_BLK, HEAD_DIM)
            pieces = jnp.concatenate(_split3(jnp.log(f)), axis=1)
            bm = jnp.dot(scan, pieces, preferred_element_type=F32)
            b = bm[:, :HEAD_DIM] + bm[:, HEAD_DIM:2 * HEAD_DIM] + bm[:, 2 * HEAD_DIM:]
            b3 = (b * log2e).reshape(nblk, HG_BLK, HEAD_DIM)
            k3_ref[h] = k3
            b3_ref[h] = b3
            c3_ref[h] = b3 - jnp.log2(jnp.maximum(k3, 0.0))
            q3_ref[h] = (q_ref[tok, cs].astype(F32) * (HEAD_DIM ** -0.5)).reshape(nblk, HG_BLK, HEAD_DIM)

        for h in range(C_HEADS):
            halves = ((q3_ref[h, :, :half, :], b3_ref[h, :, :half, :]), (q3_ref[h, :, half:, :], b3_ref[h, :, half:, :]))
            a_half = [jnp.zeros((nblk, half, HG_TILE), F32), jnp.zeros((nblk, half, HG_TILE), F32)]
            for j in range(HG_BLK):
                cj = c3_ref[h, :, j:j + 1, :]
                for hi in range(2):
                    if (j < half * hi) if reverse else (j > half * hi + half - 1):
                        continue
                    qx, bx = halves[hi]
                    p = qx * jnp.exp2(bx - cj)
                    r = jnp.sum(p, axis=-1, keepdims=True)
                    a_half[hi] = jnp.where(sel[hi] == j, r, a_half[hi])
            a_ref[h] = jnp.concatenate(a_half, axis=1).reshape(HG_TILE, HG_TILE)

        for h, cs in enumerate(heads):
            q3, k3, b3 = q3_ref[h], k3_ref[h], b3_ref[h]
            btot = b3_ref[h, :, edge_row:edge_row + 1, :]
            v = v_ref[tok, cs]
            qp3 = q3 * jnp.exp2(b3)
            kp3 = k3 * jnp.exp2(btot - b3)
            qp = qp3.reshape(HG_TILE, HEAD_DIM).astype(BF16)
            kp = kp3.reshape(HG_TILE, HEAD_DIM).astype(BF16)
            g = lax.dot_general(qp, kp, nt, preferred_element_type=F32)
            a_all = jnp.where(cross, g, a_ref[h]).astype(BF16)
            o_acc = jnp.dot(a_all, v, preferred_element_type=F32)

            dec = jnp.exp2(btot)
            one = jnp.ones_like(dec[:1])
            dprev = jnp.concatenate([one, dec[:-1]], axis=0)
            dnext = jnp.concatenate([dec[1:], one], axis=0)
            if reverse:
                qscale, kscale = jnp.where(blk_odd, 1.0, dnext), jnp.where(blk_odd, dprev, 1.0)
            else:
                qscale, kscale = jnp.where(blk_odd, dprev, 1.0), jnp.where(blk_odd, 1.0, dnext)
            qpp = (qp3 * qscale).reshape(HG_TILE, HEAD_DIM).astype(BF16)
            kpp = (kp3 * kscale).reshape(HG_TILE, HEAD_DIM).astype(BF16)
            st = st_ref[h]
            inter = [None] * n_sc
            for m in (range(n_sc - 1, -1, -1) if reverse else range(n_sc)):
                rows = slice(2 * m * HG_BLK, 2 * (m + 1) * HG_BLK)
                inter[m] = lax.dot_general(qpp[rows], st.astype(BF16), nt, preferred_element_type=F32)
                upd = lax.dot_general(v[rows], kpp[rows], (((0,), (0,)), ((), ())), preferred_element_type=F32)
                st = st * (dec[2 * m] * dec[2 * m + 1]) + upd
            st_ref[h] = st
            o_ref[tok, cs] = o_acc + jnp.concatenate(inter, axis=0)
        return carry

    lax.fori_loop(0, n_sub, one_tile, 0)


def _hgrn_scan_matrix(reverse):
    t = np.arange(HG_TILE)
    same = (t[:, None] // HG_BLK) == (t[None, :] // HG_BLK)
    incl = (t[None, :] >= t[:, None]) if reverse else (t[None, :] <= t[:, None])
    return jnp.asarray((same & incl).astype(np.float32), BF16)


def _hgrn_direction(proj, lower_bound, segs, reverse):
    n_tok = proj.shape[0]
    step_tok = next(s for s in (4 * HG_TILE, 2 * HG_TILE, HG_TILE) if all(t % s == 0 for _, t in segs))
    n_steps = n_tok // step_tok
    width = C_HEADS * HEAD_DIM
    scan = _hgrn_scan_matrix(reverse)

    def tok_spec(col0):
        return pl.BlockSpec((step_tok, width),
                            lambda i: ((n_steps - 1 - i) if reverse else i, col0 // C_HEADS))

    nblk = HG_TILE // HG_BLK
    return pl.pallas_call(
        functools.partial(_hgrn_kernel, segs=segs, reverse=reverse),
        out_shape=jax.ShapeDtypeStruct((n_tok, width), F32),
        grid=(n_steps,),
        in_specs=[
            tok_spec(COL_C_FB if reverse else COL_C_FF), tok_spec(COL_C_I), tok_spec(COL_C_Q),
            pl.BlockSpec((1, width), lambda i: (0, 0)),
            pl.BlockSpec(scan.shape, lambda i: (0, 0)),
        ],
        out_specs=pl.BlockSpec((step_tok, width), lambda i: ((n_steps - 1 - i) if reverse else i, 0)),
        scratch_shapes=[pltpu.VMEM((C_HEADS, HEAD_DIM, HEAD_DIM), F32)]
        + [pltpu.VMEM((C_HEADS, nblk, HG_BLK, HEAD_DIM), F32)] * 4
        + [pltpu.VMEM((C_HEADS, HG_TILE, HG_TILE), F32)],
        compiler_params=_cparams(("arbitrary",)),
        name="hgrn_bwd" if reverse else "hgrn_fwd",
    )(proj, proj, proj, lower_bound.reshape(1, width), scan)


def _merge_kernel(ga_ref, gb_ref, gc_ref, cg_ref, ya_ref, yb_ref, of_ref, ob_ref, hn_ref,
                  wa_ref, wb_ref, wc_ref, wo_ref, h_ref, g2_ref, o_ref, o2_ref):
    o = of_ref[...] + ob_ref[...]
    heads = []
    for h in range(C_HEADS):
        oh = o[:, h * HEAD_DIM:(h + 1) * HEAD_DIM]
        ms = jnp.mean(oh * oh, axis=-1, keepdims=True)
        heads.append(oh * lax.rsqrt(ms + EPS))
    cg = cg_ref[...].astype(F32)
    yc = (jnp.concatenate(heads, axis=1) * hn_ref[...] * (cg * jax.nn.sigmoid(cg))).astype(BF16)
    m = jax.nn.sigmoid(ga_ref[...].astype(F32)) * jnp.dot(ya_ref[...], wa_ref[...], preferred_element_type=F32)
    m += jax.nn.sigmoid(gb_ref[...].astype(F32)) * jnp.dot(yb_ref[...], wb_ref[...], preferred_element_type=F32)
    m += jax.nn.sigmoid(gc_ref[...].astype(F32)) * jnp.dot(yc, wc_ref[...], preferred_element_type=F32)
    y = h_ref[...] + jnp.dot(m.astype(BF16), wo_ref[...], preferred_element_type=F32)
    o_ref[...] = y
    ms = jnp.mean(y * y, axis=-1, keepdims=True)
    o2_ref[...] = (y * lax.rsqrt(ms + EPS) * g2_ref[...]).astype(o2_ref.dtype)


def _merge(proj, ya, yb, o_f, o_b, hgrn_gain, wa, wb, wc, wo, layer, h, gain2, tm):
    n_tok, d = h.shape
    cw = C_HEADS * HEAD_DIM

    def resident(w):
        return pl.BlockSpec((None,) + w.shape[1:], lambda i: (layer, 0, 0), pipeline_mode=pl.Buffered(1))

    def gate_spec(which):
        return pl.BlockSpec((tm, d), lambda i: (i, which))

    return pl.pallas_call(
        _merge_kernel,
        out_shape=(jax.ShapeDtypeStruct((n_tok, d), F32), jax.ShapeDtypeStruct((n_tok, d), BF16)),
        grid=(n_tok // tm,),
        in_specs=[
            gate_spec(0), gate_spec(1), gate_spec(2),
            pl.BlockSpec((tm, cw), lambda i: (i, COL_C_G // C_HEADS)),
            pl.BlockSpec((tm, ya.shape[1]), lambda i: (i, 0)),
            pl.BlockSpec((tm, cw), lambda i: (i, 0)),
            pl.BlockSpec((tm, cw), lambda i: (i, 0)),
            pl.BlockSpec((tm, cw), lambda i: (i, 0)),
            pl.BlockSpec((1, cw), lambda i: (0, 0)),
            resident(wa), resident(wb), resident(wc), resident(wo),
            pl.BlockSpec((tm, d), lambda i: (i, 0)),
            pl.BlockSpec((1, d), lambda i: (0, 0)),
        ],
        out_specs=(pl.BlockSpec((tm, d), lambda i: (i, 0)), pl.BlockSpec((tm, d), lambda i: (i, 0))),
        compiler_params=_cparams(("parallel",)),
        name="merge_out",
    )(proj, proj, proj, proj, ya, yb, o_f, o_b, hgrn_gain.reshape(1, cw), wa, wb, wc, wo, h,
      gain2.reshape(1, d))


def _tile(n, candidates):
    return next(t for t in candidates if n % t == 0)


def _trunk(xp, xs, segs, p):
    depth = p["w_in"].shape[0]
    n_prompt, n_tok = xp.shape[0], xp.shape[0] + xs.shape[0]
    tm_ffn = _tile(n_tok, (768, 512, 256, 128))
    tm_edge = next(t for t in (768, 512, 256, 128) if n_prompt % t == 0 and n_tok % t == 0)
    tm_proj = _tile(n_tok, (2048, 1024, 512, 256, 128))
    tm_merge = _tile(n_tok, (384, 256, 128))
    d_ff = p["ffn1_w_gate"].shape[-1]
    tf = 512 if d_ff % 512 == 0 else 256
    mix_w = MIX_COLS * HEAD_DIM

    lb_p = jax.nn.softmax(p["hgrn_lb_logits"].astype(F32), axis=1)
    lower_bounds = jnp.cumsum(lb_p, axis=1) - lb_p[:, :1]
    win_tab = _window_bias_table(p["t5_bias"])
    bf = lambda name: p[name].astype(BF16)
    w = {name: bf(name) for name in ("ffn1_w_gate", "ffn1_w_up", "ffn1_w_down", "w_in", "w_branch_a",
                                      "w_branch_b", "w_branch_c", "w_out", "ffn2_w_gate", "ffn2_w_up",
                                      "ffn2_w_down")}
    w_in = jnp.concatenate([w["w_in"][:, :, mix_w:], w["w_in"][:, :, :mix_w]], axis=-1)
    ffn1_w = (w["ffn1_w_gate"], w["ffn1_w_up"], w["ffn1_w_down"] * 0.5)
    ffn2_w = (w["ffn2_w_gate"], w["ffn2_w_up"], w["ffn2_w_down"] * 0.5)
    x = n = None
    for l in range(depth):
        if l == 0:
            h, u = _ffn("first", (xp, xs), ffn1_w, l, p["mix_norm"][l], tm_edge, tf, n_prompt, p["ffn1_norm"][l])
        else:
            h, u = _ffn("mid", (x, n), ffn1_w, l, p["mix_norm"][l], tm_ffn, tf)
        proj = _matmul(u, w_in, l, BF16, tm_proj, 512)
        ya = _window_attention(proj, p["attn_sink"][l].astype(F32) * LOG2E, win_tab, segs)
        yb = _neighbourhood_attention(proj, _na_bias_table(p["na_bias"][l]), segs)
        o_f = _hgrn_direction(proj, lower_bounds[0, l], segs, reverse=False)
        o_b = _hgrn_direction(proj, lower_bounds[1, l], segs, reverse=True)
        h, n = _merge(proj, ya, yb, o_f, o_b, p["hgrn_norm"][l], w["w_branch_a"], w["w_branch_b"],
                      w["w_branch_c"], w["w_out"], l, h, p["ffn2_norm"][l], tm_merge)
        if l == depth - 1:
            return _ffn("last", (h, n), ffn2_w, l, p["final_norm"], tm_edge, tf, n_prompt)
        x, n = _ffn("mid", (h, n), ffn2_w, l, p["ffn1_norm"][l + 1], tm_ffn, tf)


def kernel(x_prompt, x_sample, ffn1_norm, ffn1_w_gate, ffn1_w_up, ffn1_w_down, mix_norm, w_in, attn_sink,
           t5_bias, na_bias, hgrn_lb_logits, hgrn_norm, w_branch_a, w_branch_b, w_branch_c, w_out,
           ffn2_norm, ffn2_w_gate, ffn2_w_up, ffn2_w_down, final_norm):
    params = dict(ffn1_norm=ffn1_norm, ffn1_w_gate=ffn1_w_gate, ffn1_w_up=ffn1_w_up, ffn1_w_down=ffn1_w_down,
                  mix_norm=mix_norm, w_in=w_in, attn_sink=attn_sink, t5_bias=t5_bias, na_bias=na_bias,
                  hgrn_lb_logits=hgrn_lb_logits, hgrn_norm=hgrn_norm, w_branch_a=w_branch_a,
                  w_branch_b=w_branch_b, w_branch_c=w_branch_c, w_out=w_out, ffn2_norm=ffn2_norm,
                  ffn2_w_gate=ffn2_w_gate, ffn2_w_up=ffn2_w_up, ffn2_w_down=ffn2_w_down, final_norm=final_norm)
    d = x_prompt.shape[-1]
    segs = (x_prompt.shape[:2], x_sample.shape[:2])
    for _, t in segs:
        assert t % NA_GROUP == 0 and t // NA_GROUP >= 3 and t // GRID_W >= NA_ROWS and t % WIN_STEP == 0
    y_prompt, y_sample = _trunk(x_prompt.reshape(-1, d), x_sample.reshape(-1, d), segs, params)
    return y_prompt.reshape(x_prompt.shape), y_sample.reshape(x_sample.shape)
```

```python
import functools
import math

import numpy as np
import jax
import jax.numpy as jnp
from jax import lax
from jax.experimental import pallas as pl
from jax.experimental.pallas import tpu as pltpu

F32 = jnp.float32
BF16 = jnp.bfloat16

HEAD_DIM = 128
EPS = 1e-6
A_HEADS = 8
A_KV_HEADS = 2
WINDOW = 128
WIN_BLOCK = 128
T5_BUCKETS = 32
T5_MAX_DIST = 128
B_HEADS = 4
GRID_W = 64
NA_ROWS = 8
NA_COLS = 16
C_HEADS = 4
GATE_COLS = 48
MIX_COLS = 44
COL_A_Q, COL_A_K, COL_A_V = (GATE_COLS + c for c in (0, 8, 10))
COL_B_Q, COL_B_K, COL_B_V = (GATE_COLS + c for c in (12, 16, 20))
COL_C_FF, COL_C_FB, COL_C_I, COL_C_Q, COL_C_G = (GATE_COLS + c for c in (24, 28, 32, 36, 40))

NEG_BIG = -1e30
LOG2E = math.log2(math.e)
V7X_VMEM_LIMIT = 60 * 1024 * 1024
HG_TILE = 128
HG_BLK = 16
NA_GROUP = 4 * GRID_W
WIN_STEP = 2 * WIN_BLOCK


def _cparams(sem):
    return pltpu.CompilerParams(dimension_semantics=sem, vmem_limit_bytes=V7X_VMEM_LIMIT)


def _seq_pos(gb, blk, segs):
    (n_seq0, t0), (_, t1) = segs
    n0, n1 = t0 // blk, t1 // blk
    tot0 = n_seq0 * n0
    in0 = gb < tot0
    local = jnp.where(in0, gb % n0, (gb - tot0) % n1)
    return local, jnp.where(in0, n0, n1)


def _normed(y, gain_ref, dtype):
    ms = jnp.mean(y * y, axis=-1, keepdims=True)
    return (y * lax.rsqrt(ms + EPS) * gain_ref[...]).astype(dtype)


def _ffn_accumulate(n_ref, wg_ref, wu_ref, wd_ref, acc_ref):
    n = n_ref[...]
    g = jnp.dot(n, wg_ref[...], preferred_element_type=F32)
    u = jnp.dot(n, wu_ref[...], preferred_element_type=F32)
    a = (g * jax.nn.sigmoid(g) * u).astype(BF16)
    acc_ref[...] += jnp.dot(a, wd_ref[...], preferred_element_type=F32)


def _ffn_first_kernel(xp_ref, xs_ref, g1_ref, wg_ref, wu_ref, wd_ref, g2_ref, o_ref, o2_ref, n_ref, *, prompt_tiles):
    i, j = pl.program_id(0), pl.program_id(1)

    @pl.when(j == 0)
    def _():
        x = jnp.where(i < prompt_tiles, xp_ref[...], xs_ref[...])
        n_ref[...] = _normed(x, g1_ref, BF16)
        o_ref[...] = x

    _ffn_accumulate(n_ref, wg_ref, wu_ref, wd_ref, o_ref)

    @pl.when(j == pl.num_programs(1) - 1)
    def _():
        o2_ref[...] = _normed(o_ref[...], g2_ref, o2_ref.dtype)


def _ffn_mid_kernel(x_ref, n_ref, wg_ref, wu_ref, wd_ref, g2_ref, o_ref, o2_ref):
    j = pl.program_id(1)

    @pl.when(j == 0)
    def _():
        o_ref[...] = x_ref[...]

    _ffn_accumulate(n_ref, wg_ref, wu_ref, wd_ref, o_ref)

    @pl.when(j == pl.num_programs(1) - 1)
    def _():
        o2_ref[...] = _normed(o_ref[...], g2_ref, o2_ref.dtype)


def _ffn_last_kernel(x_ref, n_ref, wg_ref, wu_ref, wd_ref, g2_ref, yp_ref, ys_ref, acc_ref, *, prompt_tiles):
    i, j = pl.program_id(0), pl.program_id(1)
    done = j == pl.num_programs(1) - 1

    @pl.when(j == 0)
    def _():
        acc_ref[...] = x_ref[...]

    _ffn_accumulate(n_ref, wg_ref, wu_ref, wd_ref, acc_ref)

    @pl.when(done & (i < prompt_tiles))
    def _():
        yp_ref[...] = _normed(acc_ref[...], g2_ref, yp_ref.dtype)

    @pl.when(done & (i >= prompt_tiles))
    def _():
        ys_ref[...] = _normed(acc_ref[...], g2_ref, ys_ref.dtype)


def _ffn(kind, acts, weights, layer, gain2, tm, tf, n_prompt=None, gain1=None):
    wg, wu, wd = weights
    d, d_ff = wg.shape[1:]
    n_tok = sum(a.shape[0] for a in acts) if kind == "first" else acts[0].shape[0]
    tok_spec = pl.BlockSpec((tm, d), lambda i, j: (i, 0))
    vec_spec = pl.BlockSpec((1, d), lambda i, j: (0, 0))
    w_specs = [pl.BlockSpec((None, d, tf), lambda i, j: (layer, 0, j)),
               pl.BlockSpec((None, d, tf), lambda i, j: (layer, 0, j)),
               pl.BlockSpec((None, tf, d), lambda i, j: (layer, j, 0))]
    if kind != "mid":
        assert n_prompt % tm == 0
        pt = n_prompt // tm
        prompt_spec = pl.BlockSpec((tm, d), lambda i, j: (jnp.minimum(i, pt - 1), 0))
        sample_spec = pl.BlockSpec((tm, d), lambda i, j: (jnp.maximum(i - pt, 0), 0))
    both = (jax.ShapeDtypeStruct((n_tok, d), F32), jax.ShapeDtypeStruct((n_tok, d), BF16))
    if kind == "first":
        body = functools.partial(_ffn_first_kernel, prompt_tiles=pt)
        operands = (*acts, gain1.reshape(1, d))
        in_specs = [prompt_spec, sample_spec, vec_spec]
        out_shape, out_specs, scratch = both, (tok_spec, tok_spec), [pltpu.VMEM((tm, d), BF16)]
    elif kind == "mid":
        body, operands, in_specs = _ffn_mid_kernel, acts, [tok_spec, tok_spec]
        out_shape, out_specs, scratch = both, (tok_spec, tok_spec), []
    else:
        body = functools.partial(_ffn_last_kernel, prompt_tiles=pt)
        operands, in_specs = acts, [tok_spec, tok_spec]
        out_shape = (jax.ShapeDtypeStruct((n_prompt, d), F32), jax.ShapeDtypeStruct((n_tok - n_prompt, d), F32))
        out_specs, scratch = (prompt_spec, sample_spec), [pltpu.VMEM((tm, d), F32)]
    return pl.pallas_call(
        body,
        out_shape=out_shape,
        grid=(n_tok // tm, d_ff // tf),
        in_specs=[*in_specs, *w_specs, vec_spec],
        out_specs=out_specs,
        scratch_shapes=scratch,
        compiler_params=_cparams(("arbitrary", "arbitrary")),
        name="ffn_" + kind,
    )(*operands, wg, wu, wd, gain2.reshape(1, d))


def _matmul_kernel(a_ref, w_ref, o_ref):
    o_ref[...] = jnp.dot(a_ref[...], w_ref[...], preferred_element_type=F32).astype(o_ref.dtype)


def _matmul(a, w, layer, out_dtype, tm, tn, rotate):
    n_tok, k = a.shape
    n_out = w.shape[-1]
    n_col = n_out // tn
    return pl.pallas_call(
        _matmul_kernel,
        out_shape=jax.ShapeDtypeStruct((n_tok, n_out), out_dtype),
        grid=(n_tok // tm, n_out // tn),
        in_specs=[pl.BlockSpec((tm, k), lambda i, j: (i, 0)),
                  pl.BlockSpec((None, k, tn), lambda i, j: (layer, 0, (j + rotate) % n_col))],
        out_specs=pl.BlockSpec((tm, tn), lambda i, j: (i, j)),
        compiler_params=_cparams(("parallel", "arbitrary")),
        name="proj_in",
    )(a, w)


def _wattn_phases(sink_ref, q_ref, kp_ref, kc_ref, kn_ref, vp_ref, vc_ref, vn_ref, bias_ref, o_ref, s_ref, e_ref, segs):
    local, n_loc = _seq_pos(pl.program_id(0), WIN_STEP, segs)
    variant = (jnp.where(local == 0, 0, 1), jnp.where(local == n_loc - 1, 2, 1))
    group = A_HEADS // A_KV_HEADS
    gw = group * WIN_BLOCK
    scale = HEAD_DIM ** -0.5 * LOG2E
    units = [(blk, kv) for blk in range(WIN_STEP // WIN_BLOCK) for kv in range(A_KV_HEADS)]

    def keys_of(p_ref, c_ref, n_ref, blk, kv):
        cs = slice(kv * HEAD_DIM, (kv + 1) * HEAD_DIM)
        cat = jnp.concatenate([p_ref[:, cs], c_ref[:, cs], n_ref[:, cs]], axis=0)
        return cat[blk * WIN_BLOCK:(blk + 3) * WIN_BLOCK]

    def scores():
        for u, (blk, kv) in enumerate(units):
            rows = slice(blk * WIN_BLOCK, (blk + 1) * WIN_BLOCK)
            qs = jnp.concatenate([q_ref[rows, h * HEAD_DIM:(h + 1) * HEAD_DIM]
                                  for h in range(kv * group, (kv + 1) * group)], axis=0)
            s = lax.dot_general(qs, keys_of(kp_ref, kc_ref, kn_ref, blk, kv), (((1,), (1,)), ((), ())),
                                preferred_element_type=F32)
            s_ref[u] = s * scale + bias_ref[variant[blk], kv * gw:(kv + 1) * gw, :]

    def outputs():
        for u, (blk, kv) in enumerate(units):
            rows = slice(blk * WIN_BLOCK, (blk + 1) * WIN_BLOCK)
            dens = []
            for i in range(group):
                hr = slice(i * WIN_BLOCK, (i + 1) * WIN_BLOCK)
                sh = s_ref[u, hr, :]
                sink = sink_ref[kv * group + i]
                m = jnp.maximum(jnp.max(sh, axis=-1, keepdims=True), sink)
                e = jnp.exp2(sh - m)
                dens.append(jnp.sum(e, axis=-1, keepdims=True) + jnp.exp2(sink - m))
                e_ref[u, hr, :] = e.astype(BF16)
            o4 = jnp.dot(e_ref[u], keys_of(vp_ref, vc_ref, vn_ref, blk, kv), preferred_element_type=F32)
            for i in range(group):
                h = kv * group + i
                o_ref[rows, h * HEAD_DIM:(h + 1) * HEAD_DIM] = (
                    o4[i * WIN_BLOCK:(i + 1) * WIN_BLOCK] / dens[i]).astype(o_ref.dtype)

    return scores, outputs


def _window_specs(n_tok, bias_tab):
    nb = n_tok // WIN_BLOCK
    per_step = WIN_STEP // WIN_BLOCK
    kv_w = A_KV_HEADS * HEAD_DIM
    n_units = per_step * A_KV_HEADS
    gw = (A_HEADS // A_KV_HEADS) * WIN_BLOCK

    def kv_specs(col0):
        col = col0 // A_KV_HEADS
        return [pl.BlockSpec((WIN_BLOCK, kv_w), lambda s: (jnp.maximum(per_step * s - 1, 0), col)),
                pl.BlockSpec((WIN_STEP, kv_w), lambda s: (s, col)),
                pl.BlockSpec((WIN_BLOCK, kv_w), lambda s: (jnp.minimum(per_step * (s + 1), nb - 1), col))]

    in_specs = [pl.BlockSpec(memory_space=pltpu.SMEM),
                pl.BlockSpec((WIN_STEP, A_HEADS * HEAD_DIM), lambda s: (s, COL_A_Q // A_HEADS)),
                *kv_specs(COL_A_K), *kv_specs(COL_A_V),
                pl.BlockSpec(bias_tab.shape, lambda s: (0, 0, 0))]
    out_spec = pl.BlockSpec((WIN_STEP, A_HEADS * HEAD_DIM), lambda s: (s, 0))
    scratch = [pltpu.VMEM((n_units, gw, 3 * WIN_BLOCK), F32), pltpu.VMEM((n_units, gw, 3 * WIN_BLOCK), BF16)]
    return in_specs, out_spec, scratch


def _t5_bucket(rel):
    nb = T5_BUCKETS // 2
    max_exact = nb // 2
    base = jnp.where(rel > 0, nb, 0)
    n = jnp.abs(rel)
    large = max_exact + (jnp.log(jnp.maximum(n, 1).astype(F32) / max_exact)
                         / math.log(T5_MAX_DIST / max_exact) * (nb - max_exact)).astype(jnp.int32)
    large = jnp.minimum(large, nb - 1)
    return base + jnp.where(n < max_exact, n, large)


def _window_bias_table(t5_bias):
    qi = jnp.arange(WIN_BLOCK)[:, None]
    si = jnp.arange(3 * WIN_BLOCK)[None, :]
    rel = si - WIN_BLOCK - qi
    onehot = (_t5_bucket(rel)[None] == jnp.arange(T5_BUCKETS)[:, None, None]).astype(F32)
    bias = jnp.einsum('bh,bqs->hqs', t5_bias.astype(F32), onehot, precision=lax.Precision.HIGHEST)
    band = jnp.abs(rel) <= WINDOW
    variants = []
    for lo_ok, hi_ok in ((False, True), (True, True), (True, False)):
        ok = band & ((si >= WIN_BLOCK) | lo_ok) & ((si < 2 * WIN_BLOCK) | hi_ok)
        variants.append(jnp.where(ok[None], bias * LOG2E, NEG_BIG).reshape(A_HEADS * WIN_BLOCK, 3 * WIN_BLOCK))
    return jnp.stack(variants)


def _na_window(g, segs):
    lg, ng = _seq_pos(g, NA_GROUP, segs)
    return lg, ng, jnp.clip(lg - 1, 0, ng - 3)


def _na_phases(q_ref, k0_ref, k1_ref, k2_ref, v0_ref, v1_ref, v2_ref, bias_ref, o_ref, s_ref):
    scale = HEAD_DIM ** -0.5 * LOG2E
    heads = [slice(h * HEAD_DIM, (h + 1) * HEAD_DIM) for h in range(B_HEADS)]

    def scores():
        for h, cs in enumerate(heads):
            kcat = jnp.concatenate([k0_ref[:, cs], k1_ref[:, cs], k2_ref[:, cs]], axis=0)
            s = lax.dot_general(q_ref[:, cs], kcat, (((1,), (1,)), ((), ())), preferred_element_type=F32)
            s_ref[h] = s * scale + bias_ref[h]

    def outputs():
        for h, cs in enumerate(heads):
            vcat = jnp.concatenate([v0_ref[:, cs], v1_ref[:, cs], v2_ref[:, cs]], axis=0)
            s = s_ref[h]
            m = jnp.max(s, axis=-1, keepdims=True)
            e = jnp.exp2(s - m)
            den = jnp.sum(e, axis=-1, keepdims=True)
            oh = jnp.dot(e.astype(BF16), vcat, preferred_element_type=F32)
            o_ref[:, cs] = (oh / den).astype(o_ref.dtype)

    return scores, outputs


def _na_specs(segs):
    width = B_HEADS * HEAD_DIM

    def kv_spec(col0, j):
        def index_map(g):
            lg, _, lo = _na_window(g, segs)
            return (g - lg + lo + j, col0 // B_HEADS)
        return pl.BlockSpec((NA_GROUP, width), index_map)

    def bias_map(g):
        lg, ng, _ = _na_window(g, segs)
        return (jnp.where(lg == 0, 0, jnp.where(lg == ng - 1, 2, 1)), 0, 0, 0)

    in_specs = [pl.BlockSpec((NA_GROUP, width), lambda g: (g, COL_B_Q // B_HEADS)),
                kv_spec(COL_B_K, 0), kv_spec(COL_B_K, 1), kv_spec(COL_B_K, 2),
                kv_spec(COL_B_V, 0), kv_spec(COL_B_V, 1), kv_spec(COL_B_V, 2),
                pl.BlockSpec((None, B_HEADS, NA_GROUP, 3 * NA_GROUP), bias_map)]
    out_spec = pl.BlockSpec((NA_GROUP, width), lambda g: (g, 0))
    scratch = [pltpu.VMEM((B_HEADS, NA_GROUP, 3 * NA_GROUP), F32)]
    return in_specs, out_spec, scratch


N_WIN_IN, N_NA_IN = 9, 8


def _attention_kernel(*refs, segs):
    win_in, na_in = refs[:N_WIN_IN], refs[N_WIN_IN:N_WIN_IN + N_NA_IN]
    ya_ref, yb_ref, ws_ref, we_ref, ns_ref = refs[N_WIN_IN + N_NA_IN:]
    win_scores, win_outputs = _wattn_phases(*win_in, ya_ref, ws_ref, we_ref, segs)
    na_scores, na_outputs = _na_phases(*na_in, yb_ref, ns_ref)
    win_scores()
    na_scores()
    win_outputs()
    na_outputs()


def _attention_mixers(proj, sink, win_tab, na_tab, segs):
    assert WIN_STEP == NA_GROUP
    n_tok = proj.shape[0]
    win_specs, win_out, win_scratch = _window_specs(n_tok, win_tab)
    na_specs, na_out, na_scratch = _na_specs(segs)
    return pl.pallas_call(
        functools.partial(_attention_kernel, segs=segs),
        out_shape=(jax.ShapeDtypeStruct((n_tok, A_HEADS * HEAD_DIM), BF16),
                   jax.ShapeDtypeStruct((n_tok, B_HEADS * HEAD_DIM), BF16)),
        grid=(n_tok // WIN_STEP,),
        in_specs=[*win_specs, *na_specs],
        out_specs=(win_out, na_out),
        scratch_shapes=[*win_scratch, *na_scratch],
        compiler_params=_cparams(("parallel",)),
        name="attn_ab",
    )(sink, *([proj] * 7), win_tab, *([proj] * 7), na_tab)


def _na_bias_table(rel_table):
    rows_q = NA_GROUP // GRID_W
    rows_k = 3 * rows_q
    c = np.arange(GRID_W)
    col_start = np.clip(c - NA_COLS // 2, 0, GRID_W - NA_COLS)
    col_ok = (c[None, :] >= col_start[:, None]) & (c[None, :] < col_start[:, None] + NA_COLS)
    dc = np.clip(c[None, :] - c[:, None], -(NA_COLS - 1), NA_COLS - 1) + NA_COLS - 1
    onehot = (dc[None] == np.arange(2 * NA_COLS - 1)[:, None, None]).astype(np.float32)
    by_dr = jnp.einsum('hrd,dqk->hrqk', rel_table.astype(F32), jnp.asarray(onehot),
                       precision=lax.Precision.HIGHEST)
    by_dr = jnp.where(jnp.asarray(col_ok)[None, None], by_dr * LOG2E, NEG_BIG)
    by_dr = jnp.concatenate([by_dr, jnp.full_like(by_dr[:, :1], NEG_BIG)], axis=1)
    cfgs = ([(0, NA_ROWS - 1 - a) for a in range(rows_q)],
            [(a, NA_ROWS // 2 - 1) for a in range(rows_q)],
            [(rows_q, NA_ROWS // 2 - 1 - a) for a in range(rows_q)])
    idx = np.full((3, rows_q, rows_k), 2 * NA_ROWS - 1, np.int32)
    for ci, cfg in enumerate(cfgs):
        for a, (off, dr0) in enumerate(cfg):
            for j in range(NA_ROWS):
                idx[ci, a, off + j] = dr0 + j
    tab = jnp.take(by_dr, jnp.asarray(idx.reshape(-1)), axis=1)
    tab = tab.reshape(B_HEADS, 3, rows_q, rows_k, GRID_W, GRID_W).transpose(1, 0, 2, 4, 3, 5)
    return tab.reshape(3, B_HEADS, NA_GROUP, 3 * NA_GROUP)


def _split3(x):
    hi = x.astype(BF16)
    r1 = x - hi.astype(F32)
    mid = r1.astype(BF16)
    lo = (r1 - mid.astype(F32)).astype(BF16)
    return hi, mid, lo


def _hgrn_kernel(z_ref, v_ref, q_ref, lb_ref, scan_ref, o_ref, st_ref, c3_ref, b3_ref, k3_ref, q3_ref, a_ref,
                 *, segs, reverse):
    i = pl.program_id(0)
    n_steps = pl.num_programs(0)
    n_sub = z_ref.shape[0] // HG_TILE
    step = (n_steps - 1 - i) if reverse else i
    local, n_loc = _seq_pos(step, n_sub * HG_TILE, segs)
    is_start = (local == n_loc - 1) if reverse else (local == 0)

    @pl.when(is_start)
    def _():
        st_ref[...] = jnp.zeros_like(st_ref)

    nblk = HG_TILE // HG_BLK
    half = HG_BLK // 2
    n_sc = nblk // 2
    log2e = math.log2(math.e)
    nt = (((1,), (1,)), ((), ()))

    def sel_index(t0):
        shape = (nblk, half, HG_TILE)
        t = lax.broadcasted_iota(jnp.int32, shape, 1) + t0
        j = lax.broadcasted_iota(jnp.int32, shape, 2) - lax.broadcasted_iota(jnp.int32, shape, 0) * HG_BLK
        ok = (j >= 0) & (j < HG_BLK) & ((j >= t) if reverse else (j <= t))
        return jnp.where(ok, j, -1)

    sel = (sel_index(0), sel_index(half))
    rb = lax.broadcasted_iota(jnp.int32, (HG_TILE, HG_TILE), 0) // HG_BLK
    cb = lax.broadcasted_iota(jnp.int32, (HG_TILE, HG_TILE), 1) // HG_BLK
    cross = ((rb % 2 == 0) & (cb == rb + 1)) if reverse else ((rb % 2 == 1) & (cb == rb - 1))
    blk_odd = (lax.broadcasted_iota(jnp.int32, (nblk, 1, HEAD_DIM), 0) % 2) == 1
    edge_row = 0 if reverse else HG_BLK - 1
    scan = scan_ref[...]

    def one_tile(t, carry):
        tile = (n_sub - 1 - t) if reverse else t
        tok = pl.ds(pl.multiple_of(tile * HG_TILE, HG_TILE), HG_TILE)
        heads = [slice(h * HEAD_DIM, (h + 1) * HEAD_DIM) for h in range(C_HEADS)]

        for h, cs in enumerate(heads):
            lb = lb_ref[:, cs]
            f = lb + (1.0 - lb) * jax.nn.sigmoid(z_ref[tok, cs].astype(F32))
            k3 = (1.0 - f).reshape(nblk, HG_BLK, HEAD_DIM)
            pieces = jnp.concatenate(_split3(jnp.log(f)), axis=1)
            bm = jnp.dot(scan, pieces, preferred_element_type=F32)
            b = bm[:, :HEAD_DIM] + bm[:, HEAD_DIM:2 * HEAD_DIM] + bm[:, 2 * HEAD_DIM:]
            b3 = (b * log2e).reshape(nblk, HG_BLK, HEAD_DIM)
            k3_ref[h] = k3
            b3_ref[h] = b3
            c3_ref[h] = b3 - jnp.log2(jnp.maximum(k3, 0.0))
            q3_ref[h] = (q_ref[tok, cs].astype(F32) * (HEAD_DIM ** -0.5)).reshape(nblk, HG_BLK, HEAD_DIM)

        for h in range(C_HEADS):
            halves = ((q3_ref[h, :, :half, :], b3_ref[h, :, :half, :]), (q3_ref[h, :, half:, :], b3_ref[h, :, half:, :]))
            a_half = [jnp.zeros((nblk, half, HG_TILE), F32), jnp.zeros((nblk, half, HG_TILE), F32)]
            for j in range(HG_BLK):
                cj = c3_ref[h, :, j:j + 1, :]
                for hi in range(2):
                    if (j < half * hi) if reverse else (j > half * hi + half - 1):
                        continue
                    qx, bx = halves[hi]
                    p = qx * jnp.exp2(bx - cj)
                    r = jnp.sum(p, axis=-1, keepdims=True)
                    a_half[hi] = jnp.where(sel[hi] == j, r, a_half[hi])
            a_ref[h] = jnp.concatenate(a_half, axis=1).reshape(HG_TILE, HG_TILE)

        for h, cs in enumerate(heads):
            q3, k3, b3 = q3_ref[h], k3_ref[h], b3_ref[h]
            btot = b3_ref[h, :, edge_row:edge_row + 1, :]
            v = v_ref[tok, cs]
            qp3 = q3 * jnp.exp2(b3)
            kp3 = k3 * jnp.exp2(btot - b3)
            qp = qp3.reshape(HG_TILE, HEAD_DIM).astype(BF16)
            kp = kp3.reshape(HG_TILE, HEAD_DIM).astype(BF16)
            g = lax.dot_general(qp, kp, nt, preferred_element_type=F32)
            a_all = jnp.where(cross, g, a_ref[h]).astype(BF16)
            o_acc = jnp.dot(a_all, v, preferred_element_type=F32)

            dec = jnp.exp2(btot)
            one = jnp.ones_like(dec[:1])
            dprev = jnp.concatenate([one, dec[:-1]], axis=0)
            dnext = jnp.concatenate([dec[1:], one], axis=0)
            if reverse:
                qscale, kscale = jnp.where(blk_odd, 1.0, dnext), jnp.where(blk_odd, dprev, 1.0)
            else:
                qscale, kscale = jnp.where(blk_odd, dprev, 1.0), jnp.where(blk_odd, 1.0, dnext)
            qpp = (qp3 * qscale).reshape(HG_TILE, HEAD_DIM).astype(BF16)
            kpp = (kp3 * kscale).reshape(HG_TILE, HEAD_DIM).astype(BF16)
            st = st_ref[h]
            inter = [None] * n_sc
            for m in (range(n_sc - 1, -1, -1) if reverse else range(n_sc)):
                rows = slice(2 * m * HG_BLK, 2 * (m + 1) * HG_BLK)
                inter[m] = lax.dot_general(qpp[rows], st.astype(BF16), nt, preferred_element_type=F32)
                upd = lax.dot_general(v[rows], kpp[rows], (((0,), (0,)), ((), ())), preferred_element_type=F32)
                st = st * (dec[2 * m] * dec[2 * m + 1]) + upd
            st_ref[h] = st
            o_ref[tok, cs] = o_acc + jnp.concatenate(inter, axis=0)
        return carry

    lax.fori_loop(0, n_sub, one_tile, 0)


def _hgrn_scan_matrix(reverse):
    t = np.arange(HG_TILE)
    same = (t[:, None] // HG_BLK) == (t[None, :] // HG_BLK)
    incl = (t[None, :] >= t[:, None]) if reverse else (t[None, :] <= t[:, None])
    return jnp.asarray((same & incl).astype(np.float32), BF16)


def _hgrn_direction(proj, lower_bound, segs, reverse):
    n_tok = proj.shape[0]
    step_tok = next(s for s in (4 * HG_TILE, 2 * HG_TILE, HG_TILE) if all(t % s == 0 for _, t in segs))
    n_steps = n_tok // step_tok
    width = C_HEADS * HEAD_DIM
    scan = _hgrn_scan_matrix(reverse)

    def tok_spec(col0):
        return pl.BlockSpec((step_tok, width),
                            lambda i: ((n_steps - 1 - i) if reverse else i, col0 // C_HEADS))

    nblk = HG_TILE // HG_BLK
    return pl.pallas_call(
        functools.partial(_hgrn_kernel, segs=segs, reverse=reverse),
        out_shape=jax.ShapeDtypeStruct((n_tok, width), F32),
        grid=(n_steps,),
        in_specs=[
            tok_spec(COL_C_FB if reverse else COL_C_FF), tok_spec(COL_C_I), tok_spec(COL_C_Q),
            pl.BlockSpec((1, width), lambda i: (0, 0)),
            pl.BlockSpec(scan.shape, lambda i: (0, 0)),
        ],
        out_specs=pl.BlockSpec((step_tok, width), lambda i: ((n_steps - 1 - i) if reverse else i, 0)),
        scratch_shapes=[pltpu.VMEM((C_HEADS, HEAD_DIM, HEAD_DIM), F32)]
        + [pltpu.VMEM((C_HEADS, nblk, HG_BLK, HEAD_DIM), F32)] * 4
        + [pltpu.VMEM((C_HEADS, HG_TILE, HG_TILE), F32)],
        compiler_params=_cparams(("arbitrary",)),
        name="hgrn_bwd" if reverse else "hgrn_fwd",
    )(proj, proj, proj, lower_bound.reshape(1, width), scan)


def _merge_kernel(ga_ref, gb_ref, gc_ref, cg_ref, ya_ref, yb_ref, of_ref, ob_ref, hn_ref,
                  wa_ref, wb_ref, wc_ref, wo_ref, h_ref, g2_ref, o_ref, o2_ref):
    o = of_ref[...] + ob_ref[...]
    heads = []
    for h in range(C_HEADS):
        oh = o[:, h * HEAD_DIM:(h + 1) * HEAD_DIM]
        ms = jnp.mean(oh * oh, axis=-1, keepdims=True)
        heads.append(oh * lax.rsqrt(ms + EPS))
    cg = cg_ref[...].astype(F32)
    yc = (jnp.concatenate(heads, axis=1) * hn_ref[...] * (cg * jax.nn.sigmoid(cg))).astype(BF16)
    m = jax.nn.sigmoid(ga_ref[...].astype(F32)) * jnp.dot(ya_ref[...], wa_ref[...], preferred_element_type=F32)
    m += jax.nn.sigmoid(gb_ref[...].astype(F32)) * jnp.dot(yb_ref[...], wb_ref[...], preferred_element_type=F32)
    m += jax.nn.sigmoid(gc_ref[...].astype(F32)) * jnp.dot(yc, wc_ref[...], preferred_element_type=F32)
    y = h_ref[...] + jnp.dot(m.astype(BF16), wo_ref[...], preferred_element_type=F32)
    o_ref[...] = y
    ms = jnp.mean(y * y, axis=-1, keepdims=True)
    o2_ref[...] = (y * lax.rsqrt(ms + EPS) * g2_ref[...]).astype(o2_ref.dtype)


def _merge(proj, ya, yb, o_f, o_b, hgrn_gain, wa, wb, wc, wo, layer, h, gain2, tm):
    n_tok, d = h.shape
    cw = C_HEADS * HEAD_DIM

    def resident(w):
        return pl.BlockSpec((None,) + w.shape[1:], lambda i: (layer, 0, 0), pipeline_mode=pl.Buffered(1))

    def gate_spec(which):
        return pl.BlockSpec((tm, d), lambda i: (i, which))

    return pl.pallas_call(
        _merge_kernel,
        out_shape=(jax.ShapeDtypeStruct((n_tok, d), F32), jax.ShapeDtypeStruct((n_tok, d), BF16)),
        grid=(n_tok // tm,),
        in_specs=[
            gate_spec(0), gate_spec(1), gate_spec(2),
            pl.BlockSpec((tm, cw), lambda i: (i, COL_C_G // C_HEADS)),
            pl.BlockSpec((tm, ya.shape[1]), lambda i: (i, 0)),
            pl.BlockSpec((tm, cw), lambda i: (i, 0)),
            pl.BlockSpec((tm, cw), lambda i: (i, 0)),
            pl.BlockSpec((tm, cw), lambda i: (i, 0)),
            pl.BlockSpec((1, cw), lambda i: (0, 0)),
            resident(wa), resident(wb), resident(wc), resident(wo),
            pl.BlockSpec((tm, d), lambda i: (i, 0)),
            pl.BlockSpec((1, d), lambda i: (0, 0)),
        ],
        out_specs=(pl.BlockSpec((tm, d), lambda i: (i, 0)), pl.BlockSpec((tm, d), lambda i: (i, 0))),
        compiler_params=_cparams(("parallel",)),
        name="merge_out",
    )(proj, proj, proj, proj, ya, yb, o_f, o_b, hgrn_gain.reshape(1, cw), wa, wb, wc, wo, h,
      gain2.reshape(1, d))


def _tile(n, candidates):
    return next(t for t in candidates if n % t == 0)


def _trunk(xp, xs, segs, p):
    depth = p["w_in"].shape[0]
    n_prompt, n_tok = xp.shape[0], xp.shape[0] + xs.shape[0]
    tm_ffn = _tile(n_tok, (768, 512, 256, 128))
    tm_edge = next(t for t in (768, 512, 256, 128) if n_prompt % t == 0 and n_tok % t == 0)
    tm_proj = _tile(n_tok, (2048, 1024, 512, 256, 128))
    tm_merge = _tile(n_tok, (384, 256, 128))
    d_ff = p["ffn1_w_gate"].shape[-1]
    tf = 512 if d_ff % 512 == 0 else 256
    mix_w = MIX_COLS * HEAD_DIM

    lb_p = jax.nn.softmax(p["hgrn_lb_logits"].astype(F32), axis=1)
    lower_bounds = jnp.cumsum(lb_p, axis=1) - lb_p[:, :1]
    win_tab = _window_bias_table(p["t5_bias"])
    bf = lambda name: p[name].astype(BF16)
    w = {name: bf(name) for name in ("ffn1_w_gate", "ffn1_w_up", "ffn1_w_down", "w_in", "w_branch_a",
                                      "w_branch_b", "w_branch_c", "w_out", "ffn2_w_gate", "ffn2_w_up",
                                      "ffn2_w_down")}
    ffn1_w = (w["ffn1_w_gate"], w["ffn1_w_up"], w["ffn1_w_down"] * 0.5)
    ffn2_w = (w["ffn2_w_gate"], w["ffn2_w_up"], w["ffn2_w_down"] * 0.5)
    x = n = None
    for l in range(depth):
        if l == 0:
            h, u = _ffn("first", (xp, xs), ffn1_w, l, p["mix_norm"][l], tm_edge, tf, n_prompt, p["ffn1_norm"][l])
        else:
            h, u = _ffn("mid", (x, n), ffn1_w, l, p["mix_norm"][l], tm_ffn, tf)
        proj = _matmul(u, w["w_in"], l, BF16, tm_proj, 512, mix_w // 512)
        ya, yb = _attention_mixers(proj, p["attn_sink"][l].astype(F32) * LOG2E, win_tab,
                                   _na_bias_table(p["na_bias"][l]), segs)
        o_f = _hgrn_direction(proj, lower_bounds[0, l], segs, reverse=False)
        o_b = _hgrn_direction(proj, lower_bounds[1, l], segs, reverse=True)
        h, n = _merge(proj, ya, yb, o_f, o_b, p["hgrn_norm"][l], w["w_branch_a"], w["w_branch_b"],
                      w["w_branch_c"], w["w_out"], l, h, p["ffn2_norm"][l], tm_merge)
        if l == depth - 1:
            return _ffn("last", (h, n), ffn2_w, l, p["final_norm"], tm_edge, tf, n_prompt)
        x, n = _ffn("mid", (h, n), ffn2_w, l, p["ffn1_norm"][l + 1], tm_ffn, tf)


def kernel(x_prompt, x_sample, ffn1_norm, ffn1_w_gate, ffn1_w_up, ffn1_w_down, mix_norm, w_in, attn_sink,
           t5_bias, na_bias, hgrn_lb_logits, hgrn_norm, w_branch_a, w_branch_b, w_branch_c, w_out,
           ffn2_norm, ffn2_w_gate, ffn2_w_up, ffn2_w_down, final_norm):
    params = dict(ffn1_norm=ffn1_norm, ffn1_w_gate=ffn1_w_gate, ffn1_w_up=ffn1_w_up, ffn1_w_down=ffn1_w_down,
                  mix_norm=mix_norm, w_in=w_in, attn_sink=attn_sink, t5_bias=t5_bias, na_bias=na_bias,
                  hgrn_lb_logits=hgrn_lb_logits, hgrn_norm=hgrn_norm, w_branch_a=w_branch_a,
                  w_branch_b=w_branch_b, w_branch_c=w_branch_c, w_out=w_out, ffn2_norm=ffn2_norm,
                  ffn2_w_gate=ffn2_w_gate, ffn2_w_up=ffn2_w_up, ffn2_w_down=ffn2_w_down, final_norm=final_norm)
    d = x_prompt.shape[-1]
    segs = (x_prompt.shape[:2], x_sample.shape[:2])
    for _, t in segs:
        assert t % NA_GROUP == 0 and t // NA_GROUP >= 3 and t // GRID_W >= NA_ROWS and t % WIN_STEP == 0
    y_prompt, y_sample = _trunk(x_prompt.reshape(-1, d), x_sample.reshape(-1, d), segs, params)
    return y_prompt.reshape(x_prompt.shape), y_sample.reshape(x_sample.shape)
```

```python
import functools
import math

import numpy as np
import jax
import jax.numpy as jnp
from jax import lax
from jax.experimental import pallas as pl
from jax.experimental.pallas import tpu as pltpu

F32 = jnp.float32
BF16 = jnp.bfloat16

HEAD_DIM = 128
EPS = 1e-6
A_HEADS = 8
A_KV_HEADS = 2
WINDOW = 128
WIN_BLOCK = 128
T5_BUCKETS = 32
T5_MAX_DIST = 128
B_HEADS = 4
GRID_W = 64
NA_ROWS = 8
NA_COLS = 16
C_HEADS = 4
GATE_COLS = 48
MIX_COLS = 44
COL_A_Q, COL_A_K, COL_A_V = (GATE_COLS + c for c in (0, 8, 10))
COL_B_Q, COL_B_K, COL_B_V = (GATE_COLS + c for c in (12, 16, 20))
COL_C_FF, COL_C_FB, COL_C_I, COL_C_Q, COL_C_G = (GATE_COLS + c for c in (24, 28, 32, 36, 40))

NEG_BIG = -1e30
LOG2E = math.log2(math.e)
V7X_VMEM_LIMIT = 60 * 1024 * 1024
HG_TILE = 128
HG_BLK = 16
NA_GROUP = 4 * GRID_W
WIN_STEP = 2 * WIN_BLOCK


def _cparams(sem):
    return pltpu.CompilerParams(dimension_semantics=sem, vmem_limit_bytes=V7X_VMEM_LIMIT)


def _seq_pos(gb, blk, segs):
    (n_seq0, t0), (_, t1) = segs
    n0, n1 = t0 // blk, t1 // blk
    tot0 = n_seq0 * n0
    in0 = gb < tot0
    local = jnp.where(in0, gb % n0, (gb - tot0) % n1)
    return local, jnp.where(in0, n0, n1)


def _normed(y, gain_ref, dtype):
    ms = jnp.mean(y * y, axis=-1, keepdims=True)
    return (y * lax.rsqrt(ms + EPS) * gain_ref[...]).astype(dtype)


def _ffn_accumulate(n_ref, wg_ref, wu_ref, wd_ref, acc_ref):
    n = n_ref[...]
    g = jnp.dot(n, wg_ref[...], preferred_element_type=F32)
    u = jnp.dot(n, wu_ref[...], preferred_element_type=F32)
    a = (g * jax.nn.sigmoid(g) * u).astype(BF16)
    acc_ref[...] += jnp.dot(a, wd_ref[...], preferred_element_type=F32)


def _ffn_first_kernel(xp_ref, xs_ref, g1_ref, wg_ref, wu_ref, wd_ref, g2_ref, o_ref, o2_ref, n_ref, *, prompt_tiles):
    i, j = pl.program_id(0), pl.program_id(1)

    @pl.when(j == 0)
    def _():
        x = jnp.where(i < prompt_tiles, xp_ref[...], xs_ref[...])
        n_ref[...] = _normed(x, g1_ref, BF16)
        o_ref[...] = x

    _ffn_accumulate(n_ref, wg_ref, wu_ref, wd_ref, o_ref)

    @pl.when(j == pl.num_programs(1) - 1)
    def _():
        o2_ref[...] = _normed(o_ref[...], g2_ref, o2_ref.dtype)


def _ffn_mid_kernel(x_ref, n_ref, wg_ref, wu_ref, wd_ref, g2_ref, o_ref, o2_ref):
    j = pl.program_id(1)

    @pl.when(j == 0)
    def _():
        o_ref[...] = x_ref[...]

    _ffn_accumulate(n_ref, wg_ref, wu_ref, wd_ref, o_ref)

    @pl.when(j == pl.num_programs(1) - 1)
    def _():
        o2_ref[...] = _normed(o_ref[...], g2_ref, o2_ref.dtype)


def _ffn_last_kernel(x_ref, n_ref, wg_ref, wu_ref, wd_ref, g2_ref, yp_ref, ys_ref, acc_ref, *, prompt_tiles):
    i, j = pl.program_id(0), pl.program_id(1)
    done = j == pl.num_programs(1) - 1

    @pl.when(j == 0)
    def _():
        acc_ref[...] = x_ref[...]

    _ffn_accumulate(n_ref, wg_ref, wu_ref, wd_ref, acc_ref)

    @pl.when(done & (i < prompt_tiles))
    def _():
        yp_ref[...] = _normed(acc_ref[...], g2_ref, yp_ref.dtype)

    @pl.when(done & (i >= prompt_tiles))
    def _():
        ys_ref[...] = _normed(acc_ref[...], g2_ref, ys_ref.dtype)


def _ffn(kind, acts, weights, layer, gain2, tm, tf, n_prompt=None, gain1=None):
    wg, wu, wd = weights
    d, d_ff = wg.shape[1:]
    n_tok = sum(a.shape[0] for a in acts) if kind == "first" else acts[0].shape[0]
    tok_spec = pl.BlockSpec((tm, d), lambda i, j: (i, 0))
    vec_spec = pl.BlockSpec((1, d), lambda i, j: (0, 0))
    w_specs = [pl.BlockSpec((None, d, tf), lambda i, j: (layer, 0, j)),
               pl.BlockSpec((None, d, tf), lambda i, j: (layer, 0, j)),
               pl.BlockSpec((None, tf, d), lambda i, j: (layer, j, 0))]
    if kind != "mid":
        assert n_prompt % tm == 0
        pt = n_prompt // tm
        prompt_spec = pl.BlockSpec((tm, d), lambda i, j: (jnp.minimum(i, pt - 1), 0))
        sample_spec = pl.BlockSpec((tm, d), lambda i, j: (jnp.maximum(i - pt, 0), 0))
    both = (jax.ShapeDtypeStruct((n_tok, d), F32), jax.ShapeDtypeStruct((n_tok, d), BF16))
    if kind == "first":
        body = functools.partial(_ffn_first_kernel, prompt_tiles=pt)
        operands = (*acts, gain1.reshape(1, d))
        in_specs = [prompt_spec, sample_spec, vec_spec]
        out_shape, out_specs, scratch = both, (tok_spec, tok_spec), [pltpu.VMEM((tm, d), BF16)]
    elif kind == "mid":
        body, operands, in_specs = _ffn_mid_kernel, acts, [tok_spec, tok_spec]
        out_shape, out_specs, scratch = both, (tok_spec, tok_spec), []
    else:
        body = functools.partial(_ffn_last_kernel, prompt_tiles=pt)
        operands, in_specs = acts, [tok_spec, tok_spec]
        out_shape = (jax.ShapeDtypeStruct((n_prompt, d), F32), jax.ShapeDtypeStruct((n_tok - n_prompt, d), F32))
        out_specs, scratch = (prompt_spec, sample_spec), [pltpu.VMEM((tm, d), F32)]
    return pl.pallas_call(
        body,
        out_shape=out_shape,
        grid=(n_tok // tm, d_ff // tf),
        in_specs=[*in_specs, *w_specs, vec_spec],
        out_specs=out_specs,
        scratch_shapes=scratch,
        compiler_params=_cparams(("arbitrary", "arbitrary")),
        name="ffn_" + kind,
    )(*operands, wg, wu, wd, gain2.reshape(1, d))


def _matmul_kernel(a_ref, w_ref, o_ref):
    o_ref[...] = jnp.dot(a_ref[...], w_ref[...], preferred_element_type=F32).astype(o_ref.dtype)


def _matmul(a, w, layer, out_dtype, tm, tn, rotate):
    n_tok, k = a.shape
    n_out = w.shape[-1]
    n_col = n_out // tn
    return pl.pallas_call(
        _matmul_kernel,
        out_shape=jax.ShapeDtypeStruct((n_tok, n_out), out_dtype),
        grid=(n_tok // tm, n_out // tn),
        in_specs=[pl.BlockSpec((tm, k), lambda i, j: (i, 0)),
                  pl.BlockSpec((None, k, tn), lambda i, j: (layer, 0, (j + rotate) % n_col))],
        out_specs=pl.BlockSpec((tm, tn), lambda i, j: (i, j)),
        compiler_params=_cparams(("parallel", "arbitrary")),
        name="proj_in",
    )(a, w)


def _wattn_phases(sink_ref, q_ref, kp_ref, kc_ref, kn_ref, vp_ref, vc_ref, vn_ref, bias_ref, o_ref, s_ref, e_ref, segs):
    local, n_loc = _seq_pos(pl.program_id(0), WIN_STEP, segs)
    variant = (jnp.where(local == 0, 0, 1), jnp.where(local == n_loc - 1, 2, 1))
    group = A_HEADS // A_KV_HEADS
    gw = group * WIN_BLOCK
    scale = HEAD_DIM ** -0.5 * LOG2E
    units = [(blk, kv) for blk in range(WIN_STEP // WIN_BLOCK) for kv in range(A_KV_HEADS)]

    def keys_of(p_ref, c_ref, n_ref, blk, kv):
        cs = slice(kv * HEAD_DIM, (kv + 1) * HEAD_DIM)
        cat = jnp.concatenate([p_ref[:, cs], c_ref[:, cs], n_ref[:, cs]], axis=0)
        return cat[blk * WIN_BLOCK:(blk + 3) * WIN_BLOCK]

    def scores():
        for u, (blk, kv) in enumerate(units):
            rows = slice(blk * WIN_BLOCK, (blk + 1) * WIN_BLOCK)
            qs = jnp.concatenate([q_ref[rows, h * HEAD_DIM:(h + 1) * HEAD_DIM]
                                  for h in range(kv * group, (kv + 1) * group)], axis=0)
            s = lax.dot_general(qs, keys_of(kp_ref, kc_ref, kn_ref, blk, kv), (((1,), (1,)), ((), ())),
                                preferred_element_type=F32)
            s_ref[u] = s * scale + bias_ref[variant[blk], kv * gw:(kv + 1) * gw, :]

    def outputs():
        for u, (blk, kv) in enumerate(units):
            rows = slice(blk * WIN_BLOCK, (blk + 1) * WIN_BLOCK)
            dens = []
            for i in range(group):
                hr = slice(i * WIN_BLOCK, (i + 1) * WIN_BLOCK)
                sh = s_ref[u, hr, :]
                sink = sink_ref[kv * group + i]
                m = jnp.maximum(jnp.max(sh, axis=-1, keepdims=True), sink)
                e = jnp.exp2(sh - m)
                dens.append(jnp.sum(e, axis=-1, keepdims=True) + jnp.exp2(sink - m))
                e_ref[u, hr, :] = e.astype(BF16)
            o4 = jnp.dot(e_ref[u], keys_of(vp_ref, vc_ref, vn_ref, blk, kv), preferred_element_type=F32)
            for i in range(group):
                h = kv * group + i
                o_ref[rows, h * HEAD_DIM:(h + 1) * HEAD_DIM] = (
                    o4[i * WIN_BLOCK:(i + 1) * WIN_BLOCK] / dens[i]).astype(o_ref.dtype)

    return scores, outputs


def _window_specs(n_tok, bias_tab):
    nb = n_tok // WIN_BLOCK
    per_step = WIN_STEP // WIN_BLOCK
    kv_w = A_KV_HEADS * HEAD_DIM
    n_units = per_step * A_KV_HEADS
    gw = (A_HEADS // A_KV_HEADS) * WIN_BLOCK

    def kv_specs(col0):
        col = col0 // A_KV_HEADS
        return [pl.BlockSpec((WIN_BLOCK, kv_w), lambda s: (jnp.maximum(per_step * s - 1, 0), col)),
                pl.BlockSpec((WIN_STEP, kv_w), lambda s: (s, col)),
                pl.BlockSpec((WIN_BLOCK, kv_w), lambda s: (jnp.minimum(per_step * (s + 1), nb - 1), col))]

    in_specs = [pl.BlockSpec(memory_space=pltpu.SMEM),
                pl.BlockSpec((WIN_STEP, A_HEADS * HEAD_DIM), lambda s: (s, COL_A_Q // A_HEADS)),
                *kv_specs(COL_A_K), *kv_specs(COL_A_V),
                pl.BlockSpec(bias_tab.shape, lambda s: (0, 0, 0))]
    out_spec = pl.BlockSpec((WIN_STEP, A_HEADS * HEAD_DIM), lambda s: (s, 0))
    scratch = [pltpu.VMEM((n_units, gw, 3 * WIN_BLOCK), F32), pltpu.VMEM((n_units, gw, 3 * WIN_BLOCK), BF16)]
    return in_specs, out_spec, scratch


def _t5_bucket(rel):
    nb = T5_BUCKETS // 2
    max_exact = nb // 2
    base = jnp.where(rel > 0, nb, 0)
    n = jnp.abs(rel)
    large = max_exact + (jnp.log(jnp.maximum(n, 1).astype(F32) / max_exact)
                         / math.log(T5_MAX_DIST / max_exact) * (nb - max_exact)).astype(jnp.int32)
    large = jnp.minimum(large, nb - 1)
    return base + jnp.where(n < max_exact, n, large)


def _window_bias_table(t5_bias):
    qi = jnp.arange(WIN_BLOCK)[:, None]
    si = jnp.arange(3 * WIN_BLOCK)[None, :]
    rel = si - WIN_BLOCK - qi
    onehot = (_t5_bucket(rel)[None] == jnp.arange(T5_BUCKETS)[:, None, None]).astype(F32)
    bias = jnp.einsum('bh,bqs->hqs', t5_bias.astype(F32), onehot, precision=lax.Precision.HIGHEST)
    band = jnp.abs(rel) <= WINDOW
    variants = []
    for lo_ok, hi_ok in ((False, True), (True, True), (True, False)):
        ok = band & ((si >= WIN_BLOCK) | lo_ok) & ((si < 2 * WIN_BLOCK) | hi_ok)
        variants.append(jnp.where(ok[None], bias * LOG2E, NEG_BIG).reshape(A_HEADS * WIN_BLOCK, 3 * WIN_BLOCK))
    return jnp.stack(variants)


def _na_window(g, segs):
    lg, ng = _seq_pos(g, NA_GROUP, segs)
    return lg, ng, jnp.clip(lg - 1, 0, ng - 3)


def _na_phases(q_ref, k0_ref, k1_ref, k2_ref, v0_ref, v1_ref, v2_ref, bias_ref, o_ref, s_ref):
    scale = HEAD_DIM ** -0.5 * LOG2E
    heads = [slice(h * HEAD_DIM, (h + 1) * HEAD_DIM) for h in range(B_HEADS)]

    def scores():
        for h, cs in enumerate(heads):
            kcat = jnp.concatenate([k0_ref[:, cs], k1_ref[:, cs], k2_ref[:, cs]], axis=0)
            s = lax.dot_general(q_ref[:, cs], kcat, (((1,), (1,)), ((), ())), preferred_element_type=F32)
            s_ref[h] = s * scale + bias_ref[h]

    def outputs():
        for h, cs in enumerate(heads):
            vcat = jnp.concatenate([v0_ref[:, cs], v1_ref[:, cs], v2_ref[:, cs]], axis=0)
            s = s_ref[h]
            m = jnp.max(s, axis=-1, keepdims=True)
            e = jnp.exp2(s - m)
            den = jnp.sum(e, axis=-1, keepdims=True)
            oh = jnp.dot(e.astype(BF16), vcat, preferred_element_type=F32)
            o_ref[:, cs] = (oh / den).astype(o_ref.dtype)

    return scores, outputs


def _na_specs(segs):
    width = B_HEADS * HEAD_DIM

    def kv_spec(col0, j):
        def index_map(g):
            lg, _, lo = _na_window(g, segs)
            return (g - lg + lo + j, col0 // B_HEADS)
        return pl.BlockSpec((NA_GROUP, width), index_map)

    def bias_map(g):
        lg, ng, _ = _na_window(g, segs)
        return (jnp.where(lg == 0, 0, jnp.where(lg == ng - 1, 2, 1)), 0, 0, 0)

    in_specs = [pl.BlockSpec((NA_GROUP, width), lambda g: (g, COL_B_Q // B_HEADS)),
                kv_spec(COL_B_K, 0), kv_spec(COL_B_K, 1), kv_spec(COL_B_K, 2),
                kv_spec(COL_B_V, 0), kv_spec(COL_B_V, 1), kv_spec(COL_B_V, 2),
                pl.BlockSpec((None, B_HEADS, NA_GROUP, 3 * NA_GROUP), bias_map)]
    out_spec = pl.BlockSpec((NA_GROUP, width), lambda g: (g, 0))
    scratch = [pltpu.VMEM((B_HEADS, NA_GROUP, 3 * NA_GROUP), F32)]
    return in_specs, out_spec, scratch


N_WIN_IN, N_NA_IN = 9, 8


def _attention_kernel(*refs, segs):
    win_in, na_in = refs[:N_WIN_IN], refs[N_WIN_IN:N_WIN_IN + N_NA_IN]
    ya_ref, yb_ref, ws_ref, we_ref, ns_ref = refs[N_WIN_IN + N_NA_IN:]
    win_scores, win_outputs = _wattn_phases(*win_in, ya_ref, ws_ref, we_ref, segs)
    na_scores, na_outputs = _na_phases(*na_in, yb_ref, ns_ref)
    win_scores()
    na_scores()
    win_outputs()
    na_outputs()


def _attention_mixers(proj, sink, win_tab, na_tab, segs):
    assert WIN_STEP == NA_GROUP
    n_tok = proj.shape[0]
    win_specs, win_out, win_scratch = _window_specs(n_tok, win_tab)
    na_specs, na_out, na_scratch = _na_specs(segs)
    return pl.pallas_call(
        functools.partial(_attention_kernel, segs=segs),
        out_shape=(jax.ShapeDtypeStruct((n_tok, A_HEADS * HEAD_DIM), BF16),
                   jax.ShapeDtypeStruct((n_tok, B_HEADS * HEAD_DIM), BF16)),
        grid=(n_tok // WIN_STEP,),
        in_specs=[*win_specs, *na_specs],
        out_specs=(win_out, na_out),
        scratch_shapes=[*win_scratch, *na_scratch],
        compiler_params=_cparams(("parallel",)),
        name="attn_ab",
    )(sink, *([proj] * 7), win_tab, *([proj] * 7), na_tab)


def _na_bias_table(rel_table):
    rows_q = NA_GROUP // GRID_W
    rows_k = 3 * rows_q
    c = np.arange(GRID_W)
    col_start = np.clip(c - NA_COLS // 2, 0, GRID_W - NA_COLS)
    col_ok = (c[None, :] >= col_start[:, None]) & (c[None, :] < col_start[:, None] + NA_COLS)
    dc = np.clip(c[None, :] - c[:, None], -(NA_COLS - 1), NA_COLS - 1) + NA_COLS - 1
    onehot = (dc[None] == np.arange(2 * NA_COLS - 1)[:, None, None]).astype(np.float32)
    by_dr = jnp.einsum('hrd,dqk->hrqk', rel_table.astype(F32), jnp.asarray(onehot),
                       precision=lax.Precision.HIGHEST)
    by_dr = jnp.where(jnp.asarray(col_ok)[None, None], by_dr * LOG2E, NEG_BIG)
    by_dr = jnp.concatenate([by_dr, jnp.full_like(by_dr[:, :1], NEG_BIG)], axis=1)
    cfgs = ([(0, NA_ROWS - 1 - a) for a in range(rows_q)],
            [(a, NA_ROWS // 2 - 1) for a in range(rows_q)],
            [(rows_q, NA_ROWS // 2 - 1 - a) for a in range(rows_q)])
    idx = np.full((3, rows_q, rows_k), 2 * NA_ROWS - 1, np.int32)
    for ci, cfg in enumerate(cfgs):
        for a, (off, dr0) in enumerate(cfg):
            for j in range(NA_ROWS):
                idx[ci, a, off + j] = dr0 + j
    tab = jnp.take(by_dr, jnp.asarray(idx.reshape(-1)), axis=1)
    tab = tab.reshape(B_HEADS, 3, rows_q, rows_k, GRID_W, GRID_W).transpose(1, 0, 2, 4, 3, 5)
    return tab.reshape(3, B_HEADS, NA_GROUP, 3 * NA_GROUP)


def _split3(x):
    hi = x.astype(BF16)
    r1 = x - hi.astype(F32)
    mid = r1.astype(BF16)
    lo = (r1 - mid.astype(F32)).astype(BF16)
    return hi, mid, lo


def _hgrn_kernel(z_ref, v_ref, q_ref, lb_ref, scan_ref, o_ref, st_ref, c3_ref, b3_ref, k3_ref, q3_ref, a_ref,
                 *, segs, reverse):
    i = pl.program_id(0)
    n_steps = pl.num_programs(0)
    n_sub = z_ref.shape[0] // HG_TILE
    step = (n_steps - 1 - i) if reverse else i
    local, n_loc = _seq_pos(step, n_sub * HG_TILE, segs)
    is_start = (local == n_loc - 1) if reverse else (local == 0)

    @pl.when(is_start)
    def _():
        st_ref[...] = jnp.zeros_like(st_ref)

    nblk = HG_TILE // HG_BLK
    half = HG_BLK // 2
    n_sc = nblk // 2
    log2e = math.log2(math.e)
    nt = (((1,), (1,)), ((), ()))

    def sel_index(t0):
        shape = (nblk, half, HG_TILE)
        t = lax.broadcasted_iota(jnp.int32, shape, 1) + t0
        j = lax.broadcasted_iota(jnp.int32, shape, 2) - lax.broadcasted_iota(jnp.int32, shape, 0) * HG_BLK
        ok = (j >= 0) & (j < HG_BLK) & ((j >= t) if reverse else (j <= t))
        return jnp.where(ok, j, -1)

    sel = (sel_index(0), sel_index(half))
    rb = lax.broadcasted_iota(jnp.int32, (HG_TILE, HG_TILE), 0) // HG_BLK
    cb = lax.broadcasted_iota(jnp.int32, (HG_TILE, HG_TILE), 1) // HG_BLK
    cross = ((rb % 2 == 0) & (cb == rb + 1)) if reverse else ((rb % 2 == 1) & (cb == rb - 1))
    blk_odd = (lax.broadcasted_iota(jnp.int32, (nblk, 1, HEAD_DIM), 0) % 2) == 1
    edge_row = 0 if reverse else HG_BLK - 1
    scan = scan_ref[...]

    def one_tile(t, carry):
        tile = (n_sub - 1 - t) if reverse else t
        tok = pl.ds(pl.multiple_of(tile * HG_TILE, HG_TILE), HG_TILE)
        heads = [slice(h * HEAD_DIM, (h + 1) * HEAD_DIM) for h in range(C_HEADS)]

        for h, cs in enumerate(heads):
            lb = lb_ref[:, cs]
            f = lb + (1.0 - lb) * jax.nn.sigmoid(z_ref[tok, cs].astype(F32))
            k3 = (1.0 - f).reshape(nblk, HG_BLK, HEAD_DIM)
            pieces = jnp.concatenate(_split3(jnp.log(f)), axis=1)
            bm = jnp.dot(scan, pieces, preferred_element_type=F32)
            b = bm[:, :HEAD_DIM] + bm[:, HEAD_DIM:2 * HEAD_DIM] + bm[:, 2 * HEAD_DIM:]
            b3 = (b * log2e).reshape(nblk, HG_BLK, HEAD_DIM)
            k3_ref[h] = k3
            b3_ref[h] = b3
            c3_ref[h] = b3 - jnp.log2(jnp.maximum(k3, 0.0))
            q3_ref[h] = (q_ref[tok, cs].astype(F32) * (HEAD_DIM ** -0.5)).reshape(nblk, HG_BLK, HEAD_DIM)

        for h in range(C_HEADS):
            halves = ((q3_ref[h, :, :half, :], b3_ref[h, :, :half, :]), (q3_ref[h, :, half:, :], b3_ref[h, :, half:, :]))
            a_half = [jnp.zeros((nblk, half, HG_TILE), F32), jnp.zeros((nblk, half, HG_TILE), F32)]
            for j in range(HG_BLK):
                cj = c3_ref[h, :, j:j + 1, :]
                for hi in range(2):
                    if (j < half * hi) if reverse else (j > half * hi + half - 1):
                        continue
                    qx, bx = halves[hi]
                    p = qx * jnp.exp2(bx - cj)
                    r = jnp.sum(p, axis=-1, keepdims=True)
                    a_half[hi] = jnp.where(sel[hi] == j, r, a_half[hi])
            a_ref[h] = jnp.concatenate(a_half, axis=1).reshape(HG_TILE, HG_TILE)

        for h, cs in enumerate(heads):
            q3, k3, b3 = q3_ref[h], k3_ref[h], b3_ref[h]
            btot = b3_ref[h, :, edge_row:edge_row + 1, :]
            v = v_ref[tok, cs]
            qp3 = q3 * jnp.exp2(b3)
            kp3 = k3 * jnp.exp2(btot - b3)
            qp = qp3.reshape(HG_TILE, HEAD_DIM).astype(BF16)
            kp = kp3.reshape(HG_TILE, HEAD_DIM).astype(BF16)
            g = lax.dot_general(qp, kp, nt, preferred_element_type=F32)
            a_all = jnp.where(cross, g, a_ref[h]).astype(BF16)
            o_acc = jnp.dot(a_all, v, preferred_element_type=F32)

            dec = jnp.exp2(btot)
            one = jnp.ones_like(dec[:1])
            dprev = jnp.concatenate([one, dec[:-1]], axis=0)
            dnext = jnp.concatenate([dec[1:], one], axis=0)
            if reverse:
                qscale, kscale = jnp.where(blk_odd, 1.0, dnext), jnp.where(blk_odd, dprev, 1.0)
            else:
                qscale, kscale = jnp.where(blk_odd, dprev, 1.0), jnp.where(blk_odd, 1.0, dnext)
            qpp = (qp3 * qscale).reshape(HG_TILE, HEAD_DIM).astype(BF16)
            kpp = (kp3 * kscale).reshape(HG_TILE, HEAD_DIM).astype(BF16)
            st = st_ref[h]
            inter = [None] * n_sc
            for m in (range(n_sc - 1, -1, -1) if reverse else range(n_sc)):
                rows = slice(2 * m * HG_BLK, 2 * (m + 1) * HG_BLK)
                inter[m] = lax.dot_general(qpp[rows], st.astype(BF16), nt, preferred_element_type=F32)
                upd = lax.dot_general(v[rows], kpp[rows], (((0,), (0,)), ((), ())), preferred_element_type=F32)
                st = st * (dec[2 * m] * dec[2 * m + 1]) + upd
            st_ref[h] = st
            o_ref[tok, cs] = o_acc + jnp.concatenate(inter, axis=0)
        return carry

    lax.fori_loop(0, n_sub, one_tile, 0)


def _hgrn_scan_matrix(reverse):
    t = np.arange(HG_TILE)
    same = (t[:, None] // HG_BLK) == (t[None, :] // HG_BLK)
    incl = (t[None, :] >= t[:, None]) if reverse else (t[None, :] <= t[:, None])
    return jnp.asarray((same & incl).astype(np.float32), BF16)


def _hgrn_direction(proj, lower_bound, segs, reverse):
    n_tok = proj.shape[0]
    step_tok = next(s for s in (8 * HG_TILE, 4 * HG_TILE, 2 * HG_TILE, HG_TILE) if all(t % s == 0 for _, t in segs))
    n_steps = n_tok // step_tok
    width = C_HEADS * HEAD_DIM
    scan = _hgrn_scan_matrix(reverse)

    def tok_spec(col0):
        return pl.BlockSpec((step_tok, width),
                            lambda i: ((n_steps - 1 - i) if reverse else i, col0 // C_HEADS))

    nblk = HG_TILE // HG_BLK
    return pl.pallas_call(
        functools.partial(_hgrn_kernel, segs=segs, reverse=reverse),
        out_shape=jax.ShapeDtypeStruct((n_tok, width), F32),
        grid=(n_steps,),
        in_specs=[
            tok_spec(COL_C_FB if reverse else COL_C_FF), tok_spec(COL_C_I), tok_spec(COL_C_Q),
            pl.BlockSpec((1, width), lambda i: (0, 0)),
            pl.BlockSpec(scan.shape, lambda i: (0, 0)),
        ],
        out_specs=pl.BlockSpec((step_tok, width), lambda i: ((n_steps - 1 - i) if reverse else i, 0)),
        scratch_shapes=[pltpu.VMEM((C_HEADS, HEAD_DIM, HEAD_DIM), F32)]
        + [pltpu.VMEM((C_HEADS, nblk, HG_BLK, HEAD_DIM), F32)] * 4
        + [pltpu.VMEM((C_HEADS, HG_TILE, HG_TILE), F32)],
        compiler_params=_cparams(("arbitrary",)),
        name="hgrn_bwd" if reverse else "hgrn_fwd",
    )(proj, proj, proj, lower_bound.reshape(1, width), scan)


def _merge_kernel(ga_ref, gb_ref, gc_ref, cg_ref, ya_ref, yb_ref, of_ref, ob_ref, hn_ref,
                  wa_ref, wb_ref, wc_ref, wo_ref, h_ref, g2_ref, o_ref, o2_ref):
    o = of_ref[...] + ob_ref[...]
    heads = []
    for h in range(C_HEADS):
        oh = o[:, h * HEAD_DIM:(h + 1) * HEAD_DIM]
        ms = jnp.mean(oh * oh, axis=-1, keepdims=True)
        heads.append(oh * lax.rsqrt(ms + EPS))
    cg = cg_ref[...].astype(F32)
    yc = (jnp.concatenate(heads, axis=1) * hn_ref[...] * (cg * jax.nn.sigmoid(cg))).astype(BF16)
    m = jax.nn.sigmoid(ga_ref[...].astype(F32)) * jnp.dot(ya_ref[...], wa_ref[...], preferred_element_type=F32)
    m += jax.nn.sigmoid(gb_ref[...].astype(F32)) * jnp.dot(yb_ref[...], wb_ref[...], preferred_element_type=F32)
    m += jax.nn.sigmoid(gc_ref[...].astype(F32)) * jnp.dot(yc, wc_ref[...], preferred_element_type=F32)
    y = h_ref[...] + jnp.dot(m.astype(BF16), wo_ref[...], preferred_element_type=F32)
    o_ref[...] = y
    ms = jnp.mean(y * y, axis=-1, keepdims=True)
    o2_ref[...] = (y * lax.rsqrt(ms + EPS) * g2_ref[...]).astype(o2_ref.dtype)


def _merge(proj, ya, yb, o_f, o_b, hgrn_gain, wa, wb, wc, wo, layer, h, gain2, tm):
    n_tok, d = h.shape
    cw = C_HEADS * HEAD_DIM

    def resident(w):
        return pl.BlockSpec((None,) + w.shape[1:], lambda i: (layer, 0, 0), pipeline_mode=pl.Buffered(1))

    def gate_spec(which):
        return pl.BlockSpec((tm, d), lambda i: (i, which))

    return pl.pallas_call(
        _merge_kernel,
        out_shape=(jax.ShapeDtypeStruct((n_tok, d), F32), jax.ShapeDtypeStruct((n_tok, d), BF16)),
        grid=(n_tok // tm,),
        in_specs=[
            gate_spec(0), gate_spec(1), gate_spec(2),
            pl.BlockSpec((tm, cw), lambda i: (i, COL_C_G // C_HEADS)),
            pl.BlockSpec((tm, ya.shape[1]), lambda i: (i, 0)),
            pl.BlockSpec((tm, cw), lambda i: (i, 0)),
            pl.BlockSpec((tm, cw), lambda i: (i, 0)),
            pl.BlockSpec((tm, cw), lambda i: (i, 0)),
            pl.BlockSpec((1, cw), lambda i: (0, 0)),
            resident(wa), resident(wb), resident(wc), resident(wo),
            pl.BlockSpec((tm, d), lambda i: (i, 0)),
            pl.BlockSpec((1, d), lambda i: (0, 0)),
        ],
        out_specs=(pl.BlockSpec((tm, d), lambda i: (i, 0)), pl.BlockSpec((tm, d), lambda i: (i, 0))),
        compiler_params=_cparams(("parallel",)),
        name="merge_out",
    )(proj, proj, proj, proj, ya, yb, o_f, o_b, hgrn_gain.reshape(1, cw), wa, wb, wc, wo, h,
      gain2.reshape(1, d))


def _tile(n, candidates):
    return next(t for t in candidates if n % t == 0)


def _trunk(xp, xs, segs, p):
    depth = p["w_in"].shape[0]
    n_prompt, n_tok = xp.shape[0], xp.shape[0] + xs.shape[0]
    tm_ffn = _tile(n_tok, (768, 512, 256, 128))
    tm_edge = next(t for t in (768, 512, 256, 128) if n_prompt % t == 0 and n_tok % t == 0)
    tm_proj = _tile(n_tok, (4096, 2048, 1024, 512, 256, 128))
    tm_merge = _tile(n_tok, (384, 256, 128))
    d_ff = p["ffn1_w_gate"].shape[-1]
    tf = 512 if d_ff % 512 == 0 else 256
    mix_w = MIX_COLS * HEAD_DIM

    lb_p = jax.nn.softmax(p["hgrn_lb_logits"].astype(F32), axis=1)
    lower_bounds = jnp.cumsum(lb_p, axis=1) - lb_p[:, :1]
    win_tab = _window_bias_table(p["t5_bias"])
    bf = lambda name: p[name].astype(BF16)
    w = {name: bf(name) for name in ("ffn1_w_gate", "ffn1_w_up", "ffn1_w_down", "w_in", "w_branch_a",
                                      "w_branch_b", "w_branch_c", "w_out", "ffn2_w_gate", "ffn2_w_up",
                                      "ffn2_w_down")}
    ffn1_w = (w["ffn1_w_gate"], w["ffn1_w_up"], w["ffn1_w_down"] * 0.5)
    ffn2_w = (w["ffn2_w_gate"], w["ffn2_w_up"], w["ffn2_w_down"] * 0.5)
    x = n = None
    for l in range(depth):
        if l == 0:
            h, u = _ffn("first", (xp, xs), ffn1_w, l, p["mix_norm"][l], tm_edge, tf, n_prompt, p["ffn1_norm"][l])
        else:
            h, u = _ffn("mid", (x, n), ffn1_w, l, p["mix_norm"][l], tm_ffn, tf)
        proj = _matmul(u, w["w_in"], l, BF16, tm_proj, 512, mix_w // 512)
        ya, yb = _attention_mixers(proj, p["attn_sink"][l].astype(F32) * LOG2E, win_tab,
                                   _na_bias_table(p["na_bias"][l]), segs)
        o_f = _hgrn_direction(proj, lower_bounds[0, l], segs, reverse=False)
        o_b = _hgrn_direction(proj, lower_bounds[1, l], segs, reverse=True)
        h, n = _merge(proj, ya, yb, o_f, o_b, p["hgrn_norm"][l], w["w_branch_a"], w["w_branch_b"],
                      w["w_branch_c"], w["w_out"], l, h, p["ffn2_norm"][l], tm_merge)
        if l == depth - 1:
            return _ffn("last", (h, n), ffn2_w, l, p["final_norm"], tm_edge, tf, n_prompt)
        x, n = _ffn("mid", (h, n), ffn2_w, l, p["ffn1_norm"][l + 1], tm_ffn, tf)


def kernel(x_prompt, x_sample, ffn1_norm, ffn1_w_gate, ffn1_w_up, ffn1_w_down, mix_norm, w_in, attn_sink,
           t5_bias, na_bias, hgrn_lb_logits, hgrn_norm, w_branch_a, w_branch_b, w_branch_c, w_out,
           ffn2_norm, ffn2_w_gate, ffn2_w_up, ffn2_w_down, final_norm):
    params = dict(ffn1_norm=ffn1_norm, ffn1_w_gate=ffn1_w_gate, ffn1_w_up=ffn1_w_up, ffn1_w_down=ffn1_w_down,
                  mix_norm=mix_norm, w_in=w_in, attn_sink=attn_sink, t5_bias=t5_bias, na_bias=na_bias,
                  hgrn_lb_logits=hgrn_lb_logits, hgrn_norm=hgrn_norm, w_branch_a=w_branch_a,
                  w_branch_b=w_branch_b, w_branch_c=w_branch_c, w_out=w_out, ffn2_norm=ffn2_norm,
                  ffn2_w_gate=ffn2_w_gate, ffn2_w_up=ffn2_w_up, ffn2_w_down=ffn2_w_down, final_norm=final_norm)
    d = x_prompt.shape[-1]
    segs = (x_prompt.shape[:2], x_sample.shape[:2])
    for _, t in segs:
        assert t % NA_GROUP == 0 and t // NA_GROUP >= 3 and t // GRID_W >= NA_ROWS and t % WIN_STEP == 0
    y_prompt, y_sample = _trunk(x_prompt.reshape(-1, d), x_sample.reshape(-1, d), segs, params)
    return y_prompt.reshape(x_prompt.shape), y_sample.reshape(x_sample.shape)
```

```python
import functools
import math

import numpy as np
import jax
import jax.numpy as jnp
from jax import lax
from jax.experimental import pallas as pl
from jax.experimental.pallas import tpu as pltpu

F32 = jnp.float32
BF16 = jnp.bfloat16

HEAD_DIM = 128
EPS = 1e-6
A_HEADS = 8
A_KV_HEADS = 2
WINDOW = 128
WIN_BLOCK = 128
T5_BUCKETS = 32
T5_MAX_DIST = 128
B_HEADS = 4
GRID_W = 64
NA_ROWS = 8
NA_COLS = 16
C_HEADS = 4
GATE_COLS = 48
MIX_COLS = 44
COL_A_Q, COL_A_K, COL_A_V = (GATE_COLS + c for c in (0, 8, 10))
COL_B_Q, COL_B_K, COL_B_V = (GATE_COLS + c for c in (12, 16, 20))
COL_C_FF, COL_C_FB, COL_C_I, COL_C_Q, COL_C_G = (GATE_COLS + c for c in (24, 28, 32, 36, 40))

NEG_BIG = -1e30
LOG2E = math.log2(math.e)
V7X_VMEM_LIMIT = 60 * 1024 * 1024
HG_TILE = 128
HG_BLK = 16
NA_GROUP = 4 * GRID_W
WIN_STEP = 2 * WIN_BLOCK


def _cparams(sem):
    return pltpu.CompilerParams(dimension_semantics=sem, vmem_limit_bytes=V7X_VMEM_LIMIT)


def _seq_pos(gb, blk, segs):
    (n_seq0, t0), (_, t1) = segs
    n0, n1 = t0 // blk, t1 // blk
    tot0 = n_seq0 * n0
    in0 = gb < tot0
    local = jnp.where(in0, gb % n0, (gb - tot0) % n1)
    return local, jnp.where(in0, n0, n1)


def _normed(y, gain_ref, dtype):
    ms = jnp.mean(y * y, axis=-1, keepdims=True)
    return (y * lax.rsqrt(ms + EPS) * gain_ref[...]).astype(dtype)


def _ffn_stream(n_ref, w_hbm, w_buf, sem, acc_ref, *, layer):
    i, n_tiles = pl.program_id(0), pl.num_programs(0)
    tf = w_buf[0].shape[-1]
    n_ff = w_hbm[0].shape[-1] // tf
    first_slot = (i * n_ff) % 2

    def copies(j, slot):
        cols = pl.ds(pl.multiple_of(j * tf, tf), tf)
        return (pltpu.make_async_copy(w_hbm[0].at[layer, :, cols], w_buf[0].at[slot], sem.at[0, slot]),
                pltpu.make_async_copy(w_hbm[1].at[layer, :, cols], w_buf[1].at[slot], sem.at[1, slot]),
                pltpu.make_async_copy(w_hbm[2].at[layer, cols, :], w_buf[2].at[slot], sem.at[2, slot]))

    @pl.when(i == 0)
    def _():
        for c in copies(0, first_slot):
            c.start()

    def block(j, carry):
        slot = (first_slot + j) % 2
        for c in copies(j, slot):
            c.wait()

        @pl.when(j + 1 < n_ff)
        def _():
            for c in copies(j + 1, 1 - slot):
                c.start()

        @pl.when((j + 1 == n_ff) & (i + 1 < n_tiles))
        def _():
            for c in copies(0, 1 - slot):
                c.start()

        n = n_ref[...]
        g = jnp.dot(n, w_buf[0][slot], preferred_element_type=F32)
        u = jnp.dot(n, w_buf[1][slot], preferred_element_type=F32)
        a = (g * jax.nn.sigmoid(g) * u).astype(BF16)
        acc_ref[...] += jnp.dot(a, w_buf[2][slot], preferred_element_type=F32)
        return carry

    lax.fori_loop(0, n_ff, block, 0)


def _ffn_first_kernel(xp_ref, xs_ref, g1_ref, g2_ref, wg_hbm, wu_hbm, wd_hbm, o_ref, o2_ref, n_ref,
                      wg_buf, wu_buf, wd_buf, sem, *, prompt_tiles, layer):
    x = jnp.where(pl.program_id(0) < prompt_tiles, xp_ref[...], xs_ref[...])
    n_ref[...] = _normed(x, g1_ref, BF16)
    o_ref[...] = x
    _ffn_stream(n_ref, (wg_hbm, wu_hbm, wd_hbm), (wg_buf, wu_buf, wd_buf), sem, o_ref, layer=layer)
    o2_ref[...] = _normed(o_ref[...], g2_ref, o2_ref.dtype)


def _ffn_mid_kernel(x_ref, n_ref, g2_ref, wg_hbm, wu_hbm, wd_hbm, o_ref, o2_ref, wg_buf, wu_buf, wd_buf, sem, *, layer):
    o_ref[...] = x_ref[...]
    _ffn_stream(n_ref, (wg_hbm, wu_hbm, wd_hbm), (wg_buf, wu_buf, wd_buf), sem, o_ref, layer=layer)
    o2_ref[...] = _normed(o_ref[...], g2_ref, o2_ref.dtype)


def _ffn_last_kernel(x_ref, n_ref, g2_ref, wg_hbm, wu_hbm, wd_hbm, yp_ref, ys_ref, acc_ref,
                     wg_buf, wu_buf, wd_buf, sem, *, prompt_tiles, layer):
    i = pl.program_id(0)
    acc_ref[...] = x_ref[...]
    _ffn_stream(n_ref, (wg_hbm, wu_hbm, wd_hbm), (wg_buf, wu_buf, wd_buf), sem, acc_ref, layer=layer)

    @pl.when(i < prompt_tiles)
    def _():
        yp_ref[...] = _normed(acc_ref[...], g2_ref, yp_ref.dtype)

    @pl.when(i >= prompt_tiles)
    def _():
        ys_ref[...] = _normed(acc_ref[...], g2_ref, ys_ref.dtype)


def _ffn(kind, acts, weights, layer, gain2, tm, tf, n_prompt=None, gain1=None):
    wg, wu, wd = weights
    d, d_ff = wg.shape[1:]
    n_tok = sum(a.shape[0] for a in acts) if kind == "first" else acts[0].shape[0]
    tok_spec = pl.BlockSpec((tm, d), lambda i: (i, 0))
    vec_spec = pl.BlockSpec((1, d), lambda i: (0, 0))
    if kind != "mid":
        assert n_prompt % tm == 0
        pt = n_prompt // tm
        prompt_spec = pl.BlockSpec((tm, d), lambda i: (jnp.minimum(i, pt - 1), 0))
        sample_spec = pl.BlockSpec((tm, d), lambda i: (jnp.maximum(i - pt, 0), 0))
    both = (jax.ShapeDtypeStruct((n_tok, d), F32), jax.ShapeDtypeStruct((n_tok, d), BF16))
    if kind == "first":
        body = functools.partial(_ffn_first_kernel, prompt_tiles=pt, layer=layer)
        operands = (*acts, gain1.reshape(1, d))
        in_specs = [prompt_spec, sample_spec, vec_spec]
        out_shape, out_specs, scratch = both, (tok_spec, tok_spec), [pltpu.VMEM((tm, d), BF16)]
    elif kind == "mid":
        body, operands, in_specs = functools.partial(_ffn_mid_kernel, layer=layer), acts, [tok_spec, tok_spec]
        out_shape, out_specs, scratch = both, (tok_spec, tok_spec), []
    else:
        body = functools.partial(_ffn_last_kernel, prompt_tiles=pt, layer=layer)
        operands, in_specs = acts, [tok_spec, tok_spec]
        out_shape = (jax.ShapeDtypeStruct((n_prompt, d), F32), jax.ShapeDtypeStruct((n_tok - n_prompt, d), F32))
        out_specs, scratch = (prompt_spec, sample_spec), [pltpu.VMEM((tm, d), F32)]
    hbm = pl.BlockSpec(memory_space=pl.ANY)
    w_scratch = [pltpu.VMEM((2, d, tf), BF16), pltpu.VMEM((2, d, tf), BF16), pltpu.VMEM((2, tf, d), BF16),
                 pltpu.SemaphoreType.DMA((3, 2))]
    return pl.pallas_call(
        body,
        out_shape=out_shape,
        grid=(n_tok // tm,),
        in_specs=[*in_specs, vec_spec, hbm, hbm, hbm],
        out_specs=out_specs,
        scratch_shapes=[*scratch, *w_scratch],
        compiler_params=_cparams(("arbitrary",)),
        name="ffn_" + kind,
    )(*operands, gain2.reshape(1, d), wg, wu, wd)


def _matmul_kernel(a_ref, w_ref, o_ref):
    o_ref[...] = jnp.dot(a_ref[...], w_ref[...], preferred_element_type=F32).astype(o_ref.dtype)


def _matmul(a, w, layer, out_dtype, tm, tn, rotate):
    n_tok, k = a.shape
    n_out = w.shape[-1]
    n_col = n_out // tn
    return pl.pallas_call(
        _matmul_kernel,
        out_shape=jax.ShapeDtypeStruct((n_tok, n_out), out_dtype),
        grid=(n_tok // tm, n_out // tn),
        in_specs=[pl.BlockSpec((tm, k), lambda i, j: (i, 0)),
                  pl.BlockSpec((None, k, tn), lambda i, j: (layer, 0, (j + rotate) % n_col))],
        out_specs=pl.BlockSpec((tm, tn), lambda i, j: (i, j)),
        compiler_params=_cparams(("parallel", "arbitrary")),
        name="proj_in",
    )(a, w)


def _wattn_phases(sink_ref, q_ref, kp_ref, kc_ref, kn_ref, vp_ref, vc_ref, vn_ref, bias_ref, o_ref, s_ref, e_ref, segs):
    local, n_loc = _seq_pos(pl.program_id(0), WIN_STEP, segs)
    variant = (jnp.where(local == 0, 0, 1), jnp.where(local == n_loc - 1, 2, 1))
    group = A_HEADS // A_KV_HEADS
    gw = group * WIN_BLOCK
    scale = HEAD_DIM ** -0.5 * LOG2E
    units = [(blk, kv) for blk in range(WIN_STEP // WIN_BLOCK) for kv in range(A_KV_HEADS)]

    def keys_of(p_ref, c_ref, n_ref, blk, kv):
        cs = slice(kv * HEAD_DIM, (kv + 1) * HEAD_DIM)
        cat = jnp.concatenate([p_ref[:, cs], c_ref[:, cs], n_ref[:, cs]], axis=0)
        return cat[blk * WIN_BLOCK:(blk + 3) * WIN_BLOCK]

    def scores():
        for u, (blk, kv) in enumerate(units):
            rows = slice(blk * WIN_BLOCK, (blk + 1) * WIN_BLOCK)
            qs = jnp.concatenate([q_ref[rows, h * HEAD_DIM:(h + 1) * HEAD_DIM]
                                  for h in range(kv * group, (kv + 1) * group)], axis=0)
            s = lax.dot_general(qs, keys_of(kp_ref, kc_ref, kn_ref, blk, kv), (((1,), (1,)), ((), ())),
                                preferred_element_type=F32)
            s_ref[u] = s * scale + bias_ref[variant[blk], kv * gw:(kv + 1) * gw, :]

    def outputs():
        for u, (blk, kv) in enumerate(units):
            rows = slice(blk * WIN_BLOCK, (blk + 1) * WIN_BLOCK)
            dens = []
            for i in range(group):
                hr = slice(i * WIN_BLOCK, (i + 1) * WIN_BLOCK)
                sh = s_ref[u, hr, :]
                sink = sink_ref[kv * group + i]
                m = jnp.maximum(jnp.max(sh, axis=-1, keepdims=True), sink)
                e = jnp.exp2(sh - m)
                dens.append(jnp.sum(e, axis=-1, keepdims=True) + jnp.exp2(sink - m))
                e_ref[u, hr, :] = e.astype(BF16)
            o4 = jnp.dot(e_ref[u], keys_of(vp_ref, vc_ref, vn_ref, blk, kv), preferred_element_type=F32)
            for i in range(group):
                h = kv * group + i
                o_ref[rows, h * HEAD_DIM:(h + 1) * HEAD_DIM] = (
                    o4[i * WIN_BLOCK:(i + 1) * WIN_BLOCK] / dens[i]).astype(o_ref.dtype)

    return scores, outputs


def _window_specs(n_tok, bias_tab):
    nb = n_tok // WIN_BLOCK
    per_step = WIN_STEP // WIN_BLOCK
    kv_w = A_KV_HEADS * HEAD_DIM
    n_units = per_step * A_KV_HEADS
    gw = (A_HEADS // A_KV_HEADS) * WIN_BLOCK

    def kv_specs(col0):
        col = col0 // A_KV_HEADS
        return [pl.BlockSpec((WIN_BLOCK, kv_w), lambda s: (jnp.maximum(per_step * s - 1, 0), col)),
                pl.BlockSpec((WIN_STEP, kv_w), lambda s: (s, col)),
                pl.BlockSpec((WIN_BLOCK, kv_w), lambda s: (jnp.minimum(per_step * (s + 1), nb - 1), col))]

    in_specs = [pl.BlockSpec(memory_space=pltpu.SMEM),
                pl.BlockSpec((WIN_STEP, A_HEADS * HEAD_DIM), lambda s: (s, COL_A_Q // A_HEADS)),
                *kv_specs(COL_A_K), *kv_specs(COL_A_V),
                pl.BlockSpec(bias_tab.shape, lambda s: (0, 0, 0))]
    out_spec = pl.BlockSpec((WIN_STEP, A_HEADS * HEAD_DIM), lambda s: (s, 0))
    scratch = [pltpu.VMEM((n_units, gw, 3 * WIN_BLOCK), F32), pltpu.VMEM((n_units, gw, 3 * WIN_BLOCK), BF16)]
    return in_specs, out_spec, scratch


def _t5_bucket(rel):
    nb = T5_BUCKETS // 2
    max_exact = nb // 2
    base = jnp.where(rel > 0, nb, 0)
    n = jnp.abs(rel)
    large = max_exact + (jnp.log(jnp.maximum(n, 1).astype(F32) / max_exact)
                         / math.log(T5_MAX_DIST / max_exact) * (nb - max_exact)).astype(jnp.int32)
    large = jnp.minimum(large, nb - 1)
    return base + jnp.where(n < max_exact, n, large)


def _window_bias_table(t5_bias):
    qi = jnp.arange(WIN_BLOCK)[:, None]
    si = jnp.arange(3 * WIN_BLOCK)[None, :]
    rel = si - WIN_BLOCK - qi
    onehot = (_t5_bucket(rel)[None] == jnp.arange(T5_BUCKETS)[:, None, None]).astype(F32)
    bias = jnp.einsum('bh,bqs->hqs', t5_bias.astype(F32), onehot, precision=lax.Precision.HIGHEST)
    band = jnp.abs(rel) <= WINDOW
    variants = []
    for lo_ok, hi_ok in ((False, True), (True, True), (True, False)):
        ok = band & ((si >= WIN_BLOCK) | lo_ok) & ((si < 2 * WIN_BLOCK) | hi_ok)
        variants.append(jnp.where(ok[None], bias * LOG2E, NEG_BIG).reshape(A_HEADS * WIN_BLOCK, 3 * WIN_BLOCK))
    return jnp.stack(variants)


def _na_window(g, segs):
    lg, ng = _seq_pos(g, NA_GROUP, segs)
    return lg, ng, jnp.clip(lg - 1, 0, ng - 3)


def _na_phases(q_ref, k0_ref, k1_ref, k2_ref, v0_ref, v1_ref, v2_ref, bias_ref, o_ref, s_ref):
    scale = HEAD_DIM ** -0.5 * LOG2E
    heads = [slice(h * HEAD_DIM, (h + 1) * HEAD_DIM) for h in range(B_HEADS)]

    def scores():
        for h, cs in enumerate(heads):
            kcat = jnp.concatenate([k0_ref[:, cs], k1_ref[:, cs], k2_ref[:, cs]], axis=0)
            s = lax.dot_general(q_ref[:, cs], kcat, (((1,), (1,)), ((), ())), preferred_element_type=F32)
            s_ref[h] = s * scale + bias_ref[h]

    def outputs():
        for h, cs in enumerate(heads):
            vcat = jnp.concatenate([v0_ref[:, cs], v1_ref[:, cs], v2_ref[:, cs]], axis=0)
            s = s_ref[h]
            m = jnp.max(s, axis=-1, keepdims=True)
            e = jnp.exp2(s - m)
            den = jnp.sum(e, axis=-1, keepdims=True)
            oh = jnp.dot(e.astype(BF16), vcat, preferred_element_type=F32)
            o_ref[:, cs] = (oh / den).astype(o_ref.dtype)

    return scores, outputs


def _na_specs(segs):
    width = B_HEADS * HEAD_DIM

    def kv_spec(col0, j):
        def index_map(g):
            lg, _, lo = _na_window(g, segs)
            return (g - lg + lo + j, col0 // B_HEADS)
        return pl.BlockSpec((NA_GROUP, width), index_map)

    def bias_map(g):
        lg, ng, _ = _na_window(g, segs)
        return (jnp.where(lg == 0, 0, jnp.where(lg == ng - 1, 2, 1)), 0, 0, 0)

    in_specs = [pl.BlockSpec((NA_GROUP, width), lambda g: (g, COL_B_Q // B_HEADS)),
                kv_spec(COL_B_K, 0), kv_spec(COL_B_K, 1), kv_spec(COL_B_K, 2),
                kv_spec(COL_B_V, 0), kv_spec(COL_B_V, 1), kv_spec(COL_B_V, 2),
                pl.BlockSpec((None, B_HEADS, NA_GROUP, 3 * NA_GROUP), bias_map)]
    out_spec = pl.BlockSpec((NA_GROUP, width), lambda g: (g, 0))
    scratch = [pltpu.VMEM((B_HEADS, NA_GROUP, 3 * NA_GROUP), F32)]
    return in_specs, out_spec, scratch


N_WIN_IN, N_NA_IN = 9, 8


def _attention_kernel(*refs, segs):
    win_in, na_in = refs[:N_WIN_IN], refs[N_WIN_IN:N_WIN_IN + N_NA_IN]
    ya_ref, yb_ref, ws_ref, we_ref, ns_ref = refs[N_WIN_IN + N_NA_IN:]
    win_scores, win_outputs = _wattn_phases(*win_in, ya_ref, ws_ref, we_ref, segs)
    na_scores, na_outputs = _na_phases(*na_in, yb_ref, ns_ref)
    win_scores()
    na_scores()
    win_outputs()
    na_outputs()


def _attention_mixers(proj, sink, win_tab, na_tab, segs):
    assert WIN_STEP == NA_GROUP
    n_tok = proj.shape[0]
    win_specs, win_out, win_scratch = _window_specs(n_tok, win_tab)
    na_specs, na_out, na_scratch = _na_specs(segs)
    return pl.pallas_call(
        functools.partial(_attention_kernel, segs=segs),
        out_shape=(jax.ShapeDtypeStruct((n_tok, A_HEADS * HEAD_DIM), BF16),
                   jax.ShapeDtypeStruct((n_tok, B_HEADS * HEAD_DIM), BF16)),
        grid=(n_tok // WIN_STEP,),
        in_specs=[*win_specs, *na_specs],
        out_specs=(win_out, na_out),
        scratch_shapes=[*win_scratch, *na_scratch],
        compiler_params=_cparams(("parallel",)),
        name="attn_ab",
    )(sink, *([proj] * 7), win_tab, *([proj] * 7), na_tab)


def _na_bias_table(rel_table):
    rows_q = NA_GROUP // GRID_W
    rows_k = 3 * rows_q
    c = np.arange(GRID_W)
    col_start = np.clip(c - NA_COLS // 2, 0, GRID_W - NA_COLS)
    col_ok = (c[None, :] >= col_start[:, None]) & (c[None, :] < col_start[:, None] + NA_COLS)
    dc = np.clip(c[None, :] - c[:, None], -(NA_COLS - 1), NA_COLS - 1) + NA_COLS - 1
    onehot = (dc[None] == np.arange(2 * NA_COLS - 1)[:, None, None]).astype(np.float32)
    by_dr = jnp.einsum('hrd,dqk->hrqk', rel_table.astype(F32), jnp.asarray(onehot),
                       precision=lax.Precision.HIGHEST)
    by_dr = jnp.where(jnp.asarray(col_ok)[None, None], by_dr * LOG2E, NEG_BIG)
    by_dr = jnp.concatenate([by_dr, jnp.full_like(by_dr[:, :1], NEG_BIG)], axis=1)
    cfgs = ([(0, NA_ROWS - 1 - a) for a in range(rows_q)],
            [(a, NA_ROWS // 2 - 1) for a in range(rows_q)],
            [(rows_q, NA_ROWS // 2 - 1 - a) for a in range(rows_q)])
    idx = np.full((3, rows_q, rows_k), 2 * NA_ROWS - 1, np.int32)
    for ci, cfg in enumerate(cfgs):
        for a, (off, dr0) in enumerate(cfg):
            for j in range(NA_ROWS):
                idx[ci, a, off + j] = dr0 + j
    tab = jnp.take(by_dr, jnp.asarray(idx.reshape(-1)), axis=1)
    tab = tab.reshape(B_HEADS, 3, rows_q, rows_k, GRID_W, GRID_W).transpose(1, 0, 2, 4, 3, 5)
    return tab.reshape(3, B_HEADS, NA_GROUP, 3 * NA_GROUP)


def _split3(x):
    hi = x.astype(BF16)
    r1 = x - hi.astype(F32)
    mid = r1.astype(BF16)
    lo = (r1 - mid.astype(F32)).astype(BF16)
    return hi, mid, lo


def _hgrn_kernel(z_ref, v_ref, q_ref, lb_ref, scan_ref, o_ref, st_ref, c3_ref, b3_ref, k3_ref, q3_ref, a_ref,
                 *, segs, reverse):
    i = pl.program_id(0)
    n_steps = pl.num_programs(0)
    n_sub = z_ref.shape[0] // HG_TILE
    step = (n_steps - 1 - i) if reverse else i
    local, n_loc = _seq_pos(step, n_sub * HG_TILE, segs)
    is_start = (local == n_loc - 1) if reverse else (local == 0)

    @pl.when(is_start)
    def _():
        st_ref[...] = jnp.zeros_like(st_ref)

    nblk = HG_TILE // HG_BLK
    half = HG_BLK // 2
    n_sc = nblk // 2
    log2e = math.log2(math.e)
    nt = (((1,), (1,)), ((), ()))

    def sel_index(t0):
        shape = (nblk, half, HG_TILE)
        t = lax.broadcasted_iota(jnp.int32, shape, 1) + t0
        j = lax.broadcasted_iota(jnp.int32, shape, 2) - lax.broadcasted_iota(jnp.int32, shape, 0) * HG_BLK
        ok = (j >= 0) & (j < HG_BLK) & ((j >= t) if reverse else (j <= t))
        return jnp.where(ok, j, -1)

    sel = (sel_index(0), sel_index(half))
    rb = lax.broadcasted_iota(jnp.int32, (HG_TILE, HG_TILE), 0) // HG_BLK
    cb = lax.broadcasted_iota(jnp.int32, (HG_TILE, HG_TILE), 1) // HG_BLK
    cross = ((rb % 2 == 0) & (cb == rb + 1)) if reverse else ((rb % 2 == 1) & (cb == rb - 1))
    blk_odd = (lax.broadcasted_iota(jnp.int32, (nblk, 1, HEAD_DIM), 0) % 2) == 1
    edge_row = 0 if reverse else HG_BLK - 1
    scan = scan_ref[...]

    def one_tile(t, carry):
        tile = (n_sub - 1 - t) if reverse else t
        tok = pl.ds(pl.multiple_of(tile * HG_TILE, HG_TILE), HG_TILE)
        heads = [slice(h * HEAD_DIM, (h + 1) * HEAD_DIM) for h in range(C_HEADS)]

        for h, cs in enumerate(heads):
            lb = lb_ref[:, cs]
            f = lb + (1.0 - lb) * jax.nn.sigmoid(z_ref[tok, cs].astype(F32))
            k3 = (1.0 - f).reshape(nblk, HG_BLK, HEAD_DIM)
            pieces = jnp.concatenate(_split3(jnp.log(f)), axis=1)
            bm = jnp.dot(scan, pieces, preferred_element_type=F32)
            b = bm[:, :HEAD_DIM] + bm[:, HEAD_DIM:2 * HEAD_DIM] + bm[:, 2 * HEAD_DIM:]
            b3 = (b * log2e).reshape(nblk, HG_BLK, HEAD_DIM)
            k3_ref[h] = k3
            b3_ref[h] = b3
            c3_ref[h] = b3 - jnp.log2(jnp.maximum(k3, 0.0))
            q3_ref[h] = (q_ref[tok, cs].astype(F32) * (HEAD_DIM ** -0.5)).reshape(nblk, HG_BLK, HEAD_DIM)

        for h in range(C_HEADS):
            halves = ((q3_ref[h, :, :half, :], b3_ref[h, :, :half, :]), (q3_ref[h, :, half:, :], b3_ref[h, :, half:, :]))
            a_half = [jnp.zeros((nblk, half, HG_TILE), F32), jnp.zeros((nblk, half, HG_TILE), F32)]
            for j in range(HG_BLK):
                cj = c3_ref[h, :, j:j + 1, :]
                for hi in range(2):
                    if (j < half * hi) if reverse else (j > half * hi + half - 1):
                        continue
                    qx, bx = halves[hi]
                    p = qx * jnp.exp2(bx - cj)
                    r = jnp.sum(p, axis=-1, keepdims=True)
                    a_half[hi] = jnp.where(sel[hi] == j, r, a_half[hi])
            a_ref[h] = jnp.concatenate(a_half, axis=1).reshape(HG_TILE, HG_TILE)

        for h, cs in enumerate(heads):
            q3, k3, b3 = q3_ref[h], k3_ref[h], b3_ref[h]
            btot = b3_ref[h, :, edge_row:edge_row + 1, :]
            v = v_ref[tok, cs]
            qp3 = q3 * jnp.exp2(b3)
            kp3 = k3 * jnp.exp2(btot - b3)
            qp = qp3.reshape(HG_TILE, HEAD_DIM).astype(BF16)
            kp = kp3.reshape(HG_TILE, HEAD_DIM).astype(BF16)
            g = lax.dot_general(qp, kp, nt, preferred_element_type=F32)
            a_all = jnp.where(cross, g, a_ref[h]).astype(BF16)
            o_acc = jnp.dot(a_all, v, preferred_element_type=F32)

            dec = jnp.exp2(btot)
            one = jnp.ones_like(dec[:1])
            dprev = jnp.concatenate([one, dec[:-1]], axis=0)
            dnext = jnp.concatenate([dec[1:], one], axis=0)
            if reverse:
                qscale, kscale = jnp.where(blk_odd, 1.0, dnext), jnp.where(blk_odd, dprev, 1.0)
            else:
                qscale, kscale = jnp.where(blk_odd, dprev, 1.0), jnp.where(blk_odd, 1.0, dnext)
            qpp = (qp3 * qscale).reshape(HG_TILE, HEAD_DIM).astype(BF16)
            kpp = (kp3 * kscale).reshape(HG_TILE, HEAD_DIM).astype(BF16)
            st = st_ref[h]
            inter = [None] * n_sc
            for m in (range(n_sc - 1, -1, -1) if reverse else range(n_sc)):
                rows = slice(2 * m * HG_BLK, 2 * (m + 1) * HG_BLK)
                inter[m] = lax.dot_general(qpp[rows], st.astype(BF16), nt, preferred_element_type=F32)
                upd = lax.dot_general(v[rows], kpp[rows], (((0,), (0,)), ((), ())), preferred_element_type=F32)
                st = st * (dec[2 * m] * dec[2 * m + 1]) + upd
            st_ref[h] = st
            o_ref[tok, cs] = o_acc + jnp.concatenate(inter, axis=0)
        return carry

    lax.fori_loop(0, n_sub, one_tile, 0)


def _hgrn_scan_matrix(reverse):
    t = np.arange(HG_TILE)
    same = (t[:, None] // HG_BLK) == (t[None, :] // HG_BLK)
    incl = (t[None, :] >= t[:, None]) if reverse else (t[None, :] <= t[:, None])
    return jnp.asarray((same & incl).astype(np.float32), BF16)


def _hgrn_direction(proj, lower_bound, segs, reverse):
    n_tok = proj.shape[0]
    step_tok = next(s for s in (8 * HG_TILE, 4 * HG_TILE, 2 * HG_TILE, HG_TILE) if all(t % s == 0 for _, t in segs))
    n_steps = n_tok // step_tok
    width = C_HEADS * HEAD_DIM
    scan = _hgrn_scan_matrix(reverse)

    def tok_spec(col0):
        return pl.BlockSpec((step_tok, width),
                            lambda i: ((n_steps - 1 - i) if reverse else i, col0 // C_HEADS))

    nblk = HG_TILE // HG_BLK
    return pl.pallas_call(
        functools.partial(_hgrn_kernel, segs=segs, reverse=reverse),
        out_shape=jax.ShapeDtypeStruct((n_tok, width), F32),
        grid=(n_steps,),
        in_specs=[
            tok_spec(COL_C_FB if reverse else COL_C_FF), tok_spec(COL_C_I), tok_spec(COL_C_Q),
            pl.BlockSpec((1, width), lambda i: (0, 0)),
            pl.BlockSpec(scan.shape, lambda i: (0, 0)),
        ],
        out_specs=pl.BlockSpec((step_tok, width), lambda i: ((n_steps - 1 - i) if reverse else i, 0)),
        scratch_shapes=[pltpu.VMEM((C_HEADS, HEAD_DIM, HEAD_DIM), F32)]
        + [pltpu.VMEM((C_HEADS, nblk, HG_BLK, HEAD_DIM), F32)] * 4
        + [pltpu.VMEM((C_HEADS, HG_TILE, HG_TILE), F32)],
        compiler_params=_cparams(("arbitrary",)),
        name="hgrn_bwd" if reverse else "hgrn_fwd",
    )(proj, proj, proj, lower_bound.reshape(1, width), scan)


def _merge_kernel(ga_ref, gb_ref, gc_ref, cg_ref, ya_ref, yb_ref, of_ref, ob_ref, hn_ref,
                  wa_ref, wb_ref, wc_ref, wo_ref, h_ref, g2_ref, o_ref, o2_ref):
    o = of_ref[...] + ob_ref[...]
    heads = []
    for h in range(C_HEADS):
        oh = o[:, h * HEAD_DIM:(h + 1) * HEAD_DIM]
        ms = jnp.mean(oh * oh, axis=-1, keepdims=True)
        heads.append(oh * lax.rsqrt(ms + EPS))
    cg = cg_ref[...].astype(F32)
    yc = (jnp.concatenate(heads, axis=1) * hn_ref[...] * (cg * jax.nn.sigmoid(cg))).astype(BF16)
    m = jax.nn.sigmoid(ga_ref[...].astype(F32)) * jnp.dot(ya_ref[...], wa_ref[...], preferred_element_type=F32)
    m += jax.nn.sigmoid(gb_ref[...].astype(F32)) * jnp.dot(yb_ref[...], wb_ref[...], preferred_element_type=F32)
    m += jax.nn.sigmoid(gc_ref[...].astype(F32)) * jnp.dot(yc, wc_ref[...], preferred_element_type=F32)
    y = h_ref[...] + jnp.dot(m.astype(BF16), wo_ref[...], preferred_element_type=F32)
    o_ref[...] = y
    ms = jnp.mean(y * y, axis=-1, keepdims=True)
    o2_ref[...] = (y * lax.rsqrt(ms + EPS) * g2_ref[...]).astype(o2_ref.dtype)


def _merge(proj, ya, yb, o_f, o_b, hgrn_gain, wa, wb, wc, wo, layer, h, gain2, tm):
    n_tok, d = h.shape
    cw = C_HEADS * HEAD_DIM

    def resident(w):
        return pl.BlockSpec((None,) + w.shape[1:], lambda i: (layer, 0, 0), pipeline_mode=pl.Buffered(1))

    def gate_spec(which):
        return pl.BlockSpec((tm, d), lambda i: (i, which))

    return pl.pallas_call(
        _merge_kernel,
        out_shape=(jax.ShapeDtypeStruct((n_tok, d), F32), jax.ShapeDtypeStruct((n_tok, d), BF16)),
        grid=(n_tok // tm,),
        in_specs=[
            gate_spec(0), gate_spec(1), gate_spec(2),
            pl.BlockSpec((tm, cw), lambda i: (i, COL_C_G // C_HEADS)),
            pl.BlockSpec((tm, ya.shape[1]), lambda i: (i, 0)),
            pl.BlockSpec((tm, cw), lambda i: (i, 0)),
            pl.BlockSpec((tm, cw), lambda i: (i, 0)),
            pl.BlockSpec((tm, cw), lambda i: (i, 0)),
            pl.BlockSpec((1, cw), lambda i: (0, 0)),
            resident(wa), resident(wb), resident(wc), resident(wo),
            pl.BlockSpec((tm, d), lambda i: (i, 0)),
            pl.BlockSpec((1, d), lambda i: (0, 0)),
        ],
        out_specs=(pl.BlockSpec((tm, d), lambda i: (i, 0)), pl.BlockSpec((tm, d), lambda i: (i, 0))),
        compiler_params=_cparams(("parallel",)),
        name="merge_out",
    )(proj, proj, proj, proj, ya, yb, o_f, o_b, hgrn_gain.reshape(1, cw), wa, wb, wc, wo, h,
      gain2.reshape(1, d))


def _tile(n, candidates):
    return next(t for t in candidates if n % t == 0)


def _trunk(xp, xs, segs, p):
    depth = p["w_in"].shape[0]
    n_prompt, n_tok = xp.shape[0], xp.shape[0] + xs.shape[0]
    tm_ffn = _tile(n_tok, (768, 512, 256, 128))
    tm_edge = next(t for t in (768, 512, 256, 128) if n_prompt % t == 0 and n_tok % t == 0)
    tm_proj = _tile(n_tok, (4096, 2048, 1024, 512, 256, 128))
    tm_merge = _tile(n_tok, (384, 256, 128))
    d_ff = p["ffn1_w_gate"].shape[-1]
    tf = 512 if d_ff % 512 == 0 else 256
    mix_w = MIX_COLS * HEAD_DIM

    lb_p = jax.nn.softmax(p["hgrn_lb_logits"].astype(F32), axis=1)
    lower_bounds = jnp.cumsum(lb_p, axis=1) - lb_p[:, :1]
    win_tab = _window_bias_table(p["t5_bias"])
    bf = lambda name: p[name].astype(BF16)
    w = {name: bf(name) for name in ("ffn1_w_gate", "ffn1_w_up", "ffn1_w_down", "w_in", "w_branch_a",
                                      "w_branch_b", "w_branch_c", "w_out", "ffn2_w_gate", "ffn2_w_up",
                                      "ffn2_w_down")}
    ffn1_w = (w["ffn1_w_gate"], w["ffn1_w_up"], w["ffn1_w_down"] * 0.5)
    ffn2_w = (w["ffn2_w_gate"], w["ffn2_w_up"], w["ffn2_w_down"] * 0.5)
    x = n = None
    for l in range(depth):
        if l == 0:
            h, u = _ffn("first", (xp, xs), ffn1_w, l, p["mix_norm"][l], tm_edge, tf, n_prompt, p["ffn1_norm"][l])
        else:
            h, u = _ffn("mid", (x, n), ffn1_w, l, p["mix_norm"][l], tm_ffn, tf)
        proj = _matmul(u, w["w_in"], l, BF16, tm_proj, 512, mix_w // 512)
        ya, yb = _attention_mixers(proj, p["attn_sink"][l].astype(F32) * LOG2E, win_tab,
                                   _na_bias_table(p["na_bias"][l]), segs)
        o_f = _hgrn_direction(proj, lower_bounds[0, l], segs, reverse=False)
        o_b = _hgrn_direction(proj, lower_bounds[1, l], segs, reverse=True)
        h, n = _merge(proj, ya, yb, o_f, o_b, p["hgrn_norm"][l], w["w_branch_a"], w["w_branch_b"],
                      w["w_branch_c"], w["w_out"], l, h, p["ffn2_norm"][l], tm_merge)
        if l == depth - 1:
            return _ffn("last", (h, n), ffn2_w, l, p["final_norm"], tm_edge, tf, n_prompt)
        x, n = _ffn("mid", (h, n), ffn2_w, l, p["ffn1_norm"][l + 1], tm_ffn, tf)


def kernel(x_prompt, x_sample, ffn1_norm, ffn1_w_gate, ffn1_w_up, ffn1_w_down, mix_norm, w_in, attn_sink,
           t5_bias, na_bias, hgrn_lb_logits, hgrn_norm, w_branch_a, w_branch_b, w_branch_c, w_out,
           ffn2_norm, ffn2_w_gate, ffn2_w_up, ffn2_w_down, final_norm):
    params = dict(ffn1_norm=ffn1_norm, ffn1_w_gate=ffn1_w_gate, ffn1_w_up=ffn1_w_up, ffn1_w_down=ffn1_w_down,
                  mix_norm=mix_norm, w_in=w_in, attn_sink=attn_sink, t5_bias=t5_bias, na_bias=na_bias,
                  hgrn_lb_logits=hgrn_lb_logits, hgrn_norm=hgrn_norm, w_branch_a=w_branch_a,
                  w_branch_b=w_branch_b, w_branch_c=w_branch_c, w_out=w_out, ffn2_norm=ffn2_norm,
                  ffn2_w_gate=ffn2_w_gate, ffn2_w_up=ffn2_w_up, ffn2_w_down=ffn2_w_down, final_norm=final_norm)
    d = x_prompt.shape[-1]
    segs = (x_prompt.shape[:2], x_sample.shape[:2])
    for _, t in segs:
        assert t % NA_GROUP == 0 and t // NA_GROUP >= 3 and t // GRID_W >= NA_ROWS and t % WIN_STEP == 0
    y_prompt, y_sample = _trunk(x_prompt.reshape(-1, d), x_sample.reshape(-1, d), segs, params)
    return y_prompt.reshape(x_prompt.shape), y_sample.reshape(x_sample.shape)
```

```python
import functools
import math

import numpy as np
import jax
import jax.numpy as jnp
from jax import lax
from jax.experimental import pallas as pl
from jax.experimental.pallas import tpu as pltpu

F32 = jnp.float32
BF16 = jnp.bfloat16

HEAD_DIM = 128
EPS = 1e-6
A_HEADS = 8
A_KV_HEADS = 2
WINDOW = 128
WIN_BLOCK = 128
T5_BUCKETS = 32
T5_MAX_DIST = 128
B_HEADS = 4
GRID_W = 64
NA_ROWS = 8
NA_COLS = 16
C_HEADS = 4
GATE_COLS = 48
MIX_COLS = 44
COL_A_Q, COL_A_K, COL_A_V = (GATE_COLS + c for c in (0, 8, 10))
COL_B_Q, COL_B_K, COL_B_V = (GATE_COLS + c for c in (12, 16, 20))
COL_C_FF, COL_C_FB, COL_C_I, COL_C_Q, COL_C_G = (GATE_COLS + c for c in (24, 28, 32, 36, 40))

NEG_BIG = -1e30
LOG2E = math.log2(math.e)
V7X_VMEM_LIMIT = 60 * 1024 * 1024
HG_TILE = 128
HG_BLK = 16
NA_GROUP = 4 * GRID_W
WIN_STEP = 2 * WIN_BLOCK


def _cparams(sem):
    return pltpu.CompilerParams(dimension_semantics=sem, vmem_limit_bytes=V7X_VMEM_LIMIT)


def _seq_pos(gb, blk, segs):
    (n_seq0, t0), (_, t1) = segs
    n0, n1 = t0 // blk, t1 // blk
    tot0 = n_seq0 * n0
    in0 = gb < tot0
    local = jnp.where(in0, gb % n0, (gb - tot0) % n1)
    return local, jnp.where(in0, n0, n1)


def _normed(y, gain_ref, dtype):
    ms = jnp.mean(y * y, axis=-1, keepdims=True)
    return (y * lax.rsqrt(ms + EPS) * gain_ref[...]).astype(dtype)


def _ffn_accumulate(n_ref, wg_ref, wu_ref, wd_ref, acc_ref, late_x_ref=None):
    n = n_ref[...]
    g = jnp.dot(n, wg_ref[...], preferred_element_type=F32)
    u = jnp.dot(n, wu_ref[...], preferred_element_type=F32)
    a = (g * jax.nn.sigmoid(g) * u).astype(BF16)
    delta = jnp.dot(a, wd_ref[...], preferred_element_type=F32)
    if late_x_ref is not None:
        delta += jnp.where(pl.program_id(1) == 1, late_x_ref[...], 0.0)
    acc_ref[...] += delta


def _late_tile_spec(tm, d):
    return pl.BlockSpec((tm, d), lambda i, j: (jnp.where(j == 0, jnp.maximum(i - 1, 0), i), 0))


def _ffn_first_kernel(xp_ref, xs_ref, g1_ref, wg_ref, wu_ref, wd_ref, g2_ref, o_ref, o2_ref, n_ref, *, prompt_tiles):
    i, j = pl.program_id(0), pl.program_id(1)

    @pl.when(j == 0)
    def _():
        x = jnp.where(i < prompt_tiles, xp_ref[...], xs_ref[...])
        n_ref[...] = _normed(x, g1_ref, BF16)
        o_ref[...] = x

    _ffn_accumulate(n_ref, wg_ref, wu_ref, wd_ref, o_ref)

    @pl.when(j == pl.num_programs(1) - 1)
    def _():
        o2_ref[...] = _normed(o_ref[...], g2_ref, o2_ref.dtype)


def _ffn_mid_kernel(x_ref, n_ref, wg_ref, wu_ref, wd_ref, g2_ref, o_ref, o2_ref):
    j = pl.program_id(1)

    @pl.when(j == 0)
    def _():
        o_ref[...] = jnp.zeros_like(o_ref)

    _ffn_accumulate(n_ref, wg_ref, wu_ref, wd_ref, o_ref, late_x_ref=x_ref)

    @pl.when(j == pl.num_programs(1) - 1)
    def _():
        o2_ref[...] = _normed(o_ref[...], g2_ref, o2_ref.dtype)


def _ffn_last_kernel(x_ref, n_ref, wg_ref, wu_ref, wd_ref, g2_ref, yp_ref, ys_ref, acc_ref, *, prompt_tiles):
    i, j = pl.program_id(0), pl.program_id(1)
    done = j == pl.num_programs(1) - 1

    @pl.when(j == 0)
    def _():
        acc_ref[...] = jnp.zeros_like(acc_ref)

    _ffn_accumulate(n_ref, wg_ref, wu_ref, wd_ref, acc_ref, late_x_ref=x_ref)

    @pl.when(done & (i < prompt_tiles))
    def _():
        yp_ref[...] = _normed(acc_ref[...], g2_ref, yp_ref.dtype)

    @pl.when(done & (i >= prompt_tiles))
    def _():
        ys_ref[...] = _normed(acc_ref[...], g2_ref, ys_ref.dtype)


def _ffn(kind, acts, weights, layer, gain2, tm, tf, n_prompt=None, gain1=None):
    wg, wu, wd = weights
    d, d_ff = wg.shape[1:]
    n_tok = sum(a.shape[0] for a in acts) if kind == "first" else acts[0].shape[0]
    tok_spec = pl.BlockSpec((tm, d), lambda i, j: (i, 0))
    vec_spec = pl.BlockSpec((1, d), lambda i, j: (0, 0))
    w_specs = [pl.BlockSpec((None, d, tf), lambda i, j: (layer, 0, j)),
               pl.BlockSpec((None, d, tf), lambda i, j: (layer, 0, j)),
               pl.BlockSpec((None, tf, d), lambda i, j: (layer, j, 0))]
    if kind != "mid":
        assert n_prompt % tm == 0
        pt = n_prompt // tm
        prompt_spec = pl.BlockSpec((tm, d), lambda i, j: (jnp.minimum(i, pt - 1), 0))
        sample_spec = pl.BlockSpec((tm, d), lambda i, j: (jnp.maximum(i - pt, 0), 0))
    both = (jax.ShapeDtypeStruct((n_tok, d), F32), jax.ShapeDtypeStruct((n_tok, d), BF16))
    if kind == "first":
        body = functools.partial(_ffn_first_kernel, prompt_tiles=pt)
        operands = (*acts, gain1.reshape(1, d))
        in_specs = [prompt_spec, sample_spec, vec_spec]
        out_shape, out_specs, scratch = both, (tok_spec, tok_spec), [pltpu.VMEM((tm, d), BF16)]
    elif kind == "mid":
        body, operands, in_specs = _ffn_mid_kernel, acts, [_late_tile_spec(tm, d), tok_spec]
        out_shape, out_specs, scratch = both, (tok_spec, tok_spec), []
    else:
        body = functools.partial(_ffn_last_kernel, prompt_tiles=pt)
        operands, in_specs = acts, [_late_tile_spec(tm, d), tok_spec]
        out_shape = (jax.ShapeDtypeStruct((n_prompt, d), F32), jax.ShapeDtypeStruct((n_tok - n_prompt, d), F32))
        out_specs, scratch = (prompt_spec, sample_spec), [pltpu.VMEM((tm, d), F32)]
    return pl.pallas_call(
        body,
        out_shape=out_shape,
        grid=(n_tok // tm, d_ff // tf),
        in_specs=[*in_specs, *w_specs, vec_spec],
        out_specs=out_specs,
        scratch_shapes=scratch,
        compiler_params=_cparams(("arbitrary", "arbitrary")),
        name="ffn_" + kind,
    )(*operands, wg, wu, wd, gain2.reshape(1, d))


def _matmul_kernel(a_ref, w_ref, o_ref):
    o_ref[...] = jnp.dot(a_ref[...], w_ref[...], preferred_element_type=F32).astype(o_ref.dtype)


def _matmul(a, w, layer, out_dtype, tm, tn, rotate):
    n_tok, k = a.shape
    n_out = w.shape[-1]
    n_col = n_out // tn
    return pl.pallas_call(
        _matmul_kernel,
        out_shape=jax.ShapeDtypeStruct((n_tok, n_out), out_dtype),
        grid=(n_tok // tm, n_out // tn),
        in_specs=[pl.BlockSpec((tm, k), lambda i, j: (i, 0)),
                  pl.BlockSpec((None, k, tn), lambda i, j: (layer, 0, (j + rotate) % n_col))],
        out_specs=pl.BlockSpec((tm, tn), lambda i, j: (i, j)),
        compiler_params=_cparams(("parallel", "arbitrary")),
        name="proj_in",
    )(a, w)


def _wattn_phases(sink_ref, q_ref, kp_ref, kc_ref, kn_ref, vp_ref, vc_ref, vn_ref, bias_ref, o_ref, s_ref, e_ref, segs):
    local, n_loc = _seq_pos(pl.program_id(0), WIN_STEP, segs)
    variant = (jnp.where(local == 0, 0, 1), jnp.where(local == n_loc - 1, 2, 1))
    group = A_HEADS // A_KV_HEADS
    gw = group * WIN_BLOCK
    scale = HEAD_DIM ** -0.5 * LOG2E
    units = [(blk, kv) for blk in range(WIN_STEP // WIN_BLOCK) for kv in range(A_KV_HEADS)]

    def keys_of(p_ref, c_ref, n_ref, blk, kv):
        cs = slice(kv * HEAD_DIM, (kv + 1) * HEAD_DIM)
        cat = jnp.concatenate([p_ref[:, cs], c_ref[:, cs], n_ref[:, cs]], axis=0)
        return cat[blk * WIN_BLOCK:(blk + 3) * WIN_BLOCK]

    def scores():
        for u, (blk, kv) in enumerate(units):
            rows = slice(blk * WIN_BLOCK, (blk + 1) * WIN_BLOCK)
            qs = jnp.concatenate([q_ref[rows, h * HEAD_DIM:(h + 1) * HEAD_DIM]
                                  for h in range(kv * group, (kv + 1) * group)], axis=0)
            s = lax.dot_general(qs, keys_of(kp_ref, kc_ref, kn_ref, blk, kv), (((1,), (1,)), ((), ())),
                                preferred_element_type=F32)
            s_ref[u] = s * scale + bias_ref[variant[blk], kv * gw:(kv + 1) * gw, :]

    def outputs():
        for u, (blk, kv) in enumerate(units):
            rows = slice(blk * WIN_BLOCK, (blk + 1) * WIN_BLOCK)
            dens = []
            for i in range(group):
                hr = slice(i * WIN_BLOCK, (i + 1) * WIN_BLOCK)
                sh = s_ref[u, hr, :]
                sink = sink_ref[kv * group + i]
                m = jnp.maximum(jnp.max(sh, axis=-1, keepdims=True), sink)
                e = jnp.exp2(sh - m)
                dens.append(jnp.sum(e, axis=-1, keepdims=True) + jnp.exp2(sink - m))
                e_ref[u, hr, :] = e.astype(BF16)
            o4 = jnp.dot(e_ref[u], keys_of(vp_ref, vc_ref, vn_ref, blk, kv), preferred_element_type=F32)
            for i in range(group):
                h = kv * group + i
                o_ref[rows, h * HEAD_DIM:(h + 1) * HEAD_DIM] = (
                    o4[i * WIN_BLOCK:(i + 1) * WIN_BLOCK] / dens[i]).astype(o_ref.dtype)

    return scores, outputs


def _window_specs(n_tok, bias_tab):
    nb = n_tok // WIN_BLOCK
    per_step = WIN_STEP // WIN_BLOCK
    kv_w = A_KV_HEADS * HEAD_DIM
    n_units = per_step * A_KV_HEADS
    gw = (A_HEADS // A_KV_HEADS) * WIN_BLOCK

    def kv_specs(col0):
        col = col0 // A_KV_HEADS
        return [pl.BlockSpec((WIN_BLOCK, kv_w), lambda s: (jnp.maximum(per_step * s - 1, 0), col)),
                pl.BlockSpec((WIN_STEP, kv_w), lambda s: (s, col)),
                pl.BlockSpec((WIN_BLOCK, kv_w), lambda s: (jnp.minimum(per_step * (s + 1), nb - 1), col))]

    in_specs = [pl.BlockSpec(memory_space=pltpu.SMEM),
                pl.BlockSpec((WIN_STEP, A_HEADS * HEAD_DIM), lambda s: (s, COL_A_Q // A_HEADS)),
                *kv_specs(COL_A_K), *kv_specs(COL_A_V),
                pl.BlockSpec(bias_tab.shape, lambda s: (0, 0, 0))]
    out_spec = pl.BlockSpec((WIN_STEP, A_HEADS * HEAD_DIM), lambda s: (s, 0))
    scratch = [pltpu.VMEM((n_units, gw, 3 * WIN_BLOCK), F32), pltpu.VMEM((n_units, gw, 3 * WIN_BLOCK), BF16)]
    return in_specs, out_spec, scratch


def _t5_bucket(rel):
    nb = T5_BUCKETS // 2
    max_exact = nb // 2
    base = jnp.where(rel > 0, nb, 0)
    n = jnp.abs(rel)
    large = max_exact + (jnp.log(jnp.maximum(n, 1).astype(F32) / max_exact)
                         / math.log(T5_MAX_DIST / max_exact) * (nb - max_exact)).astype(jnp.int32)
    large = jnp.minimum(large, nb - 1)
    return base + jnp.where(n < max_exact, n, large)


def _window_bias_table(t5_bias):
    qi = jnp.arange(WIN_BLOCK)[:, None]
    si = jnp.arange(3 * WIN_BLOCK)[None, :]
    rel = si - WIN_BLOCK - qi
    onehot = (_t5_bucket(rel)[None] == jnp.arange(T5_BUCKETS)[:, None, None]).astype(F32)
    bias = jnp.einsum('bh,bqs->hqs', t5_bias.astype(F32), onehot, precision=lax.Precision.HIGHEST)
    band = jnp.abs(rel) <= WINDOW
    variants = []
    for lo_ok, hi_ok in ((False, True), (True, True), (True, False)):
        ok = band & ((si >= WIN_BLOCK) | lo_ok) & ((si < 2 * WIN_BLOCK) | hi_ok)
        variants.append(jnp.where(ok[None], bias * LOG2E, NEG_BIG).reshape(A_HEADS * WIN_BLOCK, 3 * WIN_BLOCK))
    return jnp.stack(variants)


def _na_window(g, segs):
    lg, ng = _seq_pos(g, NA_GROUP, segs)
    return lg, ng, jnp.clip(lg - 1, 0, ng - 3)


def _na_phases(q_ref, k0_ref, k1_ref, k2_ref, v0_ref, v1_ref, v2_ref, bias_ref, o_ref, s_ref):
    scale = HEAD_DIM ** -0.5 * LOG2E
    heads = [slice(h * HEAD_DIM, (h + 1) * HEAD_DIM) for h in range(B_HEADS)]

    def scores():
        for h, cs in enumerate(heads):
            kcat = jnp.concatenate([k0_ref[:, cs], k1_ref[:, cs], k2_ref[:, cs]], axis=0)
            s = lax.dot_general(q_ref[:, cs], kcat, (((1,), (1,)), ((), ())), preferred_element_type=F32)
            s_ref[h] = s * scale + bias_ref[h]

    def outputs():
        for h, cs in enumerate(heads):
            vcat = jnp.concatenate([v0_ref[:, cs], v1_ref[:, cs], v2_ref[:, cs]], axis=0)
            s = s_ref[h]
            m = jnp.max(s, axis=-1, keepdims=True)
            e = jnp.exp2(s - m)
            den = jnp.sum(e, axis=-1, keepdims=True)
            oh = jnp.dot(e.astype(BF16), vcat, preferred_element_type=F32)
            o_ref[:, cs] = (oh / den).astype(o_ref.dtype)

    return scores, outputs


def _na_specs(segs):
    width = B_HEADS * HEAD_DIM

    def kv_spec(col0, j):
        def index_map(g):
            lg, _, lo = _na_window(g, segs)
            return (g - lg + lo + j, col0 // B_HEADS)
        return pl.BlockSpec((NA_GROUP, width), index_map)

    def bias_map(g):
        lg, ng, _ = _na_window(g, segs)
        return (jnp.where(lg == 0, 0, jnp.where(lg == ng - 1, 2, 1)), 0, 0, 0)

    in_specs = [pl.BlockSpec((NA_GROUP, width), lambda g: (g, COL_B_Q // B_HEADS)),
                kv_spec(COL_B_K, 0), kv_spec(COL_B_K, 1), kv_spec(COL_B_K, 2),
                kv_spec(COL_B_V, 0), kv_spec(COL_B_V, 1), kv_spec(COL_B_V, 2),
                pl.BlockSpec((None, B_HEADS, NA_GROUP, 3 * NA_GROUP), bias_map)]
    out_spec = pl.BlockSpec((NA_GROUP, width), lambda g: (g, 0))
    scratch = [pltpu.VMEM((B_HEADS, NA_GROUP, 3 * NA_GROUP), F32)]
    return in_specs, out_spec, scratch


N_WIN_IN, N_NA_IN = 9, 8


def _attention_kernel(*refs, segs):
    win_in, na_in = refs[:N_WIN_IN], refs[N_WIN_IN:N_WIN_IN + N_NA_IN]
    ya_ref, yb_ref, ws_ref, we_ref, ns_ref = refs[N_WIN_IN + N_NA_IN:]
    win_scores, win_outputs = _wattn_phases(*win_in, ya_ref, ws_ref, we_ref, segs)
    na_scores, na_outputs = _na_phases(*na_in, yb_ref, ns_ref)
    win_scores()
    na_scores()
    win_outputs()
    na_outputs()


def _attention_mixers(proj, sink, win_tab, na_tab, segs):
    assert WIN_STEP == NA_GROUP
    n_tok = proj.shape[0]
    win_specs, win_out, win_scratch = _window_specs(n_tok, win_tab)
    na_specs, na_out, na_scratch = _na_specs(segs)
    return pl.pallas_call(
        functools.partial(_attention_kernel, segs=segs),
        out_shape=(jax.ShapeDtypeStruct((n_tok, A_HEADS * HEAD_DIM), BF16),
                   jax.ShapeDtypeStruct((n_tok, B_HEADS * HEAD_DIM), BF16)),
        grid=(n_tok // WIN_STEP,),
        in_specs=[*win_specs, *na_specs],
        out_specs=(win_out, na_out),
        scratch_shapes=[*win_scratch, *na_scratch],
        compiler_params=_cparams(("parallel",)),
        name="attn_ab",
    )(sink, *([proj] * 7), win_tab, *([proj] * 7), na_tab)


def _na_bias_table(rel_table):
    rows_q = NA_GROUP // GRID_W
    rows_k = 3 * rows_q
    c = np.arange(GRID_W)
    col_start = np.clip(c - NA_COLS // 2, 0, GRID_W - NA_COLS)
    col_ok = (c[None, :] >= col_start[:, None]) & (c[None, :] < col_start[:, None] + NA_COLS)
    dc = np.clip(c[None, :] - c[:, None], -(NA_COLS - 1), NA_COLS - 1) + NA_COLS - 1
    onehot = (dc[None] == np.arange(2 * NA_COLS - 1)[:, None, None]).astype(np.float32)
    by_dr = jnp.einsum('hrd,dqk->hrqk', rel_table.astype(F32), jnp.asarray(onehot),
                       precision=lax.Precision.HIGHEST)
    by_dr = jnp.where(jnp.asarray(col_ok)[None, None], by_dr * LOG2E, NEG_BIG)
    by_dr = jnp.concatenate([by_dr, jnp.full_like(by_dr[:, :1], NEG_BIG)], axis=1)
    cfgs = ([(0, NA_ROWS - 1 - a) for a in range(rows_q)],
            [(a, NA_ROWS // 2 - 1) for a in range(rows_q)],
            [(rows_q, NA_ROWS // 2 - 1 - a) for a in range(rows_q)])
    idx = np.full((3, rows_q, rows_k), 2 * NA_ROWS - 1, np.int32)
    for ci, cfg in enumerate(cfgs):
        for a, (off, dr0) in enumerate(cfg):
            for j in range(NA_ROWS):
                idx[ci, a, off + j] = dr0 + j
    tab = jnp.take(by_dr, jnp.asarray(idx.reshape(-1)), axis=1)
    tab = tab.reshape(B_HEADS, 3, rows_q, rows_k, GRID_W, GRID_W).transpose(1, 0, 2, 4, 3, 5)
    return tab.reshape(3, B_HEADS, NA_GROUP, 3 * NA_GROUP)


def _split3(x):
    hi = x.astype(BF16)
    r1 = x - hi.astype(F32)
    mid = r1.astype(BF16)
    lo = (r1 - mid.astype(F32)).astype(BF16)
    return hi, mid, lo


def _hgrn_kernel(z_ref, v_ref, q_ref, lb_ref, scan_ref, o_ref, st_ref, c3_ref, b3_ref, k3_ref, q3_ref, a_ref,
                 *, segs, reverse):
    i = pl.program_id(0)
    n_steps = pl.num_programs(0)
    n_sub = z_ref.shape[0] // HG_TILE
    step = (n_steps - 1 - i) if reverse else i
    local, n_loc = _seq_pos(step, n_sub * HG_TILE, segs)
    is_start = (local == n_loc - 1) if reverse else (local == 0)

    @pl.when(is_start)
    def _():
        st_ref[...] = jnp.zeros_like(st_ref)

    nblk = HG_TILE // HG_BLK
    half = HG_BLK // 2
    n_sc = nblk // 2
    log2e = math.log2(math.e)
    nt = (((1,), (1,)), ((), ()))

    def sel_index(t0):
        shape = (nblk, half, HG_TILE)
        t = lax.broadcasted_iota(jnp.int32, shape, 1) + t0
        j = lax.broadcasted_iota(jnp.int32, shape, 2) - lax.broadcasted_iota(jnp.int32, shape, 0) * HG_BLK
        ok = (j >= 0) & (j < HG_BLK) & ((j >= t) if reverse else (j <= t))
        return jnp.where(ok, j, -1)

    sel = (sel_index(0), sel_index(half))
    rb = lax.broadcasted_iota(jnp.int32, (HG_TILE, HG_TILE), 0) // HG_BLK
    cb = lax.broadcasted_iota(jnp.int32, (HG_TILE, HG_TILE), 1) // HG_BLK
    cross = ((rb % 2 == 0) & (cb == rb + 1)) if reverse else ((rb % 2 == 1) & (cb == rb - 1))
    blk_odd = (lax.broadcasted_iota(jnp.int32, (nblk, 1, HEAD_DIM), 0) % 2) == 1
    edge_row = 0 if reverse else HG_BLK - 1
    scan = scan_ref[...]

    def one_tile(t, carry):
        tile = (n_sub - 1 - t) if reverse else t
        tok = pl.ds(pl.multiple_of(tile * HG_TILE, HG_TILE), HG_TILE)
        heads = [slice(h * HEAD_DIM, (h + 1) * HEAD_DIM) for h in range(C_HEADS)]

        for h, cs in enumerate(heads):
            lb = lb_ref[:, cs]
            f = lb + (1.0 - lb) * jax.nn.sigmoid(z_ref[tok, cs].astype(F32))
            k3 = (1.0 - f).reshape(nblk, HG_BLK, HEAD_DIM)
            pieces = jnp.concatenate(_split3(jnp.log(f)), axis=1)
            bm = jnp.dot(scan, pieces, preferred_element_type=F32)
            b = bm[:, :HEAD_DIM] + bm[:, HEAD_DIM:2 * HEAD_DIM] + bm[:, 2 * HEAD_DIM:]
            b3 = (b * log2e).reshape(nblk, HG_BLK, HEAD_DIM)
            k3_ref[h] = k3
            b3_ref[h] = b3
            c3_ref[h] = b3 - jnp.log2(jnp.maximum(k3, 0.0))
            q3_ref[h] = (q_ref[tok, cs].astype(F32) * (HEAD_DIM ** -0.5)).reshape(nblk, HG_BLK, HEAD_DIM)

        for h in range(C_HEADS):
            halves = ((q3_ref[h, :, :half, :], b3_ref[h, :, :half, :]), (q3_ref[h, :, half:, :], b3_ref[h, :, half:, :]))
            a_half = [jnp.zeros((nblk, half, HG_TILE), F32), jnp.zeros((nblk, half, HG_TILE), F32)]
            for j in range(HG_BLK):
                cj = c3_ref[h, :, j:j + 1, :]
                for hi in range(2):
                    if (j < half * hi) if reverse else (j > half * hi + half - 1):
                        continue
                    qx, bx = halves[hi]
                    p = qx * jnp.exp2(bx - cj)
                    r = jnp.sum(p, axis=-1, keepdims=True)
                    a_half[hi] = jnp.where(sel[hi] == j, r, a_half[hi])
            a_ref[h] = jnp.concatenate(a_half, axis=1).reshape(HG_TILE, HG_TILE)

        for h, cs in enumerate(heads):
            q3, k3, b3 = q3_ref[h], k3_ref[h], b3_ref[h]
            btot = b3_ref[h, :, edge_row:edge_row + 1, :]
            v = v_ref[tok, cs]
            qp3 = q3 * jnp.exp2(b3)
            kp3 = k3 * jnp.exp2(btot - b3)
            qp = qp3.reshape(HG_TILE, HEAD_DIM).astype(BF16)
            kp = kp3.reshape(HG_TILE, HEAD_DIM).astype(BF16)
            g = lax.dot_general(qp, kp, nt, preferred_element_type=F32)
            a_all = jnp.where(cross, g, a_ref[h]).astype(BF16)
            o_acc = jnp.dot(a_all, v, preferred_element_type=F32)

            dec = jnp.exp2(btot)
            one = jnp.ones_like(dec[:1])
            dprev = jnp.concatenate([one, dec[:-1]], axis=0)
            dnext = jnp.concatenate([dec[1:], one], axis=0)
            if reverse:
                qscale, kscale = jnp.where(blk_odd, 1.0, dnext), jnp.where(blk_odd, dprev, 1.0)
            else:
                qscale, kscale = jnp.where(blk_odd, dprev, 1.0), jnp.where(blk_odd, 1.0, dnext)
            qpp = (qp3 * qscale).reshape(HG_TILE, HEAD_DIM).astype(BF16)
            kpp = (kp3 * kscale).reshape(HG_TILE, HEAD_DIM).astype(BF16)
            st = st_ref[h]
            inter = [None] * n_sc
            for m in (range(n_sc - 1, -1, -1) if reverse else range(n_sc)):
                rows = slice(2 * m * HG_BLK, 2 * (m + 1) * HG_BLK)
                inter[m] = lax.dot_general(qpp[rows], st.astype(BF16), nt, preferred_element_type=F32)
                upd = lax.dot_general(v[rows], kpp[rows], (((0,), (0,)), ((), ())), preferred_element_type=F32)
                st = st * (dec[2 * m] * dec[2 * m + 1]) + upd
            st_ref[h] = st
            o_ref[tok, cs] = o_acc + jnp.concatenate(inter, axis=0)
        return carry

    lax.fori_loop(0, n_sub, one_tile, 0)


def _hgrn_scan_matrix(reverse):
    t = np.arange(HG_TILE)
    same = (t[:, None] // HG_BLK) == (t[None, :] // HG_BLK)
    incl = (t[None, :] >= t[:, None]) if reverse else (t[None, :] <= t[:, None])
    return jnp.asarray((same & incl).astype(np.float32), BF16)


def _hgrn_direction(proj, lower_bound, segs, reverse):
    n_tok = proj.shape[0]
    step_tok = next(s for s in (8 * HG_TILE, 4 * HG_TILE, 2 * HG_TILE, HG_TILE) if all(t % s == 0 for _, t in segs))
    n_steps = n_tok // step_tok
    width = C_HEADS * HEAD_DIM
    scan = _hgrn_scan_matrix(reverse)

    def tok_spec(col0):
        return pl.BlockSpec((step_tok, width),
                            lambda i: ((n_steps - 1 - i) if reverse else i, col0 // C_HEADS))

    nblk = HG_TILE // HG_BLK
    return pl.pallas_call(
        functools.partial(_hgrn_kernel, segs=segs, reverse=reverse),
        out_shape=jax.ShapeDtypeStruct((n_tok, width), F32),
        grid=(n_steps,),
        in_specs=[
            tok_spec(COL_C_FB if reverse else COL_C_FF), tok_spec(COL_C_I), tok_spec(COL_C_Q),
            pl.BlockSpec((1, width), lambda i: (0, 0)),
            pl.BlockSpec(scan.shape, lambda i: (0, 0)),
        ],
        out_specs=pl.BlockSpec((step_tok, width), lambda i: ((n_steps - 1 - i) if reverse else i, 0)),
        scratch_shapes=[pltpu.VMEM((C_HEADS, HEAD_DIM, HEAD_DIM), F32)]
        + [pltpu.VMEM((C_HEADS, nblk, HG_BLK, HEAD_DIM), F32)] * 4
        + [pltpu.VMEM((C_HEADS, HG_TILE, HG_TILE), F32)],
        compiler_params=_cparams(("arbitrary",)),
        name="hgrn_bwd" if reverse else "hgrn_fwd",
    )(proj, proj, proj, lower_bound.reshape(1, width), scan)


def _merge_kernel(ga_ref, gb_ref, gc_ref, cg_ref, ya_ref, yb_ref, of_ref, ob_ref, hn_ref,
                  wa_ref, wb_ref, wc_ref, wo_ref, h_ref, g2_ref, o_ref, o2_ref):
    o = of_ref[...] + ob_ref[...]
    heads = []
    for h in range(C_HEADS):
        oh = o[:, h * HEAD_DIM:(h + 1) * HEAD_DIM]
        ms = jnp.mean(oh * oh, axis=-1, keepdims=True)
        heads.append(oh * lax.rsqrt(ms + EPS))
    cg = cg_ref[...].astype(F32)
    yc = (jnp.concatenate(heads, axis=1) * hn_ref[...] * (cg * jax.nn.sigmoid(cg))).astype(BF16)
    m = jax.nn.sigmoid(ga_ref[...].astype(F32)) * jnp.dot(ya_ref[...], wa_ref[...], preferred_element_type=F32)
    m += jax.nn.sigmoid(gb_ref[...].astype(F32)) * jnp.dot(yb_ref[...], wb_ref[...], preferred_element_type=F32)
    m += jax.nn.sigmoid(gc_ref[...].astype(F32)) * jnp.dot(yc, wc_ref[...], preferred_element_type=F32)
    y = h_ref[...] + jnp.dot(m.astype(BF16), wo_ref[...], preferred_element_type=F32)
    o_ref[...] = y
    ms = jnp.mean(y * y, axis=-1, keepdims=True)
    o2_ref[...] = (y * lax.rsqrt(ms + EPS) * g2_ref[...]).astype(o2_ref.dtype)


def _merge(proj, ya, yb, o_f, o_b, hgrn_gain, wa, wb, wc, wo, layer, h, gain2, tm):
    n_tok, d = h.shape
    cw = C_HEADS * HEAD_DIM

    def resident(w):
        return pl.BlockSpec((None,) + w.shape[1:], lambda i: (layer, 0, 0), pipeline_mode=pl.Buffered(1))

    def gate_spec(which):
        return pl.BlockSpec((tm, d), lambda i: (i, which))

    return pl.pallas_call(
        _merge_kernel,
        out_shape=(jax.ShapeDtypeStruct((n_tok, d), F32), jax.ShapeDtypeStruct((n_tok, d), BF16)),
        grid=(n_tok // tm,),
        in_specs=[
            gate_spec(0), gate_spec(1), gate_spec(2),
            pl.BlockSpec((tm, cw), lambda i: (i, COL_C_G // C_HEADS)),
            pl.BlockSpec((tm, ya.shape[1]), lambda i: (i, 0)),
            pl.BlockSpec((tm, cw), lambda i: (i, 0)),
            pl.BlockSpec((tm, cw), lambda i: (i, 0)),
            pl.BlockSpec((tm, cw), lambda i: (i, 0)),
            pl.BlockSpec((1, cw), lambda i: (0, 0)),
            resident(wa), resident(wb), resident(wc), resident(wo),
            pl.BlockSpec((tm, d), lambda i: (i, 0)),
            pl.BlockSpec((1, d), lambda i: (0, 0)),
        ],
        out_specs=(pl.BlockSpec((tm, d), lambda i: (i, 0)), pl.BlockSpec((tm, d), lambda i: (i, 0))),
        compiler_params=_cparams(("parallel",)),
        name="merge_out",
    )(proj, proj, proj, proj, ya, yb, o_f, o_b, hgrn_gain.reshape(1, cw), wa, wb, wc, wo, h,
      gain2.reshape(1, d))


def _tile(n, candidates):
    return next(t for t in candidates if n % t == 0)


def _trunk(xp, xs, segs, p):
    depth = p["w_in"].shape[0]
    n_prompt, n_tok = xp.shape[0], xp.shape[0] + xs.shape[0]
    tm_ffn = _tile(n_tok, (768, 512, 256, 128))
    tm_edge = next(t for t in (768, 512, 256, 128) if n_prompt % t == 0 and n_tok % t == 0)
    tm_proj = _tile(n_tok, (4096, 2048, 1024, 512, 256, 128))
    tm_merge = _tile(n_tok, (384, 256, 128))
    d_ff = p["ffn1_w_gate"].shape[-1]
    tf = 512 if d_ff % 512 == 0 else 256
    mix_w = MIX_COLS * HEAD_DIM

    lb_p = jax.nn.softmax(p["hgrn_lb_logits"].astype(F32), axis=1)
    lower_bounds = jnp.cumsum(lb_p, axis=1) - lb_p[:, :1]
    win_tab = _window_bias_table(p["t5_bias"])
    bf = lambda name: p[name].astype(BF16)
    w = {name: bf(name) for name in ("ffn1_w_gate", "ffn1_w_up", "ffn1_w_down", "w_in", "w_branch_a",
                                      "w_branch_b", "w_branch_c", "w_out", "ffn2_w_gate", "ffn2_w_up",
                                      "ffn2_w_down")}
    ffn1_w = (w["ffn1_w_gate"], w["ffn1_w_up"], w["ffn1_w_down"] * 0.5)
    ffn2_w = (w["ffn2_w_gate"], w["ffn2_w_up"], w["ffn2_w_down"] * 0.5)
    x = n = None
    for l in range(depth):
        if l == 0:
            h, u = _ffn("first", (xp, xs), ffn1_w, l, p["mix_norm"][l], tm_edge, tf, n_prompt, p["ffn1_norm"][l])
        else:
            h, u = _ffn("mid", (x, n), ffn1_w, l, p["mix_norm"][l], tm_ffn, tf)
        proj = _matmul(u, w["w_in"], l, BF16, tm_proj, 512, mix_w // 512)
        ya, yb = _attention_mixers(proj, p["attn_sink"][l].astype(F32) * LOG2E, win_tab,
                                   _na_bias_table(p["na_bias"][l]), segs)
        o_f = _hgrn_direction(proj, lower_bounds[0, l], segs, reverse=False)
        o_b = _hgrn_direction(proj, lower_bounds[1, l], segs, reverse=True)
        h, n = _merge(proj, ya, yb, o_f, o_b, p["hgrn_norm"][l], w["w_branch_a"], w["w_branch_b"],
                      w["w_branch_c"], w["w_out"], l, h, p["ffn2_norm"][l], tm_merge)
        if l == depth - 1:
            return _ffn("last", (h, n), ffn2_w, l, p["final_norm"], tm_edge, tf, n_prompt)
        x, n = _ffn("mid", (h, n), ffn2_w, l, p["ffn1_norm"][l + 1], tm_ffn, tf)


def kernel(x_prompt, x_sample, ffn1_norm, ffn1_w_gate, ffn1_w_up, ffn1_w_down, mix_norm, w_in, attn_sink,
           t5_bias, na_bias, hgrn_lb_logits, hgrn_norm, w_branch_a, w_branch_b, w_branch_c, w_out,
           ffn2_norm, ffn2_w_gate, ffn2_w_up, ffn2_w_down, final_norm):
    params = dict(ffn1_norm=ffn1_norm, ffn1_w_gate=ffn1_w_gate, ffn1_w_up=ffn1_w_up, ffn1_w_down=ffn1_w_down,
                  mix_norm=mix_norm, w_in=w_in, attn_sink=attn_sink, t5_bias=t5_bias, na_bias=na_bias,
                  hgrn_lb_logits=hgrn_lb_logits, hgrn_norm=hgrn_norm, w_branch_a=w_branch_a,
                  w_branch_b=w_branch_b, w_branch_c=w_branch_c, w_out=w_out, ffn2_norm=ffn2_norm,
                  ffn2_w_gate=ffn2_w_gate, ffn2_w_up=ffn2_w_up, ffn2_w_down=ffn2_w_down, final_norm=final_norm)
    d = x_prompt.shape[-1]
    segs = (x_prompt.shape[:2], x_sample.shape[:2])
    for _, t in segs:
        assert t % NA_GROUP == 0 and t // NA_GROUP >= 3 and t // GRID_W >= NA_ROWS and t % WIN_STEP == 0
    y_prompt, y_sample = _trunk(x_prompt.reshape(-1, d), x_sample.reshape(-1, d), segs, params)
    return y_prompt.reshape(x_prompt.shape), y_sample.reshape(x_sample.shape)
```

```python
import functools
import math

import numpy as np
import jax
import jax.numpy as jnp
from jax import lax
from jax.experimental import pallas as pl
from jax.experimental.pallas import tpu as pltpu

F32 = jnp.float32
BF16 = jnp.bfloat16

HEAD_DIM = 128
EPS = 1e-6
A_HEADS = 8
A_KV_HEADS = 2
WINDOW = 128
WIN_BLOCK = 128
T5_BUCKETS = 32
T5_MAX_DIST = 128
B_HEADS = 4
GRID_W = 64
NA_ROWS = 8
NA_COLS = 16
C_HEADS = 4
GATE_COLS = 48
MIX_COLS = 44
COL_A_Q, COL_A_K, COL_A_V = (GATE_COLS + c for c in (0, 8, 10))
COL_B_Q, COL_B_K, COL_B_V = (GATE_COLS + c for c in (12, 16, 20))
COL_C_FF, COL_C_FB, COL_C_I, COL_C_Q, COL_C_G = (GATE_COLS + c for c in (24, 28, 32, 36, 40))

NEG_BIG = -1e30
LOG2E = math.log2(math.e)
V7X_VMEM_LIMIT = 60 * 1024 * 1024
HG_TILE = 128
HG_BLK = 16
NA_GROUP = 4 * GRID_W
WIN_STEP = 2 * WIN_BLOCK


def _cparams(sem):
    return pltpu.CompilerParams(dimension_semantics=sem, vmem_limit_bytes=V7X_VMEM_LIMIT)


def _seq_pos(gb, blk, segs):
    (n_seq0, t0), (_, t1) = segs
    n0, n1 = t0 // blk, t1 // blk
    tot0 = n_seq0 * n0
    in0 = gb < tot0
    local = jnp.where(in0, gb % n0, (gb - tot0) % n1)
    return local, jnp.where(in0, n0, n1)


def _normed(y, gain_ref, dtype):
    ms = jnp.mean(y * y, axis=-1, keepdims=True)
    return (y * lax.rsqrt(ms + EPS) * gain_ref[...]).astype(dtype)


def _ffn_gate(n, wg_ref, wu_ref, a_ref):
    g = jnp.dot(n, wg_ref[...], preferred_element_type=F32)
    u = jnp.dot(n, wu_ref[...], preferred_element_type=F32)
    a_ref[...] = (g * jax.nn.sigmoid(g) * u).astype(a_ref.dtype)


def _ffn_up_kernel(n_ref, wg_ref, wu_ref, a_ref):
    _ffn_gate(n_ref[...], wg_ref, wu_ref, a_ref)


def _ffn_up_first_kernel(xp_ref, xs_ref, g1_ref, wg_ref, wu_ref, a_ref, x_ref, n_ref, *, prompt_tiles):
    @pl.when(pl.program_id(1) == 0)
    def _():
        x = jnp.where(pl.program_id(0) < prompt_tiles, xp_ref[...], xs_ref[...])
        n_ref[...] = _normed(x, g1_ref, BF16)
        x_ref[...] = x

    _ffn_gate(n_ref[...], wg_ref, wu_ref, a_ref)


def _ffn_down_kernel(a_ref, wd_ref, x_ref, g2_ref, o_ref, o2_ref):
    y = x_ref[...] + jnp.dot(a_ref[...], wd_ref[...], preferred_element_type=F32)
    o_ref[...] = y
    o2_ref[...] = _normed(y, g2_ref, o2_ref.dtype)


def _ffn_down_last_kernel(a_ref, wd_ref, x_ref, g2_ref, yp_ref, ys_ref, *, prompt_tiles):
    i = pl.program_id(0)
    y = x_ref[...] + jnp.dot(a_ref[...], wd_ref[...], preferred_element_type=F32)

    @pl.when(i < prompt_tiles)
    def _():
        yp_ref[...] = _normed(y, g2_ref, yp_ref.dtype)

    @pl.when(i >= prompt_tiles)
    def _():
        ys_ref[...] = _normed(y, g2_ref, ys_ref.dtype)


def _ffn_up(n, weights, layer, tm, tf):
    wg, wu, _ = weights
    d, d_ff = wg.shape[1:]
    n_tok = n.shape[0]
    w_spec = pl.BlockSpec((None, d, tf), lambda i, j: (layer, 0, j))
    return pl.pallas_call(
        _ffn_up_kernel,
        out_shape=jax.ShapeDtypeStruct((n_tok, d_ff), BF16),
        grid=(n_tok // tm, d_ff // tf),
        in_specs=[pl.BlockSpec((tm, d), lambda i, j: (i, 0)), w_spec, w_spec],
        out_specs=pl.BlockSpec((tm, tf), lambda i, j: (i, j)),
        compiler_params=_cparams(("parallel", "arbitrary")),
        name="ffn_up",
    )(n, wg, wu)


def _ffn_up_first(xp, xs, gain1, weights, layer, tm, tf):
    wg, wu, _ = weights
    d, d_ff = wg.shape[1:]
    n_prompt, n_tok = xp.shape[0], xp.shape[0] + xs.shape[0]
    assert n_prompt % tm == 0
    pt = n_prompt // tm
    w_spec = pl.BlockSpec((None, d, tf), lambda i, j: (layer, 0, j))
    tok_spec = pl.BlockSpec((tm, d), lambda i, j: (i, 0))
    return pl.pallas_call(
        functools.partial(_ffn_up_first_kernel, prompt_tiles=pt),
        out_shape=(jax.ShapeDtypeStruct((n_tok, d_ff), BF16), jax.ShapeDtypeStruct((n_tok, d), F32)),
        grid=(n_tok // tm, d_ff // tf),
        in_specs=[pl.BlockSpec((tm, d), lambda i, j: (jnp.minimum(i, pt - 1), 0)),
                  pl.BlockSpec((tm, d), lambda i, j: (jnp.maximum(i - pt, 0), 0)),
                  pl.BlockSpec((1, d), lambda i, j: (0, 0)), w_spec, w_spec],
        out_specs=(pl.BlockSpec((tm, tf), lambda i, j: (i, j)), tok_spec),
        scratch_shapes=[pltpu.VMEM((tm, d), BF16)],
        compiler_params=_cparams(("arbitrary", "arbitrary")),
        name="ffn_up_first",
    )(xp, xs, gain1.reshape(1, d), wg, wu)


def _ffn_down(a, x, weights, layer, gain2, tm, n_prompt=None):
    wd = weights[2]
    d_ff, d = wd.shape[1:]
    n_tok = x.shape[0]
    tok = lambda width: pl.BlockSpec((tm, width), lambda i: (i, 0))
    in_specs = [tok(d_ff), pl.BlockSpec((None, d_ff, d), lambda i: (layer, 0, 0), pipeline_mode=pl.Buffered(1)),
                tok(d), pl.BlockSpec((1, d), lambda i: (0, 0))]
    if n_prompt is None:
        body = _ffn_down_kernel
        out_shape = (jax.ShapeDtypeStruct((n_tok, d), F32), jax.ShapeDtypeStruct((n_tok, d), BF16))
        out_specs = (tok(d), tok(d))
    else:
        assert n_prompt % tm == 0
        pt = n_prompt // tm
        body = functools.partial(_ffn_down_last_kernel, prompt_tiles=pt)
        out_shape = (jax.ShapeDtypeStruct((n_prompt, d), F32), jax.ShapeDtypeStruct((n_tok - n_prompt, d), F32))
        out_specs = (pl.BlockSpec((tm, d), lambda i: (jnp.minimum(i, pt - 1), 0)),
                     pl.BlockSpec((tm, d), lambda i: (jnp.maximum(i - pt, 0), 0)))
    return pl.pallas_call(
        body,
        out_shape=out_shape,
        grid=(n_tok // tm,),
        in_specs=in_specs,
        out_specs=out_specs,
        compiler_params=_cparams(("arbitrary",)),
        name="ffn_down",
    )(a, wd, x, gain2.reshape(1, d))


def _matmul_kernel(a_ref, w_ref, o_ref):
    o_ref[...] = jnp.dot(a_ref[...], w_ref[...], preferred_element_type=F32).astype(o_ref.dtype)


def _matmul(a, w, layer, out_dtype, tm, tn, rotate):
    n_tok, k = a.shape
    n_out = w.shape[-1]
    n_col = n_out // tn
    return pl.pallas_call(
        _matmul_kernel,
        out_shape=jax.ShapeDtypeStruct((n_tok, n_out), out_dtype),
        grid=(n_tok // tm, n_out // tn),
        in_specs=[pl.BlockSpec((tm, k), lambda i, j: (i, 0)),
                  pl.BlockSpec((None, k, tn), lambda i, j: (layer, 0, (j + rotate) % n_col))],
        out_specs=pl.BlockSpec((tm, tn), lambda i, j: (i, j)),
        compiler_params=_cparams(("parallel", "arbitrary")),
        name="proj_in",
    )(a, w)


def _wattn_phases(sink_ref, q_ref, kp_ref, kc_ref, kn_ref, vp_ref, vc_ref, vn_ref, bias_ref, o_ref, s_ref, e_ref, segs):
    local, n_loc = _seq_pos(pl.program_id(0), WIN_STEP, segs)
    variant = (jnp.where(local == 0, 0, 1), jnp.where(local == n_loc - 1, 2, 1))
    group = A_HEADS // A_KV_HEADS
    gw = group * WIN_BLOCK
    scale = HEAD_DIM ** -0.5 * LOG2E
    units = [(blk, kv) for blk in range(WIN_STEP // WIN_BLOCK) for kv in range(A_KV_HEADS)]

    def keys_of(p_ref, c_ref, n_ref, blk, kv):
        cs = slice(kv * HEAD_DIM, (kv + 1) * HEAD_DIM)
        cat = jnp.concatenate([p_ref[:, cs], c_ref[:, cs], n_ref[:, cs]], axis=0)
        return cat[blk * WIN_BLOCK:(blk + 3) * WIN_BLOCK]

    def scores():
        for u, (blk, kv) in enumerate(units):
            rows = slice(blk * WIN_BLOCK, (blk + 1) * WIN_BLOCK)
            qs = jnp.concatenate([q_ref[rows, h * HEAD_DIM:(h + 1) * HEAD_DIM]
                                  for h in range(kv * group, (kv + 1) * group)], axis=0)
            s = lax.dot_general(qs, keys_of(kp_ref, kc_ref, kn_ref, blk, kv), (((1,), (1,)), ((), ())),
                                preferred_element_type=F32)
            s_ref[u] = s * scale + bias_ref[variant[blk], kv * gw:(kv + 1) * gw, :]

    def outputs():
        for u, (blk, kv) in enumerate(units):
            rows = slice(blk * WIN_BLOCK, (blk + 1) * WIN_BLOCK)
            dens = []
            for i in range(group):
                hr = slice(i * WIN_BLOCK, (i + 1) * WIN_BLOCK)
                sh = s_ref[u, hr, :]
                sink = sink_ref[kv * group + i]
                m = jnp.maximum(jnp.max(sh, axis=-1, keepdims=True), sink)
                e = jnp.exp2(sh - m)
                dens.append(jnp.sum(e, axis=-1, keepdims=True) + jnp.exp2(sink - m))
                e_ref[u, hr, :] = e.astype(BF16)
            o4 = jnp.dot(e_ref[u], keys_of(vp_ref, vc_ref, vn_ref, blk, kv), preferred_element_type=F32)
            for i in range(group):
                h = kv * group + i
                o_ref[rows, h * HEAD_DIM:(h + 1) * HEAD_DIM] = (
                    o4[i * WIN_BLOCK:(i + 1) * WIN_BLOCK] / dens[i]).astype(o_ref.dtype)

    return scores, outputs


def _window_specs(n_tok, bias_tab):
    nb = n_tok // WIN_BLOCK
    per_step = WIN_STEP // WIN_BLOCK
    kv_w = A_KV_HEADS * HEAD_DIM
    n_units = per_step * A_KV_HEADS
    gw = (A_HEADS // A_KV_HEADS) * WIN_BLOCK

    def kv_specs(col0):
        col = col0 // A_KV_HEADS
        return [pl.BlockSpec((WIN_BLOCK, kv_w), lambda s: (jnp.maximum(per_step * s - 1, 0), col)),
                pl.BlockSpec((WIN_STEP, kv_w), lambda s: (s, col)),
                pl.BlockSpec((WIN_BLOCK, kv_w), lambda s: (jnp.minimum(per_step * (s + 1), nb - 1), col))]

    in_specs = [pl.BlockSpec(memory_space=pltpu.SMEM),
                pl.BlockSpec((WIN_STEP, A_HEADS * HEAD_DIM), lambda s: (s, COL_A_Q // A_HEADS)),
                *kv_specs(COL_A_K), *kv_specs(COL_A_V),
                pl.BlockSpec(bias_tab.shape, lambda s: (0, 0, 0))]
    out_spec = pl.BlockSpec((WIN_STEP, A_HEADS * HEAD_DIM), lambda s: (s, 0))
    scratch = [pltpu.VMEM((n_units, gw, 3 * WIN_BLOCK), F32), pltpu.VMEM((n_units, gw, 3 * WIN_BLOCK), BF16)]
    return in_specs, out_spec, scratch


def _t5_bucket(rel):
    nb = T5_BUCKETS // 2
    max_exact = nb // 2
    base = jnp.where(rel > 0, nb, 0)
    n = jnp.abs(rel)
    large = max_exact + (jnp.log(jnp.maximum(n, 1).astype(F32) / max_exact)
                         / math.log(T5_MAX_DIST / max_exact) * (nb - max_exact)).astype(jnp.int32)
    large = jnp.minimum(large, nb - 1)
    return base + jnp.where(n < max_exact, n, large)


def _window_bias_table(t5_bias):
    qi = jnp.arange(WIN_BLOCK)[:, None]
    si = jnp.arange(3 * WIN_BLOCK)[None, :]
    rel = si - WIN_BLOCK - qi
    onehot = (_t5_bucket(rel)[None] == jnp.arange(T5_BUCKETS)[:, None, None]).astype(F32)
    bias = jnp.einsum('bh,bqs->hqs', t5_bias.astype(F32), onehot, precision=lax.Precision.HIGHEST)
    band = jnp.abs(rel) <= WINDOW
    variants = []
    for lo_ok, hi_ok in ((False, True), (True, True), (True, False)):
        ok = band & ((si >= WIN_BLOCK) | lo_ok) & ((si < 2 * WIN_BLOCK) | hi_ok)
        variants.append(jnp.where(ok[None], bias * LOG2E, NEG_BIG).reshape(A_HEADS * WIN_BLOCK, 3 * WIN_BLOCK))
    return jnp.stack(variants)


def _na_window(g, segs):
    lg, ng = _seq_pos(g, NA_GROUP, segs)
    return lg, ng, jnp.clip(lg - 1, 0, ng - 3)


def _na_phases(q_ref, k0_ref, k1_ref, k2_ref, v0_ref, v1_ref, v2_ref, bias_ref, o_ref, s_ref):
    scale = HEAD_DIM ** -0.5 * LOG2E
    heads = [slice(h * HEAD_DIM, (h + 1) * HEAD_DIM) for h in range(B_HEADS)]

    def scores():
        for h, cs in enumerate(heads):
            kcat = jnp.concatenate([k0_ref[:, cs], k1_ref[:, cs], k2_ref[:, cs]], axis=0)
            s = lax.dot_general(q_ref[:, cs], kcat, (((1,), (1,)), ((), ())), preferred_element_type=F32)
            s_ref[h] = s * scale + bias_ref[h]

    def outputs():
        for h, cs in enumerate(heads):
            vcat = jnp.concatenate([v0_ref[:, cs], v1_ref[:, cs], v2_ref[:, cs]], axis=0)
            s = s_ref[h]
            m = jnp.max(s, axis=-1, keepdims=True)
            e = jnp.exp2(s - m)
            den = jnp.sum(e, axis=-1, keepdims=True)
            oh = jnp.dot(e.astype(BF16), vcat, preferred_element_type=F32)
            o_ref[:, cs] = (oh / den).astype(o_ref.dtype)

    return scores, outputs


def _na_specs(segs):
    width = B_HEADS * HEAD_DIM

    def kv_spec(col0, j):
        def index_map(g):
            lg, _, lo = _na_window(g, segs)
            return (g - lg + lo + j, col0 // B_HEADS)
        return pl.BlockSpec((NA_GROUP, width), index_map)

    def bias_map(g):
        lg, ng, _ = _na_window(g, segs)
        return (jnp.where(lg == 0, 0, jnp.where(lg == ng - 1, 2, 1)), 0, 0, 0)

    in_specs = [pl.BlockSpec((NA_GROUP, width), lambda g: (g, COL_B_Q // B_HEADS)),
                kv_spec(COL_B_K, 0), kv_spec(COL_B_K, 1), kv_spec(COL_B_K, 2),
                kv_spec(COL_B_V, 0), kv_spec(COL_B_V, 1), kv_spec(COL_B_V, 2),
                pl.BlockSpec((None, B_HEADS, NA_GROUP, 3 * NA_GROUP), bias_map)]
    out_spec = pl.BlockSpec((NA_GROUP, width), lambda g: (g, 0))
    scratch = [pltpu.VMEM((B_HEADS, NA_GROUP, 3 * NA_GROUP), F32)]
    return in_specs, out_spec, scratch


N_WIN_IN, N_NA_IN = 9, 8


def _attention_kernel(*refs, segs):
    win_in, na_in = refs[:N_WIN_IN], refs[N_WIN_IN:N_WIN_IN + N_NA_IN]
    ya_ref, yb_ref, ws_ref, we_ref, ns_ref = refs[N_WIN_IN + N_NA_IN:]
    win_scores, win_outputs = _wattn_phases(*win_in, ya_ref, ws_ref, we_ref, segs)
    na_scores, na_outputs = _na_phases(*na_in, yb_ref, ns_ref)
    win_scores()
    na_scores()
    win_outputs()
    na_outputs()


def _attention_mixers(proj, sink, win_tab, na_tab, segs):
    assert WIN_STEP == NA_GROUP
    n_tok = proj.shape[0]
    win_specs, win_out, win_scratch = _window_specs(n_tok, win_tab)
    na_specs, na_out, na_scratch = _na_specs(segs)
    return pl.pallas_call(
        functools.partial(_attention_kernel, segs=segs),
        out_shape=(jax.ShapeDtypeStruct((n_tok, A_HEADS * HEAD_DIM), BF16),
                   jax.ShapeDtypeStruct((n_tok, B_HEADS * HEAD_DIM), BF16)),
        grid=(n_tok // WIN_STEP,),
        in_specs=[*win_specs, *na_specs],
        out_specs=(win_out, na_out),
        scratch_shapes=[*win_scratch, *na_scratch],
        compiler_params=_cparams(("parallel",)),
        name="attn_ab",
    )(sink, *([proj] * 7), win_tab, *([proj] * 7), na_tab)


def _na_bias_table(rel_table):
    rows_q = NA_GROUP // GRID_W
    rows_k = 3 * rows_q
    c = np.arange(GRID_W)
    col_start = np.clip(c - NA_COLS // 2, 0, GRID_W - NA_COLS)
    col_ok = (c[None, :] >= col_start[:, None]) & (c[None, :] < col_start[:, None] + NA_COLS)
    dc = np.clip(c[None, :] - c[:, None], -(NA_COLS - 1), NA_COLS - 1) + NA_COLS - 1
    onehot = (dc[None] == np.arange(2 * NA_COLS - 1)[:, None, None]).astype(np.float32)
    by_dr = jnp.einsum('hrd,dqk->hrqk', rel_table.astype(F32), jnp.asarray(onehot),
                       precision=lax.Precision.HIGHEST)
    by_dr = jnp.where(jnp.asarray(col_ok)[None, None], by_dr * LOG2E, NEG_BIG)
    by_dr = jnp.concatenate([by_dr, jnp.full_like(by_dr[:, :1], NEG_BIG)], axis=1)
    cfgs = ([(0, NA_ROWS - 1 - a) for a in range(rows_q)],
            [(a, NA_ROWS // 2 - 1) for a in range(rows_q)],
            [(rows_q, NA_ROWS // 2 - 1 - a) for a in range(rows_q)])
    idx = np.full((3, rows_q, rows_k), 2 * NA_ROWS - 1, np.int32)
    for ci, cfg in enumerate(cfgs):
        for a, (off, dr0) in enumerate(cfg):
            for j in range(NA_ROWS):
                idx[ci, a, off + j] = dr0 + j
    tab = jnp.take(by_dr, jnp.asarray(idx.reshape(-1)), axis=1)
    tab = tab.reshape(B_HEADS, 3, rows_q, rows_k, GRID_W, GRID_W).transpose(1, 0, 2, 4, 3, 5)
    return tab.reshape(3, B_HEADS, NA_GROUP, 3 * NA_GROUP)


def _split3(x):
    hi = x.astype(BF16)
    r1 = x - hi.astype(F32)
    mid = r1.astype(BF16)
    lo = (r1 - mid.astype(F32)).astype(BF16)
    return hi, mid, lo


def _hgrn_kernel(z_ref, v_ref, q_ref, lb_ref, scan_ref, o_ref, st_ref, c3_ref, b3_ref, k3_ref, q3_ref, a_ref,
                 *, segs, reverse):
    i = pl.program_id(0)
    n_steps = pl.num_programs(0)
    n_sub = z_ref.shape[0] // HG_TILE
    step = (n_steps - 1 - i) if reverse else i
    local, n_loc = _seq_pos(step, n_sub * HG_TILE, segs)
    is_start = (local == n_loc - 1) if reverse else (local == 0)

    @pl.when(is_start)
    def _():
        st_ref[...] = jnp.zeros_like(st_ref)

    nblk = HG_TILE // HG_BLK
    half = HG_BLK // 2
    n_sc = nblk // 2
    log2e = math.log2(math.e)
    nt = (((1,), (1,)), ((), ()))

    def sel_index(t0):
        shape = (nblk, half, HG_TILE)
        t = lax.broadcasted_iota(jnp.int32, shape, 1) + t0
        j = lax.broadcasted_iota(jnp.int32, shape, 2) - lax.broadcasted_iota(jnp.int32, shape, 0) * HG_BLK
        ok = (j >= 0) & (j < HG_BLK) & ((j >= t) if reverse else (j <= t))
        return jnp.where(ok, j, -1)

    sel = (sel_index(0), sel_index(half))
    rb = lax.broadcasted_iota(jnp.int32, (HG_TILE, HG_TILE), 0) // HG_BLK
    cb = lax.broadcasted_iota(jnp.int32, (HG_TILE, HG_TILE), 1) // HG_BLK
    cross = ((rb % 2 == 0) & (cb == rb + 1)) if reverse else ((rb % 2 == 1) & (cb == rb - 1))
    blk_odd = (lax.broadcasted_iota(jnp.int32, (nblk, 1, HEAD_DIM), 0) % 2) == 1
    edge_row = 0 if reverse else HG_BLK - 1
    scan = scan_ref[...]

    def one_tile(t, carry):
        tile = (n_sub - 1 - t) if reverse else t
        tok = pl.ds(pl.multiple_of(tile * HG_TILE, HG_TILE), HG_TILE)
        heads = [slice(h * HEAD_DIM, (h + 1) * HEAD_DIM) for h in range(C_HEADS)]

        for h, cs in enumerate(heads):
            lb = lb_ref[:, cs]
            f = lb + (1.0 - lb) * jax.nn.sigmoid(z_ref[tok, cs].astype(F32))
            k3 = (1.0 - f).reshape(nblk, HG_BLK, HEAD_DIM)
            pieces = jnp.concatenate(_split3(jnp.log(f)), axis=1)
            bm = jnp.dot(scan, pieces, preferred_element_type=F32)
            b = bm[:, :HEAD_DIM] + bm[:, HEAD_DIM:2 * HEAD_DIM] + bm[:, 2 * HEAD_DIM:]
            b3 = (b * log2e).reshape(nblk, HG_BLK, HEAD_DIM)
            k3_ref[h] = k3
            b3_ref[h] = b3
            c3_ref[h] = b3 - jnp.log2(jnp.maximum(k3, 0.0))
            q3_ref[h] = (q_ref[tok, cs].astype(F32) * (HEAD_DIM ** -0.5)).reshape(nblk, HG_BLK, HEAD_DIM)

        for h in range(C_HEADS):
            halves = ((q3_ref[h, :, :half, :], b3_ref[h, :, :half, :]), (q3_ref[h, :, half:, :], b3_ref[h, :, half:, :]))
            a_half = [jnp.zeros((nblk, half, HG_TILE), F32), jnp.zeros((nblk, half, HG_TILE), F32)]
            for j in range(HG_BLK):
                cj = c3_ref[h, :, j:j + 1, :]
                for hi in range(2):
                    if (j < half * hi) if reverse else (j > half * hi + half - 1):
                        continue
                    qx, bx = halves[hi]
                    p = qx * jnp.exp2(bx - cj)
                    r = jnp.sum(p, axis=-1, keepdims=True)
                    a_half[hi] = jnp.where(sel[hi] == j, r, a_half[hi])
            a_ref[h] = jnp.concatenate(a_half, axis=1).reshape(HG_TILE, HG_TILE)

        for h, cs in enumerate(heads):
            q3, k3, b3 = q3_ref[h], k3_ref[h], b3_ref[h]
            btot = b3_ref[h, :, edge_row:edge_row + 1, :]
            v = v_ref[tok, cs]
            qp3 = q3 * jnp.exp2(b3)
            kp3 = k3 * jnp.exp2(btot - b3)
            qp = qp3.reshape(HG_TILE, HEAD_DIM).astype(BF16)
            kp = kp3.reshape(HG_TILE, HEAD_DIM).astype(BF16)
            g = lax.dot_general(qp, kp, nt, preferred_element_type=F32)
            a_all = jnp.where(cross, g, a_ref[h]).astype(BF16)
            o_acc = jnp.dot(a_all, v, preferred_element_type=F32)

            dec = jnp.exp2(btot)
            one = jnp.ones_like(dec[:1])
            dprev = jnp.concatenate([one, dec[:-1]], axis=0)
            dnext = jnp.concatenate([dec[1:], one], axis=0)
            if reverse:
                qscale, kscale = jnp.where(blk_odd, 1.0, dnext), jnp.where(blk_odd, dprev, 1.0)
            else:
                qscale, kscale = jnp.where(blk_odd, dprev, 1.0), jnp.where(blk_odd, 1.0, dnext)
            qpp = (qp3 * qscale).reshape(HG_TILE, HEAD_DIM).astype(BF16)
            kpp = (kp3 * kscale).reshape(HG_TILE, HEAD_DIM).astype(BF16)
            st = st_ref[h]
            inter = [None] * n_sc
            for m in (range(n_sc - 1, -1, -1) if reverse else range(n_sc)):
                rows = slice(2 * m * HG_BLK, 2 * (m + 1) * HG_BLK)
                inter[m] = lax.dot_general(qpp[rows], st.astype(BF16), nt, preferred_element_type=F32)
                upd = lax.dot_general(v[rows], kpp[rows], (((0,), (0,)), ((), ())), preferred_element_type=F32)
                st = st * (dec[2 * m] * dec[2 * m + 1]) + upd
            st_ref[h] = st
            o_ref[tok, cs] = o_acc + jnp.concatenate(inter, axis=0)
        return carry

    lax.fori_loop(0, n_sub, one_tile, 0)


def _hgrn_scan_matrix(reverse):
    t = np.arange(HG_TILE)
    same = (t[:, None] // HG_BLK) == (t[None, :] // HG_BLK)
    incl = (t[None, :] >= t[:, None]) if reverse else (t[None, :] <= t[:, None])
    return jnp.asarray((same & incl).astype(np.float32), BF16)


def _hgrn_direction(proj, lower_bound, segs, reverse):
    n_tok = proj.shape[0]
    step_tok = next(s for s in (8 * HG_TILE, 4 * HG_TILE, 2 * HG_TILE, HG_TILE) if all(t % s == 0 for _, t in segs))
    n_steps = n_tok // step_tok
    width = C_HEADS * HEAD_DIM
    scan = _hgrn_scan_matrix(reverse)

    def tok_spec(col0):
        return pl.BlockSpec((step_tok, width),
                            lambda i: ((n_steps - 1 - i) if reverse else i, col0 // C_HEADS))

    nblk = HG_TILE // HG_BLK
    return pl.pallas_call(
        functools.partial(_hgrn_kernel, segs=segs, reverse=reverse),
        out_shape=jax.ShapeDtypeStruct((n_tok, width), F32),
        grid=(n_steps,),
        in_specs=[
            tok_spec(COL_C_FB if reverse else COL_C_FF), tok_spec(COL_C_I), tok_spec(COL_C_Q),
            pl.BlockSpec((1, width), lambda i: (0, 0)),
            pl.BlockSpec(scan.shape, lambda i: (0, 0)),
        ],
        out_specs=pl.BlockSpec((step_tok, width), lambda i: ((n_steps - 1 - i) if reverse else i, 0)),
        scratch_shapes=[pltpu.VMEM((C_HEADS, HEAD_DIM, HEAD_DIM), F32)]
        + [pltpu.VMEM((C_HEADS, nblk, HG_BLK, HEAD_DIM), F32)] * 4
        + [pltpu.VMEM((C_HEADS, HG_TILE, HG_TILE), F32)],
        compiler_params=_cparams(("arbitrary",)),
        name="hgrn_bwd" if reverse else "hgrn_fwd",
    )(proj, proj, proj, lower_bound.reshape(1, width), scan)


def _merge_kernel(ga_ref, gb_ref, gc_ref, cg_ref, ya_ref, yb_ref, of_ref, ob_ref, hn_ref,
                  wa_ref, wb_ref, wc_ref, wo_ref, h_ref, g2_ref, o_ref, o2_ref):
    o = of_ref[...] + ob_ref[...]
    heads = []
    for h in range(C_HEADS):
        oh = o[:, h * HEAD_DIM:(h + 1) * HEAD_DIM]
        ms = jnp.mean(oh * oh, axis=-1, keepdims=True)
        heads.append(oh * lax.rsqrt(ms + EPS))
    cg = cg_ref[...].astype(F32)
    yc = (jnp.concatenate(heads, axis=1) * hn_ref[...] * (cg * jax.nn.sigmoid(cg))).astype(BF16)
    m = jax.nn.sigmoid(ga_ref[...].astype(F32)) * jnp.dot(ya_ref[...], wa_ref[...], preferred_element_type=F32)
    m += jax.nn.sigmoid(gb_ref[...].astype(F32)) * jnp.dot(yb_ref[...], wb_ref[...], preferred_element_type=F32)
    m += jax.nn.sigmoid(gc_ref[...].astype(F32)) * jnp.dot(yc, wc_ref[...], preferred_element_type=F32)
    y = h_ref[...] + jnp.dot(m.astype(BF16), wo_ref[...], preferred_element_type=F32)
    o_ref[...] = y
    ms = jnp.mean(y * y, axis=-1, keepdims=True)
    o2_ref[...] = (y * lax.rsqrt(ms + EPS) * g2_ref[...]).astype(o2_ref.dtype)


def _merge(proj, ya, yb, o_f, o_b, hgrn_gain, wa, wb, wc, wo, layer, h, gain2, tm):
    n_tok, d = h.shape
    cw = C_HEADS * HEAD_DIM

    def resident(w):
        return pl.BlockSpec((None,) + w.shape[1:], lambda i: (layer, 0, 0), pipeline_mode=pl.Buffered(1))

    def gate_spec(which):
        return pl.BlockSpec((tm, d), lambda i: (i, which))

    return pl.pallas_call(
        _merge_kernel,
        out_shape=(jax.ShapeDtypeStruct((n_tok, d), F32), jax.ShapeDtypeStruct((n_tok, d), BF16)),
        grid=(n_tok // tm,),
        in_specs=[
            gate_spec(0), gate_spec(1), gate_spec(2),
            pl.BlockSpec((tm, cw), lambda i: (i, COL_C_G // C_HEADS)),
            pl.BlockSpec((tm, ya.shape[1]), lambda i: (i, 0)),
            pl.BlockSpec((tm, cw), lambda i: (i, 0)),
            pl.BlockSpec((tm, cw), lambda i: (i, 0)),
            pl.BlockSpec((tm, cw), lambda i: (i, 0)),
            pl.BlockSpec((1, cw), lambda i: (0, 0)),
            resident(wa), resident(wb), resident(wc), resident(wo),
            pl.BlockSpec((tm, d), lambda i: (i, 0)),
            pl.BlockSpec((1, d), lambda i: (0, 0)),
        ],
        out_specs=(pl.BlockSpec((tm, d), lambda i: (i, 0)), pl.BlockSpec((tm, d), lambda i: (i, 0))),
        compiler_params=_cparams(("parallel",)),
        name="merge_out",
    )(proj, proj, proj, proj, ya, yb, o_f, o_b, hgrn_gain.reshape(1, cw), wa, wb, wc, wo, h,
      gain2.reshape(1, d))


def _tile(n, candidates):
    return next(t for t in candidates if n % t == 0)


def _trunk(xp, xs, segs, p):
    depth = p["w_in"].shape[0]
    n_prompt, n_tok = xp.shape[0], xp.shape[0] + xs.shape[0]
    tm_up = _tile(n_tok, (3072, 2048, 1024, 512, 256, 128))
    tm_up_first = next(t for t in (512, 256, 128) if n_prompt % t == 0 and n_tok % t == 0)
    tm_down = _tile(n_tok, (512, 384, 256, 128))
    tm_down_last = next(t for t in (256, 128) if n_prompt % t == 0 and n_tok % t == 0)
    tm_proj = _tile(n_tok, (4096, 2048, 1024, 512, 256, 128))
    tm_merge = _tile(n_tok, (384, 256, 128))
    d_ff = p["ffn1_w_gate"].shape[-1]
    tf = 512 if d_ff % 512 == 0 else 256
    mix_w = MIX_COLS * HEAD_DIM

    lb_p = jax.nn.softmax(p["hgrn_lb_logits"].astype(F32), axis=1)
    lower_bounds = jnp.cumsum(lb_p, axis=1) - lb_p[:, :1]
    win_tab = _window_bias_table(p["t5_bias"])
    bf = lambda name: p[name].astype(BF16)
    w = {name: bf(name) for name in ("ffn1_w_gate", "ffn1_w_up", "ffn1_w_down", "w_in", "w_branch_a",
                                      "w_branch_b", "w_branch_c", "w_out", "ffn2_w_gate", "ffn2_w_up",
                                      "ffn2_w_down")}
    ffn1_w = (w["ffn1_w_gate"], w["ffn1_w_up"], w["ffn1_w_down"] * 0.5)
    ffn2_w = (w["ffn2_w_gate"], w["ffn2_w_up"], w["ffn2_w_down"] * 0.5)
    x = n = None
    for l in range(depth):
        if l == 0:
            a, x = _ffn_up_first(xp, xs, p["ffn1_norm"][l], ffn1_w, l, tm_up_first, tf)
        else:
            a = _ffn_up(n, ffn1_w, l, tm_up, tf)
        h, u = _ffn_down(a, x, ffn1_w, l, p["mix_norm"][l], tm_down)
        proj = _matmul(u, w["w_in"], l, BF16, tm_proj, 512, mix_w // 512)
        ya, yb = _attention_mixers(proj, p["attn_sink"][l].astype(F32) * LOG2E, win_tab,
                                   _na_bias_table(p["na_bias"][l]), segs)
        o_f = _hgrn_direction(proj, lower_bounds[0, l], segs, reverse=False)
        o_b = _hgrn_direction(proj, lower_bounds[1, l], segs, reverse=True)
        h, n = _merge(proj, ya, yb, o_f, o_b, p["hgrn_norm"][l], w["w_branch_a"], w["w_branch_b"],
                      w["w_branch_c"], w["w_out"], l, h, p["ffn2_norm"][l], tm_merge)
        a = _ffn_up(n, ffn2_w, l, tm_up, tf)
        if l == depth - 1:
            return _ffn_down(a, h, ffn2_w, l, p["final_norm"], tm_down_last, n_prompt)
        x, n = _ffn_down(a, h, ffn2_w, l, p["ffn1_norm"][l + 1], tm_down)


def kernel(x_prompt, x_sample, ffn1_norm, ffn1_w_gate, ffn1_w_up, ffn1_w_down, mix_norm, w_in, attn_sink,
           t5_bias, na_bias, hgrn_lb_logits, hgrn_norm, w_branch_a, w_branch_b, w_branch_c, w_out,
           ffn2_norm, ffn2_w_gate, ffn2_w_up, ffn2_w_down, final_norm):
    params = dict(ffn1_norm=ffn1_norm, ffn1_w_gate=ffn1_w_gate, ffn1_w_up=ffn1_w_up, ffn1_w_down=ffn1_w_down,
                  mix_norm=mix_norm, w_in=w_in, attn_sink=attn_sink, t5_bias=t5_bias, na_bias=na_bias,
                  hgrn_lb_logits=hgrn_lb_logits, hgrn_norm=hgrn_norm, w_branch_a=w_branch_a,
                  w_branch_b=w_branch_b, w_branch_c=w_branch_c, w_out=w_out, ffn2_norm=ffn2_norm,
                  ffn2_w_gate=ffn2_w_gate, ffn2_w_up=ffn2_w_up, ffn2_w_down=ffn2_w_down, final_norm=final_norm)
    d = x_prompt.shape[-1]
    segs = (x_prompt.shape[:2], x_sample.shape[:2])
    for _, t in segs:
        assert t % NA_GROUP == 0 and t // NA_GROUP >= 3 and t // GRID_W >= NA_ROWS and t % WIN_STEP == 0
    y_prompt, y_sample = _trunk(x_prompt.reshape(-1, d), x_sample.reshape(-1, d), segs, params)
    return y_prompt.reshape(x_prompt.shape), y_sample.reshape(x_sample.shape)
```

```python
import functools
import math

import numpy as np
import jax
import jax.numpy as jnp
from jax import lax
from jax.experimental import pallas as pl
from jax.experimental.pallas import tpu as pltpu

F32 = jnp.float32
BF16 = jnp.bfloat16

HEAD_DIM = 128
EPS = 1e-6
A_HEADS = 8
A_KV_HEADS = 2
WINDOW = 128
WIN_BLOCK = 128
T5_BUCKETS = 32
T5_MAX_DIST = 128
B_HEADS = 4
GRID_W = 64
NA_ROWS = 8
NA_COLS = 16
C_HEADS = 4
GATE_COLS = 48
MIX_COLS = 44
COL_A_Q, COL_A_K, COL_A_V = (GATE_COLS + c for c in (0, 8, 10))
COL_B_Q, COL_B_K, COL_B_V = (GATE_COLS + c for c in (12, 16, 20))
COL_C_FF, COL_C_FB, COL_C_I, COL_C_Q, COL_C_G = (GATE_COLS + c for c in (24, 28, 32, 36, 40))

NEG_BIG = -1e30
LOG2E = math.log2(math.e)
V7X_VMEM_LIMIT = 60 * 1024 * 1024
HG_TILE = 128
HG_BLK = 16
NA_GROUP = 4 * GRID_W
WIN_STEP = 2 * WIN_BLOCK


def _cparams(sem):
    return pltpu.CompilerParams(dimension_semantics=sem, vmem_limit_bytes=V7X_VMEM_LIMIT)


def _seq_pos(gb, blk, segs):
    (n_seq0, t0), (_, t1) = segs
    n0, n1 = t0 // blk, t1 // blk
    tot0 = n_seq0 * n0
    in0 = gb < tot0
    local = jnp.where(in0, gb % n0, (gb - tot0) % n1)
    return local, jnp.where(in0, n0, n1)


def _normed(y, gain_ref, dtype):
    ms = jnp.mean(y * y, axis=-1, keepdims=True)
    return (y * lax.rsqrt(ms + EPS) * gain_ref[...]).astype(dtype)


def _ffn_gate(n, wg_ref, wu_ref, a_ref):
    g = jnp.dot(n, wg_ref[...], preferred_element_type=F32)
    u = jnp.dot(n, wu_ref[...], preferred_element_type=F32)
    a_ref[...] = (g * jax.nn.sigmoid(g) * u).astype(a_ref.dtype)


def _ffn_up_kernel(n_ref, wg_ref, wu_ref, a_ref):
    _ffn_gate(n_ref[...], wg_ref, wu_ref, a_ref)


def _join_norm_kernel(xp_ref, xs_ref, g1_ref, x_ref, n_ref, *, prompt_tiles):
    x = jnp.where(pl.program_id(0) < prompt_tiles, xp_ref[...], xs_ref[...])
    x_ref[...] = x
    n_ref[...] = _normed(x, g1_ref, n_ref.dtype)


def _ffn_down_kernel(a_ref, wd_ref, x_ref, g2_ref, o_ref, o2_ref):
    y = x_ref[...] + jnp.dot(a_ref[...], wd_ref[...], preferred_element_type=F32)
    o_ref[...] = y
    o2_ref[...] = _normed(y, g2_ref, o2_ref.dtype)


def _ffn_down_last_kernel(a_ref, wd_ref, x_ref, g2_ref, yp_ref, ys_ref, *, prompt_tiles):
    i = pl.program_id(0)
    y = x_ref[...] + jnp.dot(a_ref[...], wd_ref[...], preferred_element_type=F32)

    @pl.when(i < prompt_tiles)
    def _():
        yp_ref[...] = _normed(y, g2_ref, yp_ref.dtype)

    @pl.when(i >= prompt_tiles)
    def _():
        ys_ref[...] = _normed(y, g2_ref, ys_ref.dtype)


def _ffn_up(n, weights, layer, tm, tf):
    wg, wu, _ = weights
    d, d_ff = wg.shape[1:]
    n_tok = n.shape[0]
    w_spec = pl.BlockSpec((None, d, tf), lambda i, j: (layer, 0, j))
    return pl.pallas_call(
        _ffn_up_kernel,
        out_shape=jax.ShapeDtypeStruct((n_tok, d_ff), BF16),
        grid=(n_tok // tm, d_ff // tf),
        in_specs=[pl.BlockSpec((tm, d), lambda i, j: (i, 0)), w_spec, w_spec],
        out_specs=pl.BlockSpec((tm, tf), lambda i, j: (i, j)),
        compiler_params=_cparams(("parallel", "arbitrary")),
        name="ffn_up",
    )(n, wg, wu)


def _join_norm(xp, xs, gain1, tm):
    d = xp.shape[1]
    n_prompt, n_tok = xp.shape[0], xp.shape[0] + xs.shape[0]
    assert n_prompt % tm == 0
    pt = n_prompt // tm
    tok_spec = pl.BlockSpec((tm, d), lambda i: (i, 0))
    return pl.pallas_call(
        functools.partial(_join_norm_kernel, prompt_tiles=pt),
        out_shape=(jax.ShapeDtypeStruct((n_tok, d), F32), jax.ShapeDtypeStruct((n_tok, d), BF16)),
        grid=(n_tok // tm,),
        in_specs=[pl.BlockSpec((tm, d), lambda i: (jnp.minimum(i, pt - 1), 0)),
                  pl.BlockSpec((tm, d), lambda i: (jnp.maximum(i - pt, 0), 0)),
                  pl.BlockSpec((1, d), lambda i: (0, 0))],
        out_specs=(tok_spec, tok_spec),
        compiler_params=_cparams(("arbitrary",)),
        name="join_norm",
    )(xp, xs, gain1.reshape(1, d))


def _ffn_down(a, x, weights, layer, gain2, tm, n_prompt=None):
    wd = weights[2]
    d_ff, d = wd.shape[1:]
    n_tok = x.shape[0]
    tok = lambda width: pl.BlockSpec((tm, width), lambda i: (i, 0))
    in_specs = [tok(d_ff), pl.BlockSpec((None, d_ff, d), lambda i: (layer, 0, 0), pipeline_mode=pl.Buffered(1)),
                tok(d), pl.BlockSpec((1, d), lambda i: (0, 0))]
    if n_prompt is None:
        body = _ffn_down_kernel
        out_shape = (jax.ShapeDtypeStruct((n_tok, d), F32), jax.ShapeDtypeStruct((n_tok, d), BF16))
        out_specs = (tok(d), tok(d))
    else:
        assert n_prompt % tm == 0
        pt = n_prompt // tm
        body = functools.partial(_ffn_down_last_kernel, prompt_tiles=pt)
        out_shape = (jax.ShapeDtypeStruct((n_prompt, d), F32), jax.ShapeDtypeStruct((n_tok - n_prompt, d), F32))
        out_specs = (pl.BlockSpec((tm, d), lambda i: (jnp.minimum(i, pt - 1), 0)),
                     pl.BlockSpec((tm, d), lambda i: (jnp.maximum(i - pt, 0), 0)))
    return pl.pallas_call(
        body,
        out_shape=out_shape,
        grid=(n_tok // tm,),
        in_specs=in_specs,
        out_specs=out_specs,
        compiler_params=_cparams(("arbitrary",)),
        name="ffn_down",
    )(a, wd, x, gain2.reshape(1, d))


def _matmul_kernel(a_ref, w_ref, o_ref):
    o_ref[...] = jnp.dot(a_ref[...], w_ref[...], preferred_element_type=F32).astype(o_ref.dtype)


def _matmul(a, w, layer, out_dtype, tm, tn, rotate):
    n_tok, k = a.shape
    n_out = w.shape[-1]
    n_col = n_out // tn
    return pl.pallas_call(
        _matmul_kernel,
        out_shape=jax.ShapeDtypeStruct((n_tok, n_out), out_dtype),
        grid=(n_tok // tm, n_out // tn),
        in_specs=[pl.BlockSpec((tm, k), lambda i, j: (i, 0)),
                  pl.BlockSpec((None, k, tn), lambda i, j: (layer, 0, (j + rotate) % n_col))],
        out_specs=pl.BlockSpec((tm, tn), lambda i, j: (i, j)),
        compiler_params=_cparams(("parallel", "arbitrary")),
        name="proj_in",
    )(a, w)


def _wattn_phases(sink_ref, q_ref, kp_ref, kc_ref, kn_ref, vp_ref, vc_ref, vn_ref, bias_ref, o_ref, s_ref, e_ref, segs):
    local, n_loc = _seq_pos(pl.program_id(0), WIN_STEP, segs)
    variant = (jnp.where(local == 0, 0, 1), jnp.where(local == n_loc - 1, 2, 1))
    group = A_HEADS // A_KV_HEADS
    gw = group * WIN_BLOCK
    scale = HEAD_DIM ** -0.5 * LOG2E
    units = [(blk, kv) for blk in range(WIN_STEP // WIN_BLOCK) for kv in range(A_KV_HEADS)]

    def keys_of(p_ref, c_ref, n_ref, blk, kv):
        cs = slice(kv * HEAD_DIM, (kv + 1) * HEAD_DIM)
        cat = jnp.concatenate([p_ref[:, cs], c_ref[:, cs], n_ref[:, cs]], axis=0)
        return cat[blk * WIN_BLOCK:(blk + 3) * WIN_BLOCK]

    def scores():
        for u, (blk, kv) in enumerate(units):
            rows = slice(blk * WIN_BLOCK, (blk + 1) * WIN_BLOCK)
            qs = jnp.concatenate([q_ref[rows, h * HEAD_DIM:(h + 1) * HEAD_DIM]
                                  for h in range(kv * group, (kv + 1) * group)], axis=0)
            s = lax.dot_general(qs, keys_of(kp_ref, kc_ref, kn_ref, blk, kv), (((1,), (1,)), ((), ())),
                                preferred_element_type=F32)
            s_ref[u] = s * scale + bias_ref[variant[blk], kv * gw:(kv + 1) * gw, :]

    def outputs():
        for u, (blk, kv) in enumerate(units):
            rows = slice(blk * WIN_BLOCK, (blk + 1) * WIN_BLOCK)
            dens = []
            for i in range(group):
                hr = slice(i * WIN_BLOCK, (i + 1) * WIN_BLOCK)
                sh = s_ref[u, hr, :]
                sink = sink_ref[kv * group + i]
                m = jnp.maximum(jnp.max(sh, axis=-1, keepdims=True), sink)
                e = jnp.exp2(sh - m)
                dens.append(jnp.sum(e, axis=-1, keepdims=True) + jnp.exp2(sink - m))
                e_ref[u, hr, :] = e.astype(BF16)
            o4 = jnp.dot(e_ref[u], keys_of(vp_ref, vc_ref, vn_ref, blk, kv), preferred_element_type=F32)
            for i in range(group):
                h = kv * group + i
                o_ref[rows, h * HEAD_DIM:(h + 1) * HEAD_DIM] = (
                    o4[i * WIN_BLOCK:(i + 1) * WIN_BLOCK] / dens[i]).astype(o_ref.dtype)

    return scores, outputs


def _window_specs(n_tok, bias_tab):
    nb = n_tok // WIN_BLOCK
    per_step = WIN_STEP // WIN_BLOCK
    kv_w = A_KV_HEADS * HEAD_DIM
    n_units = per_step * A_KV_HEADS
    gw = (A_HEADS // A_KV_HEADS) * WIN_BLOCK

    def kv_specs(col0):
        col = col0 // A_KV_HEADS
        return [pl.BlockSpec((WIN_BLOCK, kv_w), lambda s: (jnp.maximum(per_step * s - 1, 0), col)),
                pl.BlockSpec((WIN_STEP, kv_w), lambda s: (s, col)),
                pl.BlockSpec((WIN_BLOCK, kv_w), lambda s: (jnp.minimum(per_step * (s + 1), nb - 1), col))]

    in_specs = [pl.BlockSpec(memory_space=pltpu.SMEM),
                pl.BlockSpec((WIN_STEP, A_HEADS * HEAD_DIM), lambda s: (s, COL_A_Q // A_HEADS)),
                *kv_specs(COL_A_K), *kv_specs(COL_A_V),
                pl.BlockSpec(bias_tab.shape, lambda s: (0, 0, 0))]
    out_spec = pl.BlockSpec((WIN_STEP, A_HEADS * HEAD_DIM), lambda s: (s, 0))
    scratch = [pltpu.VMEM((n_units, gw, 3 * WIN_BLOCK), F32), pltpu.VMEM((n_units, gw, 3 * WIN_BLOCK), BF16)]
    return in_specs, out_spec, scratch


def _t5_bucket(rel):
    nb = T5_BUCKETS // 2
    max_exact = nb // 2
    base = jnp.where(rel > 0, nb, 0)
    n = jnp.abs(rel)
    large = max_exact + (jnp.log(jnp.maximum(n, 1).astype(F32) / max_exact)
                         / math.log(T5_MAX_DIST / max_exact) * (nb - max_exact)).astype(jnp.int32)
    large = jnp.minimum(large, nb - 1)
    return base + jnp.where(n < max_exact, n, large)


def _window_bias_table(t5_bias):
    qi = jnp.arange(WIN_BLOCK)[:, None]
    si = jnp.arange(3 * WIN_BLOCK)[None, :]
    rel = si - WIN_BLOCK - qi
    onehot = (_t5_bucket(rel)[None] == jnp.arange(T5_BUCKETS)[:, None, None]).astype(F32)
    bias = jnp.einsum('bh,bqs->hqs', t5_bias.astype(F32), onehot, precision=lax.Precision.HIGHEST)
    band = jnp.abs(rel) <= WINDOW
    variants = []
    for lo_ok, hi_ok in ((False, True), (True, True), (True, False)):
        ok = band & ((si >= WIN_BLOCK) | lo_ok) & ((si < 2 * WIN_BLOCK) | hi_ok)
        variants.append(jnp.where(ok[None], bias * LOG2E, NEG_BIG).reshape(A_HEADS * WIN_BLOCK, 3 * WIN_BLOCK))
    return jnp.stack(variants)


def _na_window(g, segs):
    lg, ng = _seq_pos(g, NA_GROUP, segs)
    return lg, ng, jnp.clip(lg - 1, 0, ng - 3)


def _na_phases(q_ref, k0_ref, k1_ref, k2_ref, v0_ref, v1_ref, v2_ref, bias_ref, o_ref, s_ref):
    scale = HEAD_DIM ** -0.5 * LOG2E
    heads = [slice(h * HEAD_DIM, (h + 1) * HEAD_DIM) for h in range(B_HEADS)]

    def scores():
        for h, cs in enumerate(heads):
            kcat = jnp.concatenate([k0_ref[:, cs], k1_ref[:, cs], k2_ref[:, cs]], axis=0)
            s = lax.dot_general(q_ref[:, cs], kcat, (((1,), (1,)), ((), ())), preferred_element_type=F32)
            s_ref[h] = s * scale + bias_ref[h]

    def outputs():
        for h, cs in enumerate(heads):
            vcat = jnp.concatenate([v0_ref[:, cs], v1_ref[:, cs], v2_ref[:, cs]], axis=0)
            s = s_ref[h]
            m = jnp.max(s, axis=-1, keepdims=True)
            e = jnp.exp2(s - m)
            den = jnp.sum(e, axis=-1, keepdims=True)
            oh = jnp.dot(e.astype(BF16), vcat, preferred_element_type=F32)
            o_ref[:, cs] = (oh / den).astype(o_ref.dtype)

    return scores, outputs


def _na_specs(segs):
    width = B_HEADS * HEAD_DIM

    def kv_spec(col0, j):
        def index_map(g):
            lg, _, lo = _na_window(g, segs)
            return (g - lg + lo + j, col0 // B_HEADS)
        return pl.BlockSpec((NA_GROUP, width), index_map)

    def bias_map(g):
        lg, ng, _ = _na_window(g, segs)
        return (jnp.where(lg == 0, 0, jnp.where(lg == ng - 1, 2, 1)), 0, 0, 0)

    in_specs = [pl.BlockSpec((NA_GROUP, width), lambda g: (g, COL_B_Q // B_HEADS)),
                kv_spec(COL_B_K, 0), kv_spec(COL_B_K, 1), kv_spec(COL_B_K, 2),
                kv_spec(COL_B_V, 0), kv_spec(COL_B_V, 1), kv_spec(COL_B_V, 2),
                pl.BlockSpec((None, B_HEADS, NA_GROUP, 3 * NA_GROUP), bias_map)]
    out_spec = pl.BlockSpec((NA_GROUP, width), lambda g: (g, 0))
    scratch = [pltpu.VMEM((B_HEADS, NA_GROUP, 3 * NA_GROUP), F32)]
    return in_specs, out_spec, scratch


N_WIN_IN, N_NA_IN = 9, 8


def _attention_kernel(*refs, segs):
    win_in, na_in = refs[:N_WIN_IN], refs[N_WIN_IN:N_WIN_IN + N_NA_IN]
    ya_ref, yb_ref, ws_ref, we_ref, ns_ref = refs[N_WIN_IN + N_NA_IN:]
    win_scores, win_outputs = _wattn_phases(*win_in, ya_ref, ws_ref, we_ref, segs)
    na_scores, na_outputs = _na_phases(*na_in, yb_ref, ns_ref)
    win_scores()
    na_scores()
    win_outputs()
    na_outputs()


def _attention_mixers(proj, sink, win_tab, na_tab, segs):
    assert WIN_STEP == NA_GROUP
    n_tok = proj.shape[0]
    win_specs, win_out, win_scratch = _window_specs(n_tok, win_tab)
    na_specs, na_out, na_scratch = _na_specs(segs)
    return pl.pallas_call(
        functools.partial(_attention_kernel, segs=segs),
        out_shape=(jax.ShapeDtypeStruct((n_tok, A_HEADS * HEAD_DIM), BF16),
                   jax.ShapeDtypeStruct((n_tok, B_HEADS * HEAD_DIM), BF16)),
        grid=(n_tok // WIN_STEP,),
        in_specs=[*win_specs, *na_specs],
        out_specs=(win_out, na_out),
        scratch_shapes=[*win_scratch, *na_scratch],
        compiler_params=_cparams(("parallel",)),
        name="attn_ab",
    )(sink, *([proj] * 7), win_tab, *([proj] * 7), na_tab)


def _na_bias_table(rel_table):
    rows_q = NA_GROUP // GRID_W
    rows_k = 3 * rows_q
    c = np.arange(GRID_W)
    col_start = np.clip(c - NA_COLS // 2, 0, GRID_W - NA_COLS)
    col_ok = (c[None, :] >= col_start[:, None]) & (c[None, :] < col_start[:, None] + NA_COLS)
    dc = np.clip(c[None, :] - c[:, None], -(NA_COLS - 1), NA_COLS - 1) + NA_COLS - 1
    onehot = (dc[None] == np.arange(2 * NA_COLS - 1)[:, None, None]).astype(np.float32)
    by_dr = jnp.einsum('hrd,dqk->hrqk', rel_table.astype(F32), jnp.asarray(onehot),
                       precision=lax.Precision.HIGHEST)
    by_dr = jnp.where(jnp.asarray(col_ok)[None, None], by_dr * LOG2E, NEG_BIG)
    by_dr = jnp.concatenate([by_dr, jnp.full_like(by_dr[:, :1], NEG_BIG)], axis=1)
    cfgs = ([(0, NA_ROWS - 1 - a) for a in range(rows_q)],
            [(a, NA_ROWS // 2 - 1) for a in range(rows_q)],
            [(rows_q, NA_ROWS // 2 - 1 - a) for a in range(rows_q)])
    idx = np.full((3, rows_q, rows_k), 2 * NA_ROWS - 1, np.int32)
    for ci, cfg in enumerate(cfgs):
        for a, (off, dr0) in enumerate(cfg):
            for j in range(NA_ROWS):
                idx[ci, a, off + j] = dr0 + j
    tab = jnp.take(by_dr, jnp.asarray(idx.reshape(-1)), axis=1)
    tab = tab.reshape(B_HEADS, 3, rows_q, rows_k, GRID_W, GRID_W).transpose(1, 0, 2, 4, 3, 5)
    return tab.reshape(3, B_HEADS, NA_GROUP, 3 * NA_GROUP)


def _split3(x):
    hi = x.astype(BF16)
    r1 = x - hi.astype(F32)
    mid = r1.astype(BF16)
    lo = (r1 - mid.astype(F32)).astype(BF16)
    return hi, mid, lo


def _hgrn_kernel(z_ref, v_ref, q_ref, lb_ref, scan_ref, o_ref, st_ref, c3_ref, b3_ref, k3_ref, q3_ref, a_ref,
                 *, segs, reverse):
    i = pl.program_id(0)
    n_steps = pl.num_programs(0)
    n_sub = z_ref.shape[0] // HG_TILE
    step = (n_steps - 1 - i) if reverse else i
    local, n_loc = _seq_pos(step, n_sub * HG_TILE, segs)
    is_start = (local == n_loc - 1) if reverse else (local == 0)

    @pl.when(is_start)
    def _():
        st_ref[...] = jnp.zeros_like(st_ref)

    nblk = HG_TILE // HG_BLK
    half = HG_BLK // 2
    n_sc = nblk // 2
    log2e = math.log2(math.e)
    nt = (((1,), (1,)), ((), ()))

    def sel_index(t0):
        shape = (nblk, half, HG_TILE)
        t = lax.broadcasted_iota(jnp.int32, shape, 1) + t0
        j = lax.broadcasted_iota(jnp.int32, shape, 2) - lax.broadcasted_iota(jnp.int32, shape, 0) * HG_BLK
        ok = (j >= 0) & (j < HG_BLK) & ((j >= t) if reverse else (j <= t))
        return jnp.where(ok, j, -1)

    sel = (sel_index(0), sel_index(half))
    rb = lax.broadcasted_iota(jnp.int32, (HG_TILE, HG_TILE), 0) // HG_BLK
    cb = lax.broadcasted_iota(jnp.int32, (HG_TILE, HG_TILE), 1) // HG_BLK
    cross = ((rb % 2 == 0) & (cb == rb + 1)) if reverse else ((rb % 2 == 1) & (cb == rb - 1))
    blk_odd = (lax.broadcasted_iota(jnp.int32, (nblk, 1, HEAD_DIM), 0) % 2) == 1
    edge_row = 0 if reverse else HG_BLK - 1
    scan = scan_ref[...]

    def one_tile(t, carry):
        tile = (n_sub - 1 - t) if reverse else t
        tok = pl.ds(pl.multiple_of(tile * HG_TILE, HG_TILE), HG_TILE)
        heads = [slice(h * HEAD_DIM, (h + 1) * HEAD_DIM) for h in range(C_HEADS)]

        for h, cs in enumerate(heads):
            lb = lb_ref[:, cs]
            f = lb + (1.0 - lb) * jax.nn.sigmoid(z_ref[tok, cs].astype(F32))
            k3 = (1.0 - f).reshape(nblk, HG_BLK, HEAD_DIM)
            pieces = jnp.concatenate(_split3(jnp.log(f)), axis=1)
            bm = jnp.dot(scan, pieces, preferred_element_type=F32)
            b = bm[:, :HEAD_DIM] + bm[:, HEAD_DIM:2 * HEAD_DIM] + bm[:, 2 * HEAD_DIM:]
            b3 = (b * log2e).reshape(nblk, HG_BLK, HEAD_DIM)
            k3_ref[h] = k3
            b3_ref[h] = b3
            c3_ref[h] = b3 - jnp.log2(jnp.maximum(k3, 0.0))
            q3_ref[h] = (q_ref[tok, cs].astype(F32) * (HEAD_DIM ** -0.5)).reshape(nblk, HG_BLK, HEAD_DIM)

        for h in range(C_HEADS):
            halves = ((q3_ref[h, :, :half, :], b3_ref[h, :, :half, :]), (q3_ref[h, :, half:, :], b3_ref[h, :, half:, :]))
            a_half = [jnp.zeros((nblk, half, HG_TILE), F32), jnp.zeros((nblk, half, HG_TILE), F32)]
            for j in range(HG_BLK):
                cj = c3_ref[h, :, j:j + 1, :]
                for hi in range(2):
                    if (j < half * hi) if reverse else (j > half * hi + half - 1):
                        continue
                    qx, bx = halves[hi]
                    p = qx * jnp.exp2(bx - cj)
                    r = jnp.sum(p, axis=-1, keepdims=True)
                    a_half[hi] = jnp.where(sel[hi] == j, r, a_half[hi])
            a_ref[h] = jnp.concatenate(a_half, axis=1).reshape(HG_TILE, HG_TILE)

        for h, cs in enumerate(heads):
            q3, k3, b3 = q3_ref[h], k3_ref[h], b3_ref[h]
            btot = b3_ref[h, :, edge_row:edge_row + 1, :]
            v = v_ref[tok, cs]
            qp3 = q3 * jnp.exp2(b3)
            kp3 = k3 * jnp.exp2(btot - b3)
            qp = qp3.reshape(HG_TILE, HEAD_DIM).astype(BF16)
            kp = kp3.reshape(HG_TILE, HEAD_DIM).astype(BF16)
            g = lax.dot_general(qp, kp, nt, preferred_element_type=F32)
            a_all = jnp.where(cross, g, a_ref[h]).astype(BF16)
            o_acc = jnp.dot(a_all, v, preferred_element_type=F32)

            dec = jnp.exp2(btot)
            one = jnp.ones_like(dec[:1])
            dprev = jnp.concatenate([one, dec[:-1]], axis=0)
            dnext = jnp.concatenate([dec[1:], one], axis=0)
            if reverse:
                qscale, kscale = jnp.where(blk_odd, 1.0, dnext), jnp.where(blk_odd, dprev, 1.0)
            else:
                qscale, kscale = jnp.where(blk_odd, dprev, 1.0), jnp.where(blk_odd, 1.0, dnext)
            qpp = (qp3 * qscale).reshape(HG_TILE, HEAD_DIM).astype(BF16)
            kpp = (kp3 * kscale).reshape(HG_TILE, HEAD_DIM).astype(BF16)
            st = st_ref[h]
            inter = [None] * n_sc
            for m in (range(n_sc - 1, -1, -1) if reverse else range(n_sc)):
                rows = slice(2 * m * HG_BLK, 2 * (m + 1) * HG_BLK)
                inter[m] = lax.dot_general(qpp[rows], st.astype(BF16), nt, preferred_element_type=F32)
                upd = lax.dot_general(v[rows], kpp[rows], (((0,), (0,)), ((), ())), preferred_element_type=F32)
                st = st * (dec[2 * m] * dec[2 * m + 1]) + upd
            st_ref[h] = st
            o_ref[tok, cs] = o_acc + jnp.concatenate(inter, axis=0)
        return carry

    lax.fori_loop(0, n_sub, one_tile, 0)


def _hgrn_scan_matrix(reverse):
    t = np.arange(HG_TILE)
    same = (t[:, None] // HG_BLK) == (t[None, :] // HG_BLK)
    incl = (t[None, :] >= t[:, None]) if reverse else (t[None, :] <= t[:, None])
    return jnp.asarray((same & incl).astype(np.float32), BF16)


def _hgrn_direction(proj, lower_bound, segs, reverse):
    n_tok = proj.shape[0]
    step_tok = next(s for s in (8 * HG_TILE, 4 * HG_TILE, 2 * HG_TILE, HG_TILE) if all(t % s == 0 for _, t in segs))
    n_steps = n_tok // step_tok
    width = C_HEADS * HEAD_DIM
    scan = _hgrn_scan_matrix(reverse)

    def tok_spec(col0):
        return pl.BlockSpec((step_tok, width),
                            lambda i: ((n_steps - 1 - i) if reverse else i, col0 // C_HEADS))

    nblk = HG_TILE // HG_BLK
    return pl.pallas_call(
        functools.partial(_hgrn_kernel, segs=segs, reverse=reverse),
        out_shape=jax.ShapeDtypeStruct((n_tok, width), F32),
        grid=(n_steps,),
        in_specs=[
            tok_spec(COL_C_FB if reverse else COL_C_FF), tok_spec(COL_C_I), tok_spec(COL_C_Q),
            pl.BlockSpec((1, width), lambda i: (0, 0)),
            pl.BlockSpec(scan.shape, lambda i: (0, 0)),
        ],
        out_specs=pl.BlockSpec((step_tok, width), lambda i: ((n_steps - 1 - i) if reverse else i, 0)),
        scratch_shapes=[pltpu.VMEM((C_HEADS, HEAD_DIM, HEAD_DIM), F32)]
        + [pltpu.VMEM((C_HEADS, nblk, HG_BLK, HEAD_DIM), F32)] * 4
        + [pltpu.VMEM((C_HEADS, HG_TILE, HG_TILE), F32)],
        compiler_params=_cparams(("arbitrary",)),
        name="hgrn_bwd" if reverse else "hgrn_fwd",
    )(proj, proj, proj, lower_bound.reshape(1, width), scan)


def _merge_kernel(ga_ref, gb_ref, gc_ref, cg_ref, ya_ref, yb_ref, of_ref, ob_ref, hn_ref,
                  wa_ref, wb_ref, wc_ref, wo_ref, h_ref, g2_ref, o_ref, o2_ref):
    o = of_ref[...] + ob_ref[...]
    heads = []
    for h in range(C_HEADS):
        oh = o[:, h * HEAD_DIM:(h + 1) * HEAD_DIM]
        ms = jnp.mean(oh * oh, axis=-1, keepdims=True)
        heads.append(oh * lax.rsqrt(ms + EPS))
    cg = cg_ref[...].astype(F32)
    yc = (jnp.concatenate(heads, axis=1) * hn_ref[...] * (cg * jax.nn.sigmoid(cg))).astype(BF16)
    m = jax.nn.sigmoid(ga_ref[...].astype(F32)) * jnp.dot(ya_ref[...], wa_ref[...], preferred_element_type=F32)
    m += jax.nn.sigmoid(gb_ref[...].astype(F32)) * jnp.dot(yb_ref[...], wb_ref[...], preferred_element_type=F32)
    m += jax.nn.sigmoid(gc_ref[...].astype(F32)) * jnp.dot(yc, wc_ref[...], preferred_element_type=F32)
    y = h_ref[...] + jnp.dot(m.astype(BF16), wo_ref[...], preferred_element_type=F32)
    o_ref[...] = y
    ms = jnp.mean(y * y, axis=-1, keepdims=True)
    o2_ref[...] = (y * lax.rsqrt(ms + EPS) * g2_ref[...]).astype(o2_ref.dtype)


def _merge(proj, ya, yb, o_f, o_b, hgrn_gain, wa, wb, wc, wo, layer, h, gain2, tm):
    n_tok, d = h.shape
    cw = C_HEADS * HEAD_DIM

    def resident(w):
        return pl.BlockSpec((None,) + w.shape[1:], lambda i: (layer, 0, 0), pipeline_mode=pl.Buffered(1))

    def gate_spec(which):
        return pl.BlockSpec((tm, d), lambda i: (i, which))

    return pl.pallas_call(
        _merge_kernel,
        out_shape=(jax.ShapeDtypeStruct((n_tok, d), F32), jax.ShapeDtypeStruct((n_tok, d), BF16)),
        grid=(n_tok // tm,),
        in_specs=[
            gate_spec(0), gate_spec(1), gate_spec(2),
            pl.BlockSpec((tm, cw), lambda i: (i, COL_C_G // C_HEADS)),
            pl.BlockSpec((tm, ya.shape[1]), lambda i: (i, 0)),
            pl.BlockSpec((tm, cw), lambda i: (i, 0)),
            pl.BlockSpec((tm, cw), lambda i: (i, 0)),
            pl.BlockSpec((tm, cw), lambda i: (i, 0)),
            pl.BlockSpec((1, cw), lambda i: (0, 0)),
            resident(wa), resident(wb), resident(wc), resident(wo),
            pl.BlockSpec((tm, d), lambda i: (i, 0)),
            pl.BlockSpec((1, d), lambda i: (0, 0)),
        ],
        out_specs=(pl.BlockSpec((tm, d), lambda i: (i, 0)), pl.BlockSpec((tm, d), lambda i: (i, 0))),
        compiler_params=_cparams(("parallel",)),
        name="merge_out",
    )(proj, proj, proj, proj, ya, yb, o_f, o_b, hgrn_gain.reshape(1, cw), wa, wb, wc, wo, h,
      gain2.reshape(1, d))


def _tile(n, candidates):
    return next(t for t in candidates if n % t == 0)


def _trunk(xp, xs, segs, p):
    depth = p["w_in"].shape[0]
    n_prompt, n_tok = xp.shape[0], xp.shape[0] + xs.shape[0]
    tm_up = _tile(n_tok, (3072, 2048, 1024, 512, 256, 128))
    tm_join = next(t for t in (512, 256, 128) if n_prompt % t == 0 and n_tok % t == 0)
    tm_down = _tile(n_tok, (512, 384, 256, 128))
    tm_down_last = next(t for t in (256, 128) if n_prompt % t == 0 and n_tok % t == 0)
    tm_proj = _tile(n_tok, (4096, 2048, 1024, 512, 256, 128))
    tm_merge = _tile(n_tok, (384, 256, 128))
    d_ff = p["ffn1_w_gate"].shape[-1]
    tf = 512 if d_ff % 512 == 0 else 256
    mix_w = MIX_COLS * HEAD_DIM

    lb_p = jax.nn.softmax(p["hgrn_lb_logits"].astype(F32), axis=1)
    lower_bounds = jnp.cumsum(lb_p, axis=1) - lb_p[:, :1]
    win_tab = _window_bias_table(p["t5_bias"])
    bf = lambda name: p[name].astype(BF16)
    w = {name: bf(name) for name in ("ffn1_w_gate", "ffn1_w_up", "ffn1_w_down", "w_in", "w_branch_a",
                                      "w_branch_b", "w_branch_c", "w_out", "ffn2_w_gate", "ffn2_w_up",
                                      "ffn2_w_down")}
    ffn1_w = (w["ffn1_w_gate"], w["ffn1_w_up"], w["ffn1_w_down"] * 0.5)
    ffn2_w = (w["ffn2_w_gate"], w["ffn2_w_up"], w["ffn2_w_down"] * 0.5)
    x, n = _join_norm(xp, xs, p["ffn1_norm"][0], tm_join)
    for l in range(depth):
        a = _ffn_up(n, ffn1_w, l, tm_up, tf)
        h, u = _ffn_down(a, x, ffn1_w, l, p["mix_norm"][l], tm_down)
        proj = _matmul(u, w["w_in"], l, BF16, tm_proj, 512, mix_w // 512)
        ya, yb = _attention_mixers(proj, p["attn_sink"][l].astype(F32) * LOG2E, win_tab,
                                   _na_bias_table(p["na_bias"][l]), segs)
        o_f = _hgrn_direction(proj, lower_bounds[0, l], segs, reverse=False)
        o_b = _hgrn_direction(proj, lower_bounds[1, l], segs, reverse=True)
        h, n = _merge(proj, ya, yb, o_f, o_b, p["hgrn_norm"][l], w["w_branch_a"], w["w_branch_b"],
                      w["w_branch_c"], w["w_out"], l, h, p["ffn2_norm"][l], tm_merge)
        a = _ffn_up(n, ffn2_w, l, tm_up, tf)
        if l == depth - 1:
            return _ffn_down(a, h, ffn2_w, l, p["final_norm"], tm_down_last, n_prompt)
        x, n = _ffn_down(a, h, ffn2_w, l, p["ffn1_norm"][l + 1], tm_down)


def kernel(x_prompt, x_sample, ffn1_norm, ffn1_w_gate, ffn1_w_up, ffn1_w_down, mix_norm, w_in, attn_sink,
           t5_bias, na_bias, hgrn_lb_logits, hgrn_norm, w_branch_a, w_branch_b, w_branch_c, w_out,
           ffn2_norm, ffn2_w_gate, ffn2_w_up, ffn2_w_down, final_norm):
    params = dict(ffn1_norm=ffn1_norm, ffn1_w_gate=ffn1_w_gate, ffn1_w_up=ffn1_w_up, ffn1_w_down=ffn1_w_down,
                  mix_norm=mix_norm, w_in=w_in, attn_sink=attn_sink, t5_bias=t5_bias, na_bias=na_bias,
                  hgrn_lb_logits=hgrn_lb_logits, hgrn_norm=hgrn_norm, w_branch_a=w_branch_a,
                  w_branch_b=w_branch_b, w_branch_c=w_branch_c, w_out=w_out, ffn2_norm=ffn2_norm,
                  ffn2_w_gate=ffn2_w_gate, ffn2_w_up=ffn2_w_up, ffn2_w_down=ffn2_w_down, final_norm=final_norm)
    d = x_prompt.shape[-1]
    segs = (x_prompt.shape[:2], x_sample.shape[:2])
    for _, t in segs:
        assert t % NA_GROUP == 0 and t // NA_GROUP >= 3 and t // GRID_W >= NA_ROWS and t % WIN_STEP == 0
    y_prompt, y_sample = _trunk(x_prompt.reshape(-1, d), x_sample.reshape(-1, d), segs, params)
    return y_prompt.reshape(x_prompt.shape), y_sample.reshape(x_sample.shape)
```

```python
import functools
import math

import numpy as np
import jax
import jax.numpy as jnp
from jax import lax
from jax.experimental import pallas as pl
from jax.experimental.pallas import tpu as pltpu

F32 = jnp.float32
BF16 = jnp.bfloat16

HEAD_DIM = 128
EPS = 1e-6
A_HEADS = 8
A_KV_HEADS = 2
WINDOW = 128
WIN_BLOCK = 128
T5_BUCKETS = 32
T5_MAX_DIST = 128
B_HEADS = 4
GRID_W = 64
NA_ROWS = 8
NA_COLS = 16
C_HEADS = 4
GATE_COLS = 48
MIX_COLS = 44
COL_A_Q, COL_A_K, COL_A_V = (GATE_COLS + c for c in (0, 8, 10))
COL_B_Q, COL_B_K, COL_B_V = (GATE_COLS + c for c in (12, 16, 20))
COL_C_FF, COL_C_FB, COL_C_I, COL_C_Q, COL_C_G = (GATE_COLS + c for c in (24, 28, 32, 36, 40))

NEG_BIG = -1e30
LOG2E = math.log2(math.e)
V7X_VMEM_LIMIT = 60 * 1024 * 1024
HG_TILE = 128
HG_BLK = 16
NA_GROUP = 4 * GRID_W
WIN_STEP = 2 * WIN_BLOCK
MXU_COLS = 256


def _cparams(sem):
    return pltpu.CompilerParams(dimension_semantics=sem, vmem_limit_bytes=V7X_VMEM_LIMIT)


def _seq_pos(gb, blk, segs):
    (n_seq0, t0), (_, t1) = segs
    n0, n1 = t0 // blk, t1 // blk
    tot0 = n_seq0 * n0
    in0 = gb < tot0
    local = jnp.where(in0, gb % n0, (gb - tot0) % n1)
    return local, jnp.where(in0, n0, n1)


def _normed(y, gain_ref, dtype):
    ms = jnp.mean(y * y, axis=-1, keepdims=True)
    return (y * lax.rsqrt(ms + EPS) * gain_ref[...]).astype(dtype)


def _ffn_gate(n, wg_ref, wu_ref, a_ref):
    for c in range(0, a_ref.shape[1], MXU_COLS):
        cc = slice(c, c + MXU_COLS)
        g = jnp.dot(n, wg_ref[:, cc], preferred_element_type=F32)
        u = jnp.dot(n, wu_ref[:, cc], preferred_element_type=F32)
        a_ref[:, cc] = (g * jax.nn.sigmoid(g) * u).astype(a_ref.dtype)


def _ffn_up_kernel(n_ref, wg_ref, wu_ref, a_ref):
    _ffn_gate(n_ref[...], wg_ref, wu_ref, a_ref)


def _join_norm_kernel(xp_ref, xs_ref, g1_ref, x_ref, n_ref, *, prompt_tiles):
    x = jnp.where(pl.program_id(0) < prompt_tiles, xp_ref[...], xs_ref[...])
    x_ref[...] = x
    n_ref[...] = _normed(x, g1_ref, n_ref.dtype)


def _ffn_down_kernel(a_ref, wd_ref, x_ref, g2_ref, o_ref, o2_ref):
    y = x_ref[...] + jnp.dot(a_ref[...], wd_ref[...], preferred_element_type=F32)
    o_ref[...] = y
    o2_ref[...] = _normed(y, g2_ref, o2_ref.dtype)


def _ffn_down_last_kernel(a_ref, wd_ref, x_ref, g2_ref, yp_ref, ys_ref, *, prompt_tiles):
    i = pl.program_id(0)
    y = x_ref[...] + jnp.dot(a_ref[...], wd_ref[...], preferred_element_type=F32)

    @pl.when(i < prompt_tiles)
    def _():
        yp_ref[...] = _normed(y, g2_ref, yp_ref.dtype)

    @pl.when(i >= prompt_tiles)
    def _():
        ys_ref[...] = _normed(y, g2_ref, ys_ref.dtype)


def _ffn_up(n, weights, layer, tm, tf):
    wg, wu, _ = weights
    d, d_ff = wg.shape[1:]
    n_tok = n.shape[0]
    w_spec = pl.BlockSpec((None, d, tf), lambda i, j: (layer, 0, j))
    return pl.pallas_call(
        _ffn_up_kernel,
        out_shape=jax.ShapeDtypeStruct((n_tok, d_ff), BF16),
        grid=(n_tok // tm, d_ff // tf),
        in_specs=[pl.BlockSpec((tm, d), lambda i, j: (i, 0)), w_spec, w_spec],
        out_specs=pl.BlockSpec((tm, tf), lambda i, j: (i, j)),
        compiler_params=_cparams(("parallel", "arbitrary")),
        name="ffn_up",
    )(n, wg, wu)


def _join_norm(xp, xs, gain1, tm):
    d = xp.shape[1]
    n_prompt, n_tok = xp.shape[0], xp.shape[0] + xs.shape[0]
    assert n_prompt % tm == 0
    pt = n_prompt // tm
    tok_spec = pl.BlockSpec((tm, d), lambda i: (i, 0))
    return pl.pallas_call(
        functools.partial(_join_norm_kernel, prompt_tiles=pt),
        out_shape=(jax.ShapeDtypeStruct((n_tok, d), F32), jax.ShapeDtypeStruct((n_tok, d), BF16)),
        grid=(n_tok // tm,),
        in_specs=[pl.BlockSpec((tm, d), lambda i: (jnp.minimum(i, pt - 1), 0)),
                  pl.BlockSpec((tm, d), lambda i: (jnp.maximum(i - pt, 0), 0)),
                  pl.BlockSpec((1, d), lambda i: (0, 0))],
        out_specs=(tok_spec, tok_spec),
        compiler_params=_cparams(("arbitrary",)),
        name="join_norm",
    )(xp, xs, gain1.reshape(1, d))


def _ffn_down(a, x, weights, layer, gain2, tm, n_prompt=None):
    wd = weights[2]
    d_ff, d = wd.shape[1:]
    n_tok = x.shape[0]
    tok = lambda width: pl.BlockSpec((tm, width), lambda i: (i, 0))
    in_specs = [tok(d_ff), pl.BlockSpec((None, d_ff, d), lambda i: (layer, 0, 0), pipeline_mode=pl.Buffered(1)),
                tok(d), pl.BlockSpec((1, d), lambda i: (0, 0))]
    if n_prompt is None:
        body = _ffn_down_kernel
        out_shape = (jax.ShapeDtypeStruct((n_tok, d), F32), jax.ShapeDtypeStruct((n_tok, d), BF16))
        out_specs = (tok(d), tok(d))
    else:
        assert n_prompt % tm == 0
        pt = n_prompt // tm
        body = functools.partial(_ffn_down_last_kernel, prompt_tiles=pt)
        out_shape = (jax.ShapeDtypeStruct((n_prompt, d), F32), jax.ShapeDtypeStruct((n_tok - n_prompt, d), F32))
        out_specs = (pl.BlockSpec((tm, d), lambda i: (jnp.minimum(i, pt - 1), 0)),
                     pl.BlockSpec((tm, d), lambda i: (jnp.maximum(i - pt, 0), 0)))
    return pl.pallas_call(
        body,
        out_shape=out_shape,
        grid=(n_tok // tm,),
        in_specs=in_specs,
        out_specs=out_specs,
        compiler_params=_cparams(("arbitrary",)),
        name="ffn_down",
    )(a, wd, x, gain2.reshape(1, d))


def _matmul_kernel(a_ref, w_ref, o_ref):
    o_ref[...] = jnp.dot(a_ref[...], w_ref[...], preferred_element_type=F32).astype(o_ref.dtype)


def _matmul(a, w, layer, out_dtype, tm, tn, rotate):
    n_tok, k = a.shape
    n_out = w.shape[-1]
    n_col = n_out // tn
    return pl.pallas_call(
        _matmul_kernel,
        out_shape=jax.ShapeDtypeStruct((n_tok, n_out), out_dtype),
        grid=(n_tok // tm, n_out // tn),
        in_specs=[pl.BlockSpec((tm, k), lambda i, j: (i, 0)),
                  pl.BlockSpec((None, k, tn), lambda i, j: (layer, 0, (j + rotate) % n_col))],
        out_specs=pl.BlockSpec((tm, tn), lambda i, j: (i, j)),
        compiler_params=_cparams(("parallel", "arbitrary")),
        name="proj_in",
    )(a, w)


def _wattn_phases(sink_ref, q_ref, kp_ref, kc_ref, kn_ref, vp_ref, vc_ref, vn_ref, bias_ref, o_ref, s_ref, e_ref, segs):
    local, n_loc = _seq_pos(pl.program_id(0), WIN_STEP, segs)
    variant = (jnp.where(local == 0, 0, 1), jnp.where(local == n_loc - 1, 2, 1))
    group = A_HEADS // A_KV_HEADS
    gw = group * WIN_BLOCK
    scale = HEAD_DIM ** -0.5 * LOG2E
    units = [(blk, kv) for blk in range(WIN_STEP // WIN_BLOCK) for kv in range(A_KV_HEADS)]

    def keys_of(p_ref, c_ref, n_ref, blk, kv):
        cs = slice(kv * HEAD_DIM, (kv + 1) * HEAD_DIM)
        cat = jnp.concatenate([p_ref[:, cs], c_ref[:, cs], n_ref[:, cs]], axis=0)
        return cat[blk * WIN_BLOCK:(blk + 3) * WIN_BLOCK]

    def scores():
        for u, (blk, kv) in enumerate(units):
            rows = slice(blk * WIN_BLOCK, (blk + 1) * WIN_BLOCK)
            qs = jnp.concatenate([q_ref[rows, h * HEAD_DIM:(h + 1) * HEAD_DIM]
                                  for h in range(kv * group, (kv + 1) * group)], axis=0)
            s = lax.dot_general(qs, keys_of(kp_ref, kc_ref, kn_ref, blk, kv), (((1,), (1,)), ((), ())),
                                preferred_element_type=F32)
            s_ref[u] = s * scale + bias_ref[variant[blk], kv * gw:(kv + 1) * gw, :]

    def outputs():
        for u, (blk, kv) in enumerate(units):
            rows = slice(blk * WIN_BLOCK, (blk + 1) * WIN_BLOCK)
            dens = []
            for i in range(group):
                hr = slice(i * WIN_BLOCK, (i + 1) * WIN_BLOCK)
                sh = s_ref[u, hr, :]
                sink = sink_ref[kv * group + i]
                m = jnp.maximum(jnp.max(sh, axis=-1, keepdims=True), sink)
                e = jnp.exp2(sh - m)
                dens.append(jnp.sum(e, axis=-1, keepdims=True) + jnp.exp2(sink - m))
                e_ref[u, hr, :] = e.astype(BF16)
            o4 = jnp.dot(e_ref[u], keys_of(vp_ref, vc_ref, vn_ref, blk, kv), preferred_element_type=F32)
            for i in range(group):
                h = kv * group + i
                o_ref[rows, h * HEAD_DIM:(h + 1) * HEAD_DIM] = (
                    o4[i * WIN_BLOCK:(i + 1) * WIN_BLOCK] / dens[i]).astype(o_ref.dtype)

    return scores, outputs


def _window_specs(n_tok, bias_tab):
    nb = n_tok // WIN_BLOCK
    per_step = WIN_STEP // WIN_BLOCK
    kv_w = A_KV_HEADS * HEAD_DIM
    n_units = per_step * A_KV_HEADS
    gw = (A_HEADS // A_KV_HEADS) * WIN_BLOCK

    def kv_specs(col0):
        col = col0 // A_KV_HEADS
        return [pl.BlockSpec((WIN_BLOCK, kv_w), lambda s: (jnp.maximum(per_step * s - 1, 0), col)),
                pl.BlockSpec((WIN_STEP, kv_w), lambda s: (s, col)),
                pl.BlockSpec((WIN_BLOCK, kv_w), lambda s: (jnp.minimum(per_step * (s + 1), nb - 1), col))]

    in_specs = [pl.BlockSpec(memory_space=pltpu.SMEM),
                pl.BlockSpec((WIN_STEP, A_HEADS * HEAD_DIM), lambda s: (s, COL_A_Q // A_HEADS)),
                *kv_specs(COL_A_K), *kv_specs(COL_A_V),
                pl.BlockSpec(bias_tab.shape, lambda s: (0, 0, 0))]
    out_spec = pl.BlockSpec((WIN_STEP, A_HEADS * HEAD_DIM), lambda s: (s, 0))
    scratch = [pltpu.VMEM((n_units, gw, 3 * WIN_BLOCK), F32), pltpu.VMEM((n_units, gw, 3 * WIN_BLOCK), BF16)]
    return in_specs, out_spec, scratch


def _t5_bucket(rel):
    nb = T5_BUCKETS // 2
    max_exact = nb // 2
    base = jnp.where(rel > 0, nb, 0)
    n = jnp.abs(rel)
    large = max_exact + (jnp.log(jnp.maximum(n, 1).astype(F32) / max_exact)
                         / math.log(T5_MAX_DIST / max_exact) * (nb - max_exact)).astype(jnp.int32)
    large = jnp.minimum(large, nb - 1)
    return base + jnp.where(n < max_exact, n, large)


def _window_bias_table(t5_bias):
    qi = jnp.arange(WIN_BLOCK)[:, None]
    si = jnp.arange(3 * WIN_BLOCK)[None, :]
    rel = si - WIN_BLOCK - qi
    onehot = (_t5_bucket(rel)[None] == jnp.arange(T5_BUCKETS)[:, None, None]).astype(F32)
    bias = jnp.einsum('bh,bqs->hqs', t5_bias.astype(F32), onehot, precision=lax.Precision.HIGHEST)
    band = jnp.abs(rel) <= WINDOW
    variants = []
    for lo_ok, hi_ok in ((False, True), (True, True), (True, False)):
        ok = band & ((si >= WIN_BLOCK) | lo_ok) & ((si < 2 * WIN_BLOCK) | hi_ok)
        variants.append(jnp.where(ok[None], bias * LOG2E, NEG_BIG).reshape(A_HEADS * WIN_BLOCK, 3 * WIN_BLOCK))
    return jnp.stack(variants)


def _na_window(g, segs):
    lg, ng = _seq_pos(g, NA_GROUP, segs)
    return lg, ng, jnp.clip(lg - 1, 0, ng - 3)


def _na_phases(q_ref, k0_ref, k1_ref, k2_ref, v0_ref, v1_ref, v2_ref, bias_ref, o_ref, s_ref):
    scale = HEAD_DIM ** -0.5 * LOG2E
    heads = [slice(h * HEAD_DIM, (h + 1) * HEAD_DIM) for h in range(B_HEADS)]

    def scores():
        for h, cs in enumerate(heads):
            kcat = jnp.concatenate([k0_ref[:, cs], k1_ref[:, cs], k2_ref[:, cs]], axis=0)
            s = lax.dot_general(q_ref[:, cs], kcat, (((1,), (1,)), ((), ())), preferred_element_type=F32)
            s_ref[h] = s * scale + bias_ref[h]

    def outputs():
        for h, cs in enumerate(heads):
            vcat = jnp.concatenate([v0_ref[:, cs], v1_ref[:, cs], v2_ref[:, cs]], axis=0)
            s = s_ref[h]
            m = jnp.max(s, axis=-1, keepdims=True)
            e = jnp.exp2(s - m)
            den = jnp.sum(e, axis=-1, keepdims=True)
            oh = jnp.dot(e.astype(BF16), vcat, preferred_element_type=F32)
            o_ref[:, cs] = (oh / den).astype(o_ref.dtype)

    return scores, outputs


def _na_specs(segs):
    width = B_HEADS * HEAD_DIM

    def kv_spec(col0, j):
        def index_map(g):
            lg, _, lo = _na_window(g, segs)
            return (g - lg + lo + j, col0 // B_HEADS)
        return pl.BlockSpec((NA_GROUP, width), index_map)

    def bias_map(g):
        lg, ng, _ = _na_window(g, segs)
        return (jnp.where(lg == 0, 0, jnp.where(lg == ng - 1, 2, 1)), 0, 0, 0)

    in_specs = [pl.BlockSpec((NA_GROUP, width), lambda g: (g, COL_B_Q // B_HEADS)),
                kv_spec(COL_B_K, 0), kv_spec(COL_B_K, 1), kv_spec(COL_B_K, 2),
                kv_spec(COL_B_V, 0), kv_spec(COL_B_V, 1), kv_spec(COL_B_V, 2),
                pl.BlockSpec((None, B_HEADS, NA_GROUP, 3 * NA_GROUP), bias_map)]
    out_spec = pl.BlockSpec((NA_GROUP, width), lambda g: (g, 0))
    scratch = [pltpu.VMEM((B_HEADS, NA_GROUP, 3 * NA_GROUP), F32)]
    return in_specs, out_spec, scratch


N_WIN_IN, N_NA_IN = 9, 8


def _attention_kernel(*refs, segs):
    win_in, na_in = refs[:N_WIN_IN], refs[N_WIN_IN:N_WIN_IN + N_NA_IN]
    ya_ref, yb_ref, ws_ref, we_ref, ns_ref = refs[N_WIN_IN + N_NA_IN:]
    win_scores, win_outputs = _wattn_phases(*win_in, ya_ref, ws_ref, we_ref, segs)
    na_scores, na_outputs = _na_phases(*na_in, yb_ref, ns_ref)
    win_scores()
    na_scores()
    win_outputs()
    na_outputs()


def _attention_mixers(proj, sink, win_tab, na_tab, segs):
    assert WIN_STEP == NA_GROUP
    n_tok = proj.shape[0]
    win_specs, win_out, win_scratch = _window_specs(n_tok, win_tab)
    na_specs, na_out, na_scratch = _na_specs(segs)
    return pl.pallas_call(
        functools.partial(_attention_kernel, segs=segs),
        out_shape=(jax.ShapeDtypeStruct((n_tok, A_HEADS * HEAD_DIM), BF16),
                   jax.ShapeDtypeStruct((n_tok, B_HEADS * HEAD_DIM), BF16)),
        grid=(n_tok // WIN_STEP,),
        in_specs=[*win_specs, *na_specs],
        out_specs=(win_out, na_out),
        scratch_shapes=[*win_scratch, *na_scratch],
        compiler_params=_cparams(("parallel",)),
        name="attn_ab",
    )(sink, *([proj] * 7), win_tab, *([proj] * 7), na_tab)


def _na_bias_table(rel_table):
    rows_q = NA_GROUP // GRID_W
    rows_k = 3 * rows_q
    c = np.arange(GRID_W)
    col_start = np.clip(c - NA_COLS // 2, 0, GRID_W - NA_COLS)
    col_ok = (c[None, :] >= col_start[:, None]) & (c[None, :] < col_start[:, None] + NA_COLS)
    dc = np.clip(c[None, :] - c[:, None], -(NA_COLS - 1), NA_COLS - 1) + NA_COLS - 1
    onehot = (dc[None] == np.arange(2 * NA_COLS - 1)[:, None, None]).astype(np.float32)
    by_dr = jnp.einsum('hrd,dqk->hrqk', rel_table.astype(F32), jnp.asarray(onehot),
                       precision=lax.Precision.HIGHEST)
    by_dr = jnp.where(jnp.asarray(col_ok)[None, None], by_dr * LOG2E, NEG_BIG)
    by_dr = jnp.concatenate([by_dr, jnp.full_like(by_dr[:, :1], NEG_BIG)], axis=1)
    cfgs = ([(0, NA_ROWS - 1 - a) for a in range(rows_q)],
            [(a, NA_ROWS // 2 - 1) for a in range(rows_q)],
            [(rows_q, NA_ROWS // 2 - 1 - a) for a in range(rows_q)])
    idx = np.full((3, rows_q, rows_k), 2 * NA_ROWS - 1, np.int32)
    for ci, cfg in enumerate(cfgs):
        for a, (off, dr0) in enumerate(cfg):
            for j in range(NA_ROWS):
                idx[ci, a, off + j] = dr0 + j
    tab = jnp.take(by_dr, jnp.asarray(idx.reshape(-1)), axis=1)
    tab = tab.reshape(B_HEADS, 3, rows_q, rows_k, GRID_W, GRID_W).transpose(1, 0, 2, 4, 3, 5)
    return tab.reshape(3, B_HEADS, NA_GROUP, 3 * NA_GROUP)


def _split3(x):
    hi = x.astype(BF16)
    r1 = x - hi.astype(F32)
    mid = r1.astype(BF16)
    lo = (r1 - mid.astype(F32)).astype(BF16)
    return hi, mid, lo


def _hgrn_kernel(z_ref, v_ref, q_ref, lb_ref, scan_ref, o_ref, st_ref, c3_ref, b3_ref, k3_ref, q3_ref, a_ref,
                 *, segs, reverse):
    i = pl.program_id(0)
    n_steps = pl.num_programs(0)
    n_sub = z_ref.shape[0] // HG_TILE
    step = (n_steps - 1 - i) if reverse else i
    local, n_loc = _seq_pos(step, n_sub * HG_TILE, segs)
    is_start = (local == n_loc - 1) if reverse else (local == 0)

    @pl.when(is_start)
    def _():
        st_ref[...] = jnp.zeros_like(st_ref)

    nblk = HG_TILE // HG_BLK
    half = HG_BLK // 2
    n_sc = nblk // 2
    log2e = math.log2(math.e)
    nt = (((1,), (1,)), ((), ()))

    def sel_index(t0):
        shape = (nblk, half, HG_TILE)
        t = lax.broadcasted_iota(jnp.int32, shape, 1) + t0
        j = lax.broadcasted_iota(jnp.int32, shape, 2) - lax.broadcasted_iota(jnp.int32, shape, 0) * HG_BLK
        ok = (j >= 0) & (j < HG_BLK) & ((j >= t) if reverse else (j <= t))
        return jnp.where(ok, j, -1)

    sel = (sel_index(0), sel_index(half))
    rb = lax.broadcasted_iota(jnp.int32, (HG_TILE, HG_TILE), 0) // HG_BLK
    cb = lax.broadcasted_iota(jnp.int32, (HG_TILE, HG_TILE), 1) // HG_BLK
    cross = ((rb % 2 == 0) & (cb == rb + 1)) if reverse else ((rb % 2 == 1) & (cb == rb - 1))
    blk_odd = (lax.broadcasted_iota(jnp.int32, (nblk, 1, HEAD_DIM), 0) % 2) == 1
    edge_row = 0 if reverse else HG_BLK - 1
    scan = scan_ref[...]

    def one_tile(t, carry):
        tile = (n_sub - 1 - t) if reverse else t
        tok = pl.ds(pl.multiple_of(tile * HG_TILE, HG_TILE), HG_TILE)
        heads = [slice(h * HEAD_DIM, (h + 1) * HEAD_DIM) for h in range(C_HEADS)]

        for h, cs in enumerate(heads):
            lb = lb_ref[:, cs]
            f = lb + (1.0 - lb) * jax.nn.sigmoid(z_ref[tok, cs].astype(F32))
            k3 = (1.0 - f).reshape(nblk, HG_BLK, HEAD_DIM)
            pieces = jnp.concatenate(_split3(jnp.log(f)), axis=1)
            bm = jnp.dot(scan, pieces, preferred_element_type=F32)
            b = bm[:, :HEAD_DIM] + bm[:, HEAD_DIM:2 * HEAD_DIM] + bm[:, 2 * HEAD_DIM:]
            b3 = (b * log2e).reshape(nblk, HG_BLK, HEAD_DIM)
            k3_ref[h] = k3
            b3_ref[h] = b3
            c3_ref[h] = b3 - jnp.log2(jnp.maximum(k3, 0.0))
            q3_ref[h] = (q_ref[tok, cs].astype(F32) * (HEAD_DIM ** -0.5)).reshape(nblk, HG_BLK, HEAD_DIM)

        for h in range(C_HEADS):
            halves = ((q3_ref[h, :, :half, :], b3_ref[h, :, :half, :]), (q3_ref[h, :, half:, :], b3_ref[h, :, half:, :]))
            a_half = [jnp.zeros((nblk, half, HG_TILE), F32), jnp.zeros((nblk, half, HG_TILE), F32)]
            for j in range(HG_BLK):
                cj = c3_ref[h, :, j:j + 1, :]
                for hi in range(2):
                    if (j < half * hi) if reverse else (j > half * hi + half - 1):
                        continue
                    qx, bx = halves[hi]
                    p = qx * jnp.exp2(bx - cj)
                    r = jnp.sum(p, axis=-1, keepdims=True)
                    a_half[hi] = jnp.where(sel[hi] == j, r, a_half[hi])
            a_ref[h] = jnp.concatenate(a_half, axis=1).reshape(HG_TILE, HG_TILE)

        for h, cs in enumerate(heads):
            q3, k3, b3 = q3_ref[h], k3_ref[h], b3_ref[h]
            btot = b3_ref[h, :, edge_row:edge_row + 1, :]
            v = v_ref[tok, cs]
            qp3 = q3 * jnp.exp2(b3)
            kp3 = k3 * jnp.exp2(btot - b3)
            qp = qp3.reshape(HG_TILE, HEAD_DIM).astype(BF16)
            kp = kp3.reshape(HG_TILE, HEAD_DIM).astype(BF16)
            g = lax.dot_general(qp, kp, nt, preferred_element_type=F32)
            a_all = jnp.where(cross, g, a_ref[h]).astype(BF16)
            o_acc = jnp.dot(a_all, v, preferred_element_type=F32)

            dec = jnp.exp2(btot)
            one = jnp.ones_like(dec[:1])
            dprev = jnp.concatenate([one, dec[:-1]], axis=0)
            dnext = jnp.concatenate([dec[1:], one], axis=0)
            if reverse:
                qscale, kscale = jnp.where(blk_odd, 1.0, dnext), jnp.where(blk_odd, dprev, 1.0)
            else:
                qscale, kscale = jnp.where(blk_odd, dprev, 1.0), jnp.where(blk_odd, 1.0, dnext)
            qpp = (qp3 * qscale).reshape(HG_TILE, HEAD_DIM).astype(BF16)
            kpp = (kp3 * kscale).reshape(HG_TILE, HEAD_DIM).astype(BF16)
            st = st_ref[h]
            inter = [None] * n_sc
            for m in (range(n_sc - 1, -1, -1) if reverse else range(n_sc)):
                rows = slice(2 * m * HG_BLK, 2 * (m + 1) * HG_BLK)
                inter[m] = lax.dot_general(qpp[rows], st.astype(BF16), nt, preferred_element_type=F32)
                upd = lax.dot_general(v[rows], kpp[rows], (((0,), (0,)), ((), ())), preferred_element_type=F32)
                st = st * (dec[2 * m] * dec[2 * m + 1]) + upd
            st_ref[h] = st
            o_ref[tok, cs] = o_acc + jnp.concatenate(inter, axis=0)
        return carry

    lax.fori_loop(0, n_sub, one_tile, 0)


def _hgrn_scan_matrix(reverse):
    t = np.arange(HG_TILE)
    same = (t[:, None] // HG_BLK) == (t[None, :] // HG_BLK)
    incl = (t[None, :] >= t[:, None]) if reverse else (t[None, :] <= t[:, None])
    return jnp.asarray((same & incl).astype(np.float32), BF16)


def _hgrn_direction(proj, lower_bound, segs, reverse):
    n_tok = proj.shape[0]
    step_tok = next(s for s in (8 * HG_TILE, 4 * HG_TILE, 2 * HG_TILE, HG_TILE) if all(t % s == 0 for _, t in segs))
    n_steps = n_tok // step_tok
    width = C_HEADS * HEAD_DIM
    scan = _hgrn_scan_matrix(reverse)

    def tok_spec(col0):
        return pl.BlockSpec((step_tok, width),
                            lambda i: ((n_steps - 1 - i) if reverse else i, col0 // C_HEADS))

    nblk = HG_TILE // HG_BLK
    return pl.pallas_call(
        functools.partial(_hgrn_kernel, segs=segs, reverse=reverse),
        out_shape=jax.ShapeDtypeStruct((n_tok, width), F32),
        grid=(n_steps,),
        in_specs=[
            tok_spec(COL_C_FB if reverse else COL_C_FF), tok_spec(COL_C_I), tok_spec(COL_C_Q),
            pl.BlockSpec((1, width), lambda i: (0, 0)),
            pl.BlockSpec(scan.shape, lambda i: (0, 0)),
        ],
        out_specs=pl.BlockSpec((step_tok, width), lambda i: ((n_steps - 1 - i) if reverse else i, 0)),
        scratch_shapes=[pltpu.VMEM((C_HEADS, HEAD_DIM, HEAD_DIM), F32)]
        + [pltpu.VMEM((C_HEADS, nblk, HG_BLK, HEAD_DIM), F32)] * 4
        + [pltpu.VMEM((C_HEADS, HG_TILE, HG_TILE), F32)],
        compiler_params=_cparams(("arbitrary",)),
        name="hgrn_bwd" if reverse else "hgrn_fwd",
    )(proj, proj, proj, lower_bound.reshape(1, width), scan)


def _merge_kernel(ga_ref, gb_ref, gc_ref, cg_ref, ya_ref, yb_ref, of_ref, ob_ref, hn_ref,
                  wa_ref, wb_ref, wc_ref, wo_ref, h_ref, g2_ref, o_ref, o2_ref):
    o = of_ref[...] + ob_ref[...]
    heads = []
    for h in range(C_HEADS):
        oh = o[:, h * HEAD_DIM:(h + 1) * HEAD_DIM]
        ms = jnp.mean(oh * oh, axis=-1, keepdims=True)
        heads.append(oh * lax.rsqrt(ms + EPS))
    cg = cg_ref[...].astype(F32)
    yc = (jnp.concatenate(heads, axis=1) * hn_ref[...] * (cg * jax.nn.sigmoid(cg))).astype(BF16)
    m = jax.nn.sigmoid(ga_ref[...].astype(F32)) * jnp.dot(ya_ref[...], wa_ref[...], preferred_element_type=F32)
    m += jax.nn.sigmoid(gb_ref[...].astype(F32)) * jnp.dot(yb_ref[...], wb_ref[...], preferred_element_type=F32)
    m += jax.nn.sigmoid(gc_ref[...].astype(F32)) * jnp.dot(yc, wc_ref[...], preferred_element_type=F32)
    y = h_ref[...] + jnp.dot(m.astype(BF16), wo_ref[...], preferred_element_type=F32)
    o_ref[...] = y
    ms = jnp.mean(y * y, axis=-1, keepdims=True)
    o2_ref[...] = (y * lax.rsqrt(ms + EPS) * g2_ref[...]).astype(o2_ref.dtype)


def _merge(proj, ya, yb, o_f, o_b, hgrn_gain, wa, wb, wc, wo, layer, h, gain2, tm):
    n_tok, d = h.shape
    cw = C_HEADS * HEAD_DIM

    def resident(w):
        return pl.BlockSpec((None,) + w.shape[1:], lambda i: (layer, 0, 0), pipeline_mode=pl.Buffered(1))

    def gate_spec(which):
        return pl.BlockSpec((tm, d), lambda i: (i, which))

    return pl.pallas_call(
        _merge_kernel,
        out_shape=(jax.ShapeDtypeStruct((n_tok, d), F32), jax.ShapeDtypeStruct((n_tok, d), BF16)),
        grid=(n_tok // tm,),
        in_specs=[
            gate_spec(0), gate_spec(1), gate_spec(2),
            pl.BlockSpec((tm, cw), lambda i: (i, COL_C_G // C_HEADS)),
            pl.BlockSpec((tm, ya.shape[1]), lambda i: (i, 0)),
            pl.BlockSpec((tm, cw), lambda i: (i, 0)),
            pl.BlockSpec((tm, cw), lambda i: (i, 0)),
            pl.BlockSpec((tm, cw), lambda i: (i, 0)),
            pl.BlockSpec((1, cw), lambda i: (0, 0)),
            resident(wa), resident(wb), resident(wc), resident(wo),
            pl.BlockSpec((tm, d), lambda i: (i, 0)),
            pl.BlockSpec((1, d), lambda i: (0, 0)),
        ],
        out_specs=(pl.BlockSpec((tm, d), lambda i: (i, 0)), pl.BlockSpec((tm, d), lambda i: (i, 0))),
        compiler_params=_cparams(("parallel",)),
        name="merge_out",
    )(proj, proj, proj, proj, ya, yb, o_f, o_b, hgrn_gain.reshape(1, cw), wa, wb, wc, wo, h,
      gain2.reshape(1, d))


def _tile(n, candidates):
    return next(t for t in candidates if n % t == 0)


def _trunk(xp, xs, segs, p):
    depth = p["w_in"].shape[0]
    n_prompt, n_tok = xp.shape[0], xp.shape[0] + xs.shape[0]
    tm_up = _tile(n_tok, (3072, 2048, 1024, 512, 256, 128))
    tm_join = next(t for t in (512, 256, 128) if n_prompt % t == 0 and n_tok % t == 0)
    tm_down = _tile(n_tok, (512, 384, 256, 128))
    tm_down_last = next(t for t in (256, 128) if n_prompt % t == 0 and n_tok % t == 0)
    tm_proj = _tile(n_tok, (4096, 2048, 1024, 512, 256, 128))
    tm_merge = _tile(n_tok, (384, 256, 128))
    d_ff = p["ffn1_w_gate"].shape[-1]
    tf = 512 if d_ff % 512 == 0 else 256
    mix_w = MIX_COLS * HEAD_DIM

    lb_p = jax.nn.softmax(p["hgrn_lb_logits"].astype(F32), axis=1)
    lower_bounds = jnp.cumsum(lb_p, axis=1) - lb_p[:, :1]
    win_tab = _window_bias_table(p["t5_bias"])
    bf = lambda name: p[name].astype(BF16)
    w = {name: bf(name) for name in ("ffn1_w_gate", "ffn1_w_up", "ffn1_w_down", "w_in", "w_branch_a",
                                      "w_branch_b", "w_branch_c", "w_out", "ffn2_w_gate", "ffn2_w_up",
                                      "ffn2_w_down")}
    ffn1_w = (w["ffn1_w_gate"], w["ffn1_w_up"], w["ffn1_w_down"] * 0.5)
    ffn2_w = (w["ffn2_w_gate"], w["ffn2_w_up"], w["ffn2_w_down"] * 0.5)
    x, n = _join_norm(xp, xs, p["ffn1_norm"][0], tm_join)
    for l in range(depth):
        a = _ffn_up(n, ffn1_w, l, tm_up, tf)
        h, u = _ffn_down(a, x, ffn1_w, l, p["mix_norm"][l], tm_down)
        proj = _matmul(u, w["w_in"], l, BF16, tm_proj, 512, mix_w // 512)
        ya, yb = _attention_mixers(proj, p["attn_sink"][l].astype(F32) * LOG2E, win_tab,
                                   _na_bias_table(p["na_bias"][l]), segs)
        o_f = _hgrn_direction(proj, lower_bounds[0, l], segs, reverse=False)
        o_b = _hgrn_direction(proj, lower_bounds[1, l], segs, reverse=True)
        h, n = _merge(proj, ya, yb, o_f, o_b, p["hgrn_norm"][l], w["w_branch_a"], w["w_branch_b"],
                      w["w_branch_c"], w["w_out"], l, h, p["ffn2_norm"][l], tm_merge)
        a = _ffn_up(n, ffn2_w, l, tm_up, tf)
        if l == depth - 1:
            return _ffn_down(a, h, ffn2_w, l, p["final_norm"], tm_down_last, n_prompt)
        x, n = _ffn_down(a, h, ffn2_w, l, p["ffn1_norm"][l + 1], tm_down)


def kernel(x_prompt, x_sample, ffn1_norm, ffn1_w_gate, ffn1_w_up, ffn1_w_down, mix_norm, w_in, attn_sink,
           t5_bias, na_bias, hgrn_lb_logits, hgrn_norm, w_branch_a, w_branch_b, w_branch_c, w_out,
           ffn2_norm, ffn2_w_gate, ffn2_w_up, ffn2_w_down, final_norm):
    params = dict(ffn1_norm=ffn1_norm, ffn1_w_gate=ffn1_w_gate, ffn1_w_up=ffn1_w_up, ffn1_w_down=ffn1_w_down,
                  mix_norm=mix_norm, w_in=w_in, attn_sink=attn_sink, t5_bias=t5_bias, na_bias=na_bias,
                  hgrn_lb_logits=hgrn_lb_logits, hgrn_norm=hgrn_norm, w_branch_a=w_branch_a,
                  w_branch_b=w_branch_b, w_branch_c=w_branch_c, w_out=w_out, ffn2_norm=ffn2_norm,
                  ffn2_w_gate=ffn2_w_gate, ffn2_w_up=ffn2_w_up, ffn2_w_down=ffn2_w_down, final_norm=final_norm)
    d = x_prompt.shape[-1]
    segs = (x_prompt.shape[:2], x_sample.shape[:2])
    for _, t in segs:
        assert t % NA_GROUP == 0 and t // NA_GROUP >= 3 and t // GRID_W >= NA_ROWS and t % WIN_STEP == 0
    y_prompt, y_sample = _trunk(x_prompt.reshape(-1, d), x_sample.reshape(-1, d), segs, params)
    return y_prompt.reshape(x_prompt.shape), y_sample.reshape(x_sample.shape)
```

```python
import functools
import math

import numpy as np
import jax
import jax.numpy as jnp
from jax import lax
from jax.experimental import pallas as pl
from jax.experimental.pallas import tpu as pltpu

F32 = jnp.float32
BF16 = jnp.bfloat16

HEAD_DIM = 128
EPS = 1e-6
A_HEADS = 8
A_KV_HEADS = 2
WINDOW = 128
WIN_BLOCK = 128
T5_BUCKETS = 32
T5_MAX_DIST = 128
B_HEADS = 4
GRID_W = 64
NA_ROWS = 8
NA_COLS = 16
C_HEADS = 4
GATE_COLS = 48
MIX_COLS = 44
COL_A_Q, COL_A_K, COL_A_V = (GATE_COLS + c for c in (0, 8, 10))
COL_B_Q, COL_B_K, COL_B_V = (GATE_COLS + c for c in (12, 16, 20))
COL_C_FF, COL_C_FB, COL_C_I, COL_C_Q, COL_C_G = (GATE_COLS + c for c in (24, 28, 32, 36, 40))

NEG_BIG = -1e30
LOG2E = math.log2(math.e)
V7X_VMEM_LIMIT = 60 * 1024 * 1024
HG_TILE = 128
HG_BLK = 16
NA_GROUP = 4 * GRID_W
WIN_STEP = 2 * WIN_BLOCK
MXU_COLS = 256


def _cparams(sem):
    return pltpu.CompilerParams(dimension_semantics=sem, vmem_limit_bytes=V7X_VMEM_LIMIT)


def _seq_pos(gb, blk, segs):
    (n_seq0, t0), (_, t1) = segs
    n0, n1 = t0 // blk, t1 // blk
    tot0 = n_seq0 * n0
    in0 = gb < tot0
    local = jnp.where(in0, gb % n0, (gb - tot0) % n1)
    return local, jnp.where(in0, n0, n1)


def _normed(y, gain_ref, dtype):
    ms = jnp.mean(y * y, axis=-1, keepdims=True)
    return (y * lax.rsqrt(ms + EPS) * gain_ref[...]).astype(dtype)


def _ffn_gate(n, wg_ref, wu_ref, a_ref):
    for c in range(0, a_ref.shape[1], MXU_COLS):
        cc = slice(c, c + MXU_COLS)
        g = jnp.dot(n, wg_ref[:, cc], preferred_element_type=F32)
        u = jnp.dot(n, wu_ref[:, cc], preferred_element_type=F32)
        a_ref[:, cc] = (g * jax.nn.sigmoid(g) * u).astype(a_ref.dtype)


def _ffn_up_kernel(n_ref, wg_ref, wu_ref, a_ref):
    _ffn_gate(n_ref[...], wg_ref, wu_ref, a_ref)


def _join_norm_kernel(xp_ref, xs_ref, g1_ref, x_ref, n_ref, *, prompt_tiles):
    x = jnp.where(pl.program_id(0) < prompt_tiles, xp_ref[...], xs_ref[...])
    x_ref[...] = x
    n_ref[...] = _normed(x, g1_ref, n_ref.dtype)


def _ffn_down_kernel(a_ref, wd_ref, x_ref, g2_ref, o_ref, o2_ref):
    y = x_ref[...] + jnp.dot(a_ref[...], wd_ref[...], preferred_element_type=F32)
    o_ref[...] = y
    o2_ref[...] = _normed(y, g2_ref, o2_ref.dtype)


def _ffn_down_last_kernel(a_ref, wd_ref, x_ref, g2_ref, yp_ref, ys_ref, *, prompt_tiles):
    i = pl.program_id(0)
    y = x_ref[...] + jnp.dot(a_ref[...], wd_ref[...], preferred_element_type=F32)

    @pl.when(i < prompt_tiles)
    def _():
        yp_ref[...] = _normed(y, g2_ref, yp_ref.dtype)

    @pl.when(i >= prompt_tiles)
    def _():
        ys_ref[...] = _normed(y, g2_ref, ys_ref.dtype)


def _ffn_up(n, weights, layer, tm, tf):
    wg, wu, _ = weights
    d, d_ff = wg.shape[1:]
    n_tok = n.shape[0]
    w_spec = pl.BlockSpec((None, d, tf), lambda i, j: (layer, 0, j))
    return pl.pallas_call(
        _ffn_up_kernel,
        out_shape=jax.ShapeDtypeStruct((n_tok, d_ff), BF16),
        grid=(n_tok // tm, d_ff // tf),
        in_specs=[pl.BlockSpec((tm, d), lambda i, j: (i, 0)), w_spec, w_spec],
        out_specs=pl.BlockSpec((tm, tf), lambda i, j: (i, j)),
        compiler_params=_cparams(("parallel", "arbitrary")),
        name="ffn_up",
    )(n, wg, wu)


def _join_norm(xp, xs, gain1, tm):
    d = xp.shape[1]
    n_prompt, n_tok = xp.shape[0], xp.shape[0] + xs.shape[0]
    assert n_prompt % tm == 0
    pt = n_prompt // tm
    tok_spec = pl.BlockSpec((tm, d), lambda i: (i, 0))
    return pl.pallas_call(
        functools.partial(_join_norm_kernel, prompt_tiles=pt),
        out_shape=(jax.ShapeDtypeStruct((n_tok, d), F32), jax.ShapeDtypeStruct((n_tok, d), BF16)),
        grid=(n_tok // tm,),
        in_specs=[pl.BlockSpec((tm, d), lambda i: (jnp.minimum(i, pt - 1), 0)),
                  pl.BlockSpec((tm, d), lambda i: (jnp.maximum(i - pt, 0), 0)),
                  pl.BlockSpec((1, d), lambda i: (0, 0))],
        out_specs=(tok_spec, tok_spec),
        compiler_params=_cparams(("arbitrary",)),
        name="join_norm",
    )(xp, xs, gain1.reshape(1, d))


def _ffn_down(a, x, weights, layer, gain2, tm, n_prompt=None):
    wd = weights[2]
    d_ff, d = wd.shape[1:]
    n_tok = x.shape[0]
    tok = lambda width: pl.BlockSpec((tm, width), lambda i: (i, 0))
    in_specs = [tok(d_ff), pl.BlockSpec((None, d_ff, d), lambda i: (layer, 0, 0), pipeline_mode=pl.Buffered(1)),
                tok(d), pl.BlockSpec((1, d), lambda i: (0, 0))]
    if n_prompt is None:
        body = _ffn_down_kernel
        out_shape = (jax.ShapeDtypeStruct((n_tok, d), F32), jax.ShapeDtypeStruct((n_tok, d), BF16))
        out_specs = (tok(d), tok(d))
    else:
        assert n_prompt % tm == 0
        pt = n_prompt // tm
        body = functools.partial(_ffn_down_last_kernel, prompt_tiles=pt)
        out_shape = (jax.ShapeDtypeStruct((n_prompt, d), F32), jax.ShapeDtypeStruct((n_tok - n_prompt, d), F32))
        out_specs = (pl.BlockSpec((tm, d), lambda i: (jnp.minimum(i, pt - 1), 0)),
                     pl.BlockSpec((tm, d), lambda i: (jnp.maximum(i - pt, 0), 0)))
    return pl.pallas_call(
        body,
        out_shape=out_shape,
        grid=(n_tok // tm,),
        in_specs=in_specs,
        out_specs=out_specs,
        compiler_params=_cparams(("arbitrary",)),
        name="ffn_down",
    )(a, wd, x, gain2.reshape(1, d))


def _matmul_kernel(a_ref, w_ref, o_ref):
    o_ref[...] = jnp.dot(a_ref[...], w_ref[...], preferred_element_type=F32).astype(o_ref.dtype)


def _matmul(a, w, layer, out_dtype, tm, tn, rotate):
    n_tok, k = a.shape
    n_out = w.shape[-1]
    n_col = n_out // tn
    return pl.pallas_call(
        _matmul_kernel,
        out_shape=jax.ShapeDtypeStruct((n_tok, n_out), out_dtype),
        grid=(n_tok // tm, n_out // tn),
        in_specs=[pl.BlockSpec((tm, k), lambda i, j: (i, 0)),
                  pl.BlockSpec((None, k, tn), lambda i, j: (layer, 0, (j + rotate) % n_col))],
        out_specs=pl.BlockSpec((tm, tn), lambda i, j: (i, j)),
        compiler_params=_cparams(("parallel", "arbitrary")),
        name="proj_in",
    )(a, w)


def _wattn_phases(sink_ref, q_ref, kp_ref, kc_ref, kn_ref, vp_ref, vc_ref, vn_ref, bias_ref, o_ref, s_ref, e_ref, segs):
    local, n_loc = _seq_pos(pl.program_id(0), WIN_STEP, segs)
    variant = (jnp.where(local == 0, 0, 1), jnp.where(local == n_loc - 1, 2, 1))
    group = A_HEADS // A_KV_HEADS
    gw = group * WIN_BLOCK
    scale = HEAD_DIM ** -0.5 * LOG2E
    units = [(blk, kv) for blk in range(WIN_STEP // WIN_BLOCK) for kv in range(A_KV_HEADS)]

    def keys_of(p_ref, c_ref, n_ref, blk, kv):
        cs = slice(kv * HEAD_DIM, (kv + 1) * HEAD_DIM)
        cat = jnp.concatenate([p_ref[:, cs], c_ref[:, cs], n_ref[:, cs]], axis=0)
        return cat[blk * WIN_BLOCK:(blk + 3) * WIN_BLOCK]

    def scores():
        for u, (blk, kv) in enumerate(units):
            rows = slice(blk * WIN_BLOCK, (blk + 1) * WIN_BLOCK)
            qs = jnp.concatenate([q_ref[rows, h * HEAD_DIM:(h + 1) * HEAD_DIM]
                                  for h in range(kv * group, (kv + 1) * group)], axis=0)
            s = lax.dot_general(qs, keys_of(kp_ref, kc_ref, kn_ref, blk, kv), (((1,), (1,)), ((), ())),
                                preferred_element_type=F32)
            s_ref[u] = s * scale + bias_ref[variant[blk], kv * gw:(kv + 1) * gw, :]

    def outputs():
        for u, (blk, kv) in enumerate(units):
            rows = slice(blk * WIN_BLOCK, (blk + 1) * WIN_BLOCK)
            dens = []
            for i in range(group):
                hr = slice(i * WIN_BLOCK, (i + 1) * WIN_BLOCK)
                sh = s_ref[u, hr, :]
                sink = sink_ref[kv * group + i]
                m = jnp.maximum(jnp.max(sh, axis=-1, keepdims=True), sink)
                e = jnp.exp2(sh - m)
                dens.append(jnp.sum(e, axis=-1, keepdims=True) + jnp.exp2(sink - m))
                e_ref[u, hr, :] = e.astype(BF16)
            o4 = jnp.dot(e_ref[u], keys_of(vp_ref, vc_ref, vn_ref, blk, kv), preferred_element_type=F32)
            for i in range(group):
                h = kv * group + i
                o_ref[rows, h * HEAD_DIM:(h + 1) * HEAD_DIM] = (
                    o4[i * WIN_BLOCK:(i + 1) * WIN_BLOCK] / dens[i]).astype(o_ref.dtype)

    return scores, outputs


def _window_specs(n_tok, bias_tab):
    nb = n_tok // WIN_BLOCK
    per_step = WIN_STEP // WIN_BLOCK
    kv_w = A_KV_HEADS * HEAD_DIM
    n_units = per_step * A_KV_HEADS
    gw = (A_HEADS // A_KV_HEADS) * WIN_BLOCK

    def kv_specs(col0):
        col = col0 // A_KV_HEADS
        return [pl.BlockSpec((WIN_BLOCK, kv_w), lambda s: (jnp.maximum(per_step * s - 1, 0), col)),
                pl.BlockSpec((WIN_STEP, kv_w), lambda s: (s, col)),
                pl.BlockSpec((WIN_BLOCK, kv_w), lambda s: (jnp.minimum(per_step * (s + 1), nb - 1), col))]

    in_specs = [pl.BlockSpec(memory_space=pltpu.SMEM),
                pl.BlockSpec((WIN_STEP, A_HEADS * HEAD_DIM), lambda s: (s, COL_A_Q // A_HEADS)),
                *kv_specs(COL_A_K), *kv_specs(COL_A_V),
                pl.BlockSpec(bias_tab.shape, lambda s: (0, 0, 0))]
    out_spec = pl.BlockSpec((WIN_STEP, A_HEADS * HEAD_DIM), lambda s: (s, 0))
    scratch = [pltpu.VMEM((n_units, gw, 3 * WIN_BLOCK), F32), pltpu.VMEM((n_units, gw, 3 * WIN_BLOCK), BF16)]
    return in_specs, out_spec, scratch


def _t5_bucket(rel):
    nb = T5_BUCKETS // 2
    max_exact = nb // 2
    base = jnp.where(rel > 0, nb, 0)
    n = jnp.abs(rel)
    large = max_exact + (jnp.log(jnp.maximum(n, 1).astype(F32) / max_exact)
                         / math.log(T5_MAX_DIST / max_exact) * (nb - max_exact)).astype(jnp.int32)
    large = jnp.minimum(large, nb - 1)
    return base + jnp.where(n < max_exact, n, large)


def _window_bias_table(t5_bias):
    qi = jnp.arange(WIN_BLOCK)[:, None]
    si = jnp.arange(3 * WIN_BLOCK)[None, :]
    rel = si - WIN_BLOCK - qi
    onehot = (_t5_bucket(rel)[None] == jnp.arange(T5_BUCKETS)[:, None, None]).astype(F32)
    bias = jnp.einsum('bh,bqs->hqs', t5_bias.astype(F32), onehot, precision=lax.Precision.HIGHEST)
    band = jnp.abs(rel) <= WINDOW
    variants = []
    for lo_ok, hi_ok in ((False, True), (True, True), (True, False)):
        ok = band & ((si >= WIN_BLOCK) | lo_ok) & ((si < 2 * WIN_BLOCK) | hi_ok)
        variants.append(jnp.where(ok[None], bias * LOG2E, NEG_BIG).reshape(A_HEADS * WIN_BLOCK, 3 * WIN_BLOCK))
    return jnp.stack(variants)


def _na_window(g, segs):
    lg, ng = _seq_pos(g, NA_GROUP, segs)
    return lg, ng, jnp.clip(lg - 1, 0, ng - 3)


def _na_phases(q_ref, k0_ref, k1_ref, k2_ref, v0_ref, v1_ref, v2_ref, bias_ref, o_ref, s_ref):
    scale = HEAD_DIM ** -0.5 * LOG2E
    heads = [slice(h * HEAD_DIM, (h + 1) * HEAD_DIM) for h in range(B_HEADS)]

    def scores():
        for h, cs in enumerate(heads):
            kcat = jnp.concatenate([k0_ref[:, cs], k1_ref[:, cs], k2_ref[:, cs]], axis=0)
            s = lax.dot_general(q_ref[:, cs], kcat, (((1,), (1,)), ((), ())), preferred_element_type=F32)
            s_ref[h] = s * scale + bias_ref[h]

    def outputs():
        for h, cs in enumerate(heads):
            vcat = jnp.concatenate([v0_ref[:, cs], v1_ref[:, cs], v2_ref[:, cs]], axis=0)
            s = s_ref[h]
            m = jnp.max(s, axis=-1, keepdims=True)
            e = jnp.exp2(s - m)
            den = jnp.sum(e, axis=-1, keepdims=True)
            oh = jnp.dot(e.astype(BF16), vcat, preferred_element_type=F32)
            o_ref[:, cs] = (oh / den).astype(o_ref.dtype)

    return scores, outputs


def _na_specs(segs):
    width = B_HEADS * HEAD_DIM

    def kv_spec(col0, j):
        def index_map(g):
            lg, _, lo = _na_window(g, segs)
            return (g - lg + lo + j, col0 // B_HEADS)
        return pl.BlockSpec((NA_GROUP, width), index_map)

    def bias_map(g):
        lg, ng, _ = _na_window(g, segs)
        return (jnp.where(lg == 0, 0, jnp.where(lg == ng - 1, 2, 1)), 0, 0, 0)

    in_specs = [pl.BlockSpec((NA_GROUP, width), lambda g: (g, COL_B_Q // B_HEADS)),
                kv_spec(COL_B_K, 0), kv_spec(COL_B_K, 1), kv_spec(COL_B_K, 2),
                kv_spec(COL_B_V, 0), kv_spec(COL_B_V, 1), kv_spec(COL_B_V, 2),
                pl.BlockSpec((None, B_HEADS, NA_GROUP, 3 * NA_GROUP), bias_map)]
    out_spec = pl.BlockSpec((NA_GROUP, width), lambda g: (g, 0))
    scratch = [pltpu.VMEM((B_HEADS, NA_GROUP, 3 * NA_GROUP), F32)]
    return in_specs, out_spec, scratch


N_WIN_IN, N_NA_IN = 9, 8


def _attention_kernel(*refs, segs):
    win_in, na_in = refs[:N_WIN_IN], refs[N_WIN_IN:N_WIN_IN + N_NA_IN]
    ya_ref, yb_ref, ws_ref, we_ref, ns_ref = refs[N_WIN_IN + N_NA_IN:]
    win_scores, win_outputs = _wattn_phases(*win_in, ya_ref, ws_ref, we_ref, segs)
    na_scores, na_outputs = _na_phases(*na_in, yb_ref, ns_ref)
    win_scores()
    na_scores()
    win_outputs()
    na_outputs()


def _attention_mixers(proj, sink, win_tab, na_tab, segs):
    assert WIN_STEP == NA_GROUP
    n_tok = proj.shape[0]
    win_specs, win_out, win_scratch = _window_specs(n_tok, win_tab)
    na_specs, na_out, na_scratch = _na_specs(segs)
    return pl.pallas_call(
        functools.partial(_attention_kernel, segs=segs),
        out_shape=(jax.ShapeDtypeStruct((n_tok, A_HEADS * HEAD_DIM), BF16),
                   jax.ShapeDtypeStruct((n_tok, B_HEADS * HEAD_DIM), BF16)),
        grid=(n_tok // WIN_STEP,),
        in_specs=[*win_specs, *na_specs],
        out_specs=(win_out, na_out),
        scratch_shapes=[*win_scratch, *na_scratch],
        compiler_params=_cparams(("parallel",)),
        name="attn_ab",
    )(sink, *([proj] * 7), win_tab, *([proj] * 7), na_tab)


def _na_bias_table(rel_table):
    rows_q = NA_GROUP // GRID_W
    rows_k = 3 * rows_q
    c = np.arange(GRID_W)
    col_start = np.clip(c - NA_COLS // 2, 0, GRID_W - NA_COLS)
    col_ok = (c[None, :] >= col_start[:, None]) & (c[None, :] < col_start[:, None] + NA_COLS)
    dc = np.clip(c[None, :] - c[:, None], -(NA_COLS - 1), NA_COLS - 1) + NA_COLS - 1
    onehot = (dc[None] == np.arange(2 * NA_COLS - 1)[:, None, None]).astype(np.float32)
    by_dr = jnp.einsum('hrd,dqk->hrqk', rel_table.astype(F32), jnp.asarray(onehot),
                       precision=lax.Precision.HIGHEST)
    by_dr = jnp.where(jnp.asarray(col_ok)[None, None], by_dr * LOG2E, NEG_BIG)
    by_dr = jnp.concatenate([by_dr, jnp.full_like(by_dr[:, :1], NEG_BIG)], axis=1)
    cfgs = ([(0, NA_ROWS - 1 - a) for a in range(rows_q)],
            [(a, NA_ROWS // 2 - 1) for a in range(rows_q)],
            [(rows_q, NA_ROWS // 2 - 1 - a) for a in range(rows_q)])
    idx = np.full((3, rows_q, rows_k), 2 * NA_ROWS - 1, np.int32)
    for ci, cfg in enumerate(cfgs):
        for a, (off, dr0) in enumerate(cfg):
            for j in range(NA_ROWS):
                idx[ci, a, off + j] = dr0 + j
    tab = jnp.take(by_dr, jnp.asarray(idx.reshape(-1)), axis=1)
    tab = tab.reshape(B_HEADS, 3, rows_q, rows_k, GRID_W, GRID_W).transpose(1, 0, 2, 4, 3, 5)
    return tab.reshape(3, B_HEADS, NA_GROUP, 3 * NA_GROUP)


def _split3(x):
    hi = x.astype(BF16)
    r1 = x - hi.astype(F32)
    mid = r1.astype(BF16)
    lo = (r1 - mid.astype(F32)).astype(BF16)
    return hi, mid, lo


def _hgrn_kernel(z_ref, v_ref, q_ref, lb_ref, scan_ref, o_ref, st_ref, c3_ref, b3_ref, k3_ref, q3_ref, a_ref,
                 *, segs, reverse):
    i = pl.program_id(0)
    n_steps = pl.num_programs(0)
    n_sub = z_ref.shape[0] // HG_TILE
    step = (n_steps - 1 - i) if reverse else i
    local, n_loc = _seq_pos(step, n_sub * HG_TILE, segs)
    is_start = (local == n_loc - 1) if reverse else (local == 0)

    @pl.when(is_start)
    def _():
        st_ref[...] = jnp.zeros_like(st_ref)

    nblk = HG_TILE // HG_BLK
    half = HG_BLK // 2
    n_sc = nblk // 2
    log2e = math.log2(math.e)
    nt = (((1,), (1,)), ((), ()))

    def sel_index(t0):
        shape = (nblk, half, HG_TILE)
        t = lax.broadcasted_iota(jnp.int32, shape, 1) + t0
        j = lax.broadcasted_iota(jnp.int32, shape, 2) - lax.broadcasted_iota(jnp.int32, shape, 0) * HG_BLK
        ok = (j >= 0) & (j < HG_BLK) & ((j >= t) if reverse else (j <= t))
        return jnp.where(ok, j, -1)

    sel = (sel_index(0), sel_index(half))
    rb = lax.broadcasted_iota(jnp.int32, (HG_TILE, HG_TILE), 0) // HG_BLK
    cb = lax.broadcasted_iota(jnp.int32, (HG_TILE, HG_TILE), 1) // HG_BLK
    cross = ((rb % 2 == 0) & (cb == rb + 1)) if reverse else ((rb % 2 == 1) & (cb == rb - 1))
    blk_odd = (lax.broadcasted_iota(jnp.int32, (nblk, 1, HEAD_DIM), 0) % 2) == 1
    edge_row = 0 if reverse else HG_BLK - 1
    scan = scan_ref[...]

    def one_tile(t, carry):
        tile = (n_sub - 1 - t) if reverse else t
        tok = pl.ds(pl.multiple_of(tile * HG_TILE, HG_TILE), HG_TILE)
        heads = [slice(h * HEAD_DIM, (h + 1) * HEAD_DIM) for h in range(C_HEADS)]

        for h, cs in enumerate(heads):
            lb = lb_ref[:, cs]
            f = lb + (1.0 - lb) * jax.nn.sigmoid(z_ref[tok, cs].astype(F32))
            k3 = (1.0 - f).reshape(nblk, HG_BLK, HEAD_DIM)
            pieces = jnp.concatenate(_split3(jnp.log(f)), axis=1)
            bm = jnp.dot(scan, pieces, preferred_element_type=F32)
            b = bm[:, :HEAD_DIM] + bm[:, HEAD_DIM:2 * HEAD_DIM] + bm[:, 2 * HEAD_DIM:]
            b3 = (b * log2e).reshape(nblk, HG_BLK, HEAD_DIM)
            k3_ref[h] = k3
            b3_ref[h] = b3
            c3_ref[h] = b3 - jnp.log2(jnp.maximum(k3, 0.0))
            q3_ref[h] = (q_ref[tok, cs].astype(F32) * (HEAD_DIM ** -0.5)).reshape(nblk, HG_BLK, HEAD_DIM)

        for h in range(C_HEADS):
            halves = ((q3_ref[h, :, :half, :], b3_ref[h, :, :half, :]), (q3_ref[h, :, half:, :], b3_ref[h, :, half:, :]))
            a_half = [jnp.zeros((nblk, half, HG_TILE), F32), jnp.zeros((nblk, half, HG_TILE), F32)]
            for j in range(HG_BLK):
                cj = c3_ref[h, :, j:j + 1, :]
                for hi in range(2):
                    if (j < half * hi) if reverse else (j > half * hi + half - 1):
                        continue
                    qx, bx = halves[hi]
                    p = qx * jnp.exp2(bx - cj)
                    r = jnp.sum(p, axis=-1, keepdims=True)
                    a_half[hi] = jnp.where(sel[hi] == j, r, a_half[hi])
            a_ref[h] = jnp.concatenate(a_half, axis=1).reshape(HG_TILE, HG_TILE)

        for h, cs in enumerate(heads):
            q3, k3, b3 = q3_ref[h], k3_ref[h], b3_ref[h]
            btot = b3_ref[h, :, edge_row:edge_row + 1, :]
            v = v_ref[tok, cs]
            qp3 = q3 * jnp.exp2(b3)
            kp3 = k3 * jnp.exp2(btot - b3)
            qp = qp3.reshape(HG_TILE, HEAD_DIM).astype(BF16)
            kp = kp3.reshape(HG_TILE, HEAD_DIM).astype(BF16)
            g = lax.dot_general(qp, kp, nt, preferred_element_type=F32)
            a_all = jnp.where(cross, g, a_ref[h]).astype(BF16)
            o_acc = jnp.dot(a_all, v, preferred_element_type=F32)

            dec = jnp.exp2(btot)
            one = jnp.ones_like(dec[:1])
            dprev = jnp.concatenate([one, dec[:-1]], axis=0)
            dnext = jnp.concatenate([dec[1:], one], axis=0)
            if reverse:
                qscale, kscale = jnp.where(blk_odd, 1.0, dnext), jnp.where(blk_odd, dprev, 1.0)
            else:
                qscale, kscale = jnp.where(blk_odd, dprev, 1.0), jnp.where(blk_odd, 1.0, dnext)
            qpp = (qp3 * qscale).reshape(HG_TILE, HEAD_DIM).astype(BF16)
            kpp = (kp3 * kscale).reshape(HG_TILE, HEAD_DIM).astype(BF16)
            st = st_ref[h]
            inter = [None] * n_sc
            for m in (range(n_sc - 1, -1, -1) if reverse else range(n_sc)):
                rows = slice(2 * m * HG_BLK, 2 * (m + 1) * HG_BLK)
                inter[m] = lax.dot_general(qpp[rows], st.astype(BF16), nt, preferred_element_type=F32)
                upd = lax.dot_general(v[rows], kpp[rows], (((0,), (0,)), ((), ())), preferred_element_type=F32)
                st = st * (dec[2 * m] * dec[2 * m + 1]) + upd
            st_ref[h] = st
            o_ref[tok, cs] = o_acc + jnp.concatenate(inter, axis=0)
        return carry

    lax.fori_loop(0, n_sub, one_tile, 0)


def _hgrn_scan_matrix(reverse):
    t = np.arange(HG_TILE)
    same = (t[:, None] // HG_BLK) == (t[None, :] // HG_BLK)
    incl = (t[None, :] >= t[:, None]) if reverse else (t[None, :] <= t[:, None])
    return jnp.asarray((same & incl).astype(np.float32), BF16)


def _hgrn_direction(proj, lower_bound, segs, reverse):
    n_tok = proj.shape[0]
    step_tok = next(s for s in (8 * HG_TILE, 4 * HG_TILE, 2 * HG_TILE, HG_TILE) if all(t % s == 0 for _, t in segs))
    n_steps = n_tok // step_tok
    width = C_HEADS * HEAD_DIM
    scan = _hgrn_scan_matrix(reverse)

    def tok_spec(col0):
        return pl.BlockSpec((step_tok, width),
                            lambda i: ((n_steps - 1 - i) if reverse else i, col0 // C_HEADS))

    nblk = HG_TILE // HG_BLK
    return pl.pallas_call(
        functools.partial(_hgrn_kernel, segs=segs, reverse=reverse),
        out_shape=jax.ShapeDtypeStruct((n_tok, width), F32),
        grid=(n_steps,),
        in_specs=[
            tok_spec(COL_C_FB if reverse else COL_C_FF), tok_spec(COL_C_I), tok_spec(COL_C_Q),
            pl.BlockSpec((1, width), lambda i: (0, 0)),
            pl.BlockSpec(scan.shape, lambda i: (0, 0)),
        ],
        out_specs=pl.BlockSpec((step_tok, width), lambda i: ((n_steps - 1 - i) if reverse else i, 0)),
        scratch_shapes=[pltpu.VMEM((C_HEADS, HEAD_DIM, HEAD_DIM), F32)]
        + [pltpu.VMEM((C_HEADS, nblk, HG_BLK, HEAD_DIM), F32)] * 4
        + [pltpu.VMEM((C_HEADS, HG_TILE, HG_TILE), F32)],
        compiler_params=_cparams(("arbitrary",)),
        name="hgrn_bwd" if reverse else "hgrn_fwd",
    )(proj, proj, proj, lower_bound.reshape(1, width), scan)


def _merge_kernel(ga_ref, gb_ref, gc_ref, cg_ref, ya_ref, yb_ref, of_ref, ob_ref, hn_ref,
                  wa_ref, wb_ref, wc_ref, wo_ref, h_ref, g2_ref, o_ref, o2_ref, m_ref):
    o = of_ref[...] + ob_ref[...]
    heads = []
    for h in range(C_HEADS):
        oh = o[:, h * HEAD_DIM:(h + 1) * HEAD_DIM]
        ms = jnp.mean(oh * oh, axis=-1, keepdims=True)
        heads.append(oh * lax.rsqrt(ms + EPS))
    cg = cg_ref[...].astype(F32)
    yc = (jnp.concatenate(heads, axis=1) * hn_ref[...] * (cg * jax.nn.sigmoid(cg))).astype(BF16)
    ya, yb = ya_ref[...], yb_ref[...]
    for c in range(0, m_ref.shape[1], MXU_COLS):
        cc = slice(c, c + MXU_COLS)
        m = jax.nn.sigmoid(ga_ref[:, cc].astype(F32)) * jnp.dot(ya, wa_ref[:, cc], preferred_element_type=F32)
        m += jax.nn.sigmoid(gb_ref[:, cc].astype(F32)) * jnp.dot(yb, wb_ref[:, cc], preferred_element_type=F32)
        m += jax.nn.sigmoid(gc_ref[:, cc].astype(F32)) * jnp.dot(yc, wc_ref[:, cc], preferred_element_type=F32)
        m_ref[:, cc] = m.astype(BF16)
    y = h_ref[...] + jnp.dot(m_ref[...], wo_ref[...], preferred_element_type=F32)
    o_ref[...] = y
    ms = jnp.mean(y * y, axis=-1, keepdims=True)
    o2_ref[...] = (y * lax.rsqrt(ms + EPS) * g2_ref[...]).astype(o2_ref.dtype)


def _merge(proj, ya, yb, o_f, o_b, hgrn_gain, wa, wb, wc, wo, layer, h, gain2, tm):
    n_tok, d = h.shape
    cw = C_HEADS * HEAD_DIM

    def resident(w):
        return pl.BlockSpec((None,) + w.shape[1:], lambda i: (layer, 0, 0), pipeline_mode=pl.Buffered(1))

    def gate_spec(which):
        return pl.BlockSpec((tm, d), lambda i: (i, which))

    return pl.pallas_call(
        _merge_kernel,
        out_shape=(jax.ShapeDtypeStruct((n_tok, d), F32), jax.ShapeDtypeStruct((n_tok, d), BF16)),
        grid=(n_tok // tm,),
        in_specs=[
            gate_spec(0), gate_spec(1), gate_spec(2),
            pl.BlockSpec((tm, cw), lambda i: (i, COL_C_G // C_HEADS)),
            pl.BlockSpec((tm, ya.shape[1]), lambda i: (i, 0)),
            pl.BlockSpec((tm, cw), lambda i: (i, 0)),
            pl.BlockSpec((tm, cw), lambda i: (i, 0)),
            pl.BlockSpec((tm, cw), lambda i: (i, 0)),
            pl.BlockSpec((1, cw), lambda i: (0, 0)),
            resident(wa), resident(wb), resident(wc), resident(wo),
            pl.BlockSpec((tm, d), lambda i: (i, 0)),
            pl.BlockSpec((1, d), lambda i: (0, 0)),
        ],
        out_specs=(pl.BlockSpec((tm, d), lambda i: (i, 0)), pl.BlockSpec((tm, d), lambda i: (i, 0))),
        scratch_shapes=[pltpu.VMEM((tm, d), BF16)],
        compiler_params=_cparams(("parallel",)),
        name="merge_out",
    )(proj, proj, proj, proj, ya, yb, o_f, o_b, hgrn_gain.reshape(1, cw), wa, wb, wc, wo, h,
      gain2.reshape(1, d))


def _tile(n, candidates):
    return next(t for t in candidates if n % t == 0)


def _trunk(xp, xs, segs, p):
    depth = p["w_in"].shape[0]
    n_prompt, n_tok = xp.shape[0], xp.shape[0] + xs.shape[0]
    tm_up = _tile(n_tok, (3072, 2048, 1024, 512, 256, 128))
    tm_join = next(t for t in (512, 256, 128) if n_prompt % t == 0 and n_tok % t == 0)
    tm_down = _tile(n_tok, (512, 384, 256, 128))
    tm_down_last = next(t for t in (256, 128) if n_prompt % t == 0 and n_tok % t == 0)
    tm_proj = _tile(n_tok, (4096, 2048, 1024, 512, 256, 128))
    tm_merge = _tile(n_tok, (384, 256, 128))
    d_ff = p["ffn1_w_gate"].shape[-1]
    tf = 512 if d_ff % 512 == 0 else 256
    mix_w = MIX_COLS * HEAD_DIM

    lb_p = jax.nn.softmax(p["hgrn_lb_logits"].astype(F32), axis=1)
    lower_bounds = jnp.cumsum(lb_p, axis=1) - lb_p[:, :1]
    win_tab = _window_bias_table(p["t5_bias"])
    bf = lambda name: p[name].astype(BF16)
    w = {name: bf(name) for name in ("ffn1_w_gate", "ffn1_w_up", "ffn1_w_down", "w_in", "w_branch_a",
                                      "w_branch_b", "w_branch_c", "w_out", "ffn2_w_gate", "ffn2_w_up",
                                      "ffn2_w_down")}
    ffn1_w = (w["ffn1_w_gate"], w["ffn1_w_up"], w["ffn1_w_down"] * 0.5)
    ffn2_w = (w["ffn2_w_gate"], w["ffn2_w_up"], w["ffn2_w_down"] * 0.5)
    x, n = _join_norm(xp, xs, p["ffn1_norm"][0], tm_join)
    for l in range(depth):
        a = _ffn_up(n, ffn1_w, l, tm_up, tf)
        h, u = _ffn_down(a, x, ffn1_w, l, p["mix_norm"][l], tm_down)
        proj = _matmul(u, w["w_in"], l, BF16, tm_proj, 512, mix_w // 512)
        ya, yb = _attention_mixers(proj, p["attn_sink"][l].astype(F32) * LOG2E, win_tab,
                                   _na_bias_table(p["na_bias"][l]), segs)
        o_f = _hgrn_direction(proj, lower_bounds[0, l], segs, reverse=False)
        o_b = _hgrn_direction(proj, lower_bounds[1, l], segs, reverse=True)
        h, n = _merge(proj, ya, yb, o_f, o_b, p["hgrn_norm"][l], w["w_branch_a"], w["w_branch_b"],
                      w["w_branch_c"], w["w_out"], l, h, p["ffn2_norm"][l], tm_merge)
        a = _ffn_up(n, ffn2_w, l, tm_up, tf)
        if l == depth - 1:
            return _ffn_down(a, h, ffn2_w, l, p["final_norm"], tm_down_last, n_prompt)
        x, n = _ffn_down(a, h, ffn2_w, l, p["ffn1_norm"][l + 1], tm_down)


def kernel(x_prompt, x_sample, ffn1_norm, ffn1_w_gate, ffn1_w_up, ffn1_w_down, mix_norm, w_in, attn_sink,
           t5_bias, na_bias, hgrn_lb_logits, hgrn_norm, w_branch_a, w_branch_b, w_branch_c, w_out,
           ffn2_norm, ffn2_w_gate, ffn2_w_up, ffn2_w_down, final_norm):
    params = dict(ffn1_norm=ffn1_norm, ffn1_w_gate=ffn1_w_gate, ffn1_w_up=ffn1_w_up, ffn1_w_down=ffn1_w_down,
                  mix_norm=mix_norm, w_in=w_in, attn_sink=attn_sink, t5_bias=t5_bias, na_bias=na_bias,
                  hgrn_lb_logits=hgrn_lb_logits, hgrn_norm=hgrn_norm, w_branch_a=w_branch_a,
                  w_branch_b=w_branch_b, w_branch_c=w_branch_c, w_out=w_out, ffn2_norm=ffn2_norm,
                  ffn2_w_gate=ffn2_w_gate, ffn2_w_up=ffn2_w_up, ffn2_w_down=ffn2_w_down, final_norm=final_norm)
    d = x_prompt.shape[-1]
    segs = (x_prompt.shape[:2], x_sample.shape[:2])
    for _, t in segs:
        assert t % NA_GROUP == 0 and t // NA_GROUP >= 3 and t // GRID_W >= NA_ROWS and t % WIN_STEP == 0
    y_prompt, y_sample = _trunk(x_prompt.reshape(-1, d), x_sample.reshape(-1, d), segs, params)
    return y_prompt.reshape(x_prompt.shape), y_sample.reshape(x_sample.shape)
```
